```python
import jax, jax.numpy as jnp
from jax import lax
import numpy as np

D_MODEL = 1024
BATCH = 4
SEQ = 8192
DEPTH = 2

GLA_HEADS = 4
GLA_DK = D_MODEL // (2 * GLA_HEADS)
GLA_DV = D_MODEL // GLA_HEADS
GLA_RANK = 16
GLA_TAU = 16.0
GLA_CHUNK = 64
MLSTM_HEADS = 4
MLSTM_INNER = D_MODEL
MLSTM_DH = MLSTM_INNER // MLSTM_HEADS
MLSTM_CONV = 4
MLSTM_BLOCK = 4
MLSTM_NBLK = MLSTM_INNER // MLSTM_BLOCK
MLSTM_CHUNK = 64
FOX_HEADS = 8
FOX_DH = D_MODEL // FOX_HEADS
FOX_QBLOCK = 128
N_BRANCH = 3
D_FF = ((8 * D_MODEL // 3 + 255) // 256) * 256
FFN_CONV = 3
EPS = 1e-6

IN_SPLITS = (
    GLA_HEADS * GLA_DK, GLA_HEADS * GLA_DK, GLA_HEADS * GLA_DV, GLA_RANK, GLA_HEADS * GLA_DV,
    MLSTM_INNER, MLSTM_INNER,
    FOX_HEADS * FOX_DH, FOX_HEADS * FOX_DH, FOX_HEADS * FOX_DH, FOX_HEADS, FOX_HEADS * FOX_DH,
    N_BRANCH * D_MODEL,
)
N_IN = sum(IN_SPLITS)

kernel_name = 'hybrid_gla_mlstm_fox_convffn'


def rmsnorm(x, g):
    xf = x.astype(jnp.float32)
    y = xf * lax.rsqrt(jnp.mean(xf * xf, axis=-1, keepdims=True) + EPS)
    return (y * g.astype(jnp.float32)).astype(x.dtype)


def head_rmsnorm(x, g, n_heads):
    b, t, w = x.shape
    xh = x.astype(jnp.float32).reshape(b, t, n_heads, w // n_heads)
    y = xh * lax.rsqrt(jnp.mean(xh * xh, axis=-1, keepdims=True) + EPS)
    return y.reshape(b, t, w) * g.astype(jnp.float32)


def causal_dwconv(x, w, b):
    width = w.shape[0]
    t = x.shape[1]
    xp = jnp.pad(x, ((0, 0), (width - 1, 0), (0, 0)))
    y = b
    for j in range(width):
        y = y + w[j] * xp[:, j:j + t]
    return y


def to_chunks(t, n_heads, chunk):
    b, s, w = t.shape
    return t.reshape(b, s // chunk, chunk, n_heads, w // n_heads).transpose(1, 0, 3, 2, 4).astype(jnp.float32)


def from_chunks(t):
    n, b, h, l, d = t.shape
    return t.transpose(1, 0, 3, 2, 4).reshape(b, n * l, h * d)


def gla_branch(q, k, v, lr, r, w_lr, b_lr, g_norm):
    bsz = q.shape[0]
    log_a = jax.nn.log_sigmoid((lr @ w_lr + b_lr).astype(jnp.float32)) / GLA_TAU
    qc = to_chunks(q, GLA_HEADS, GLA_CHUNK) * GLA_DK ** -0.5
    kc = to_chunks(k, GLA_HEADS, GLA_CHUNK)
    vc = to_chunks(v, GLA_HEADS, GLA_CHUNK)
    bc = jnp.cumsum(to_chunks(log_a, GLA_HEADS, GLA_CHUNK), axis=3)
    b_last = bc[:, :, :, -1:, :]
    q_in = qc * jnp.exp(bc)
    k_in = kc * jnp.exp(-bc)
    k_st = kc * jnp.exp(b_last - bc)
    decay = jnp.exp(b_last[:, :, :, 0, :])
    causal = jnp.tril(jnp.ones((GLA_CHUNK, GLA_CHUNK), dtype=bool))
    att = jnp.where(causal, jnp.einsum('nbhtd,nbhsd->nbhts', q_in, k_in), 0.0)
    o_intra = jnp.einsum('nbhts,nbhsv->nbhtv', att, vc)

    def step(state, xs):
        q_c, k_c, v_c, d_c = xs
        o = jnp.einsum('bhtd,bhdv->bhtv', q_c, state)
        state = d_c[..., None] * state + jnp.einsum('bhsd,bhsv->bhdv', k_c, v_c)
        return state, o

    s0 = jnp.zeros((bsz, GLA_HEADS, GLA_DK, GLA_DV), jnp.float32)
    _, o_inter = lax.scan(step, s0, (q_in, k_st, vc, decay))
    o = from_chunks(o_intra + o_inter)
    return (head_rmsnorm(o, g_norm, GLA_HEADS) * jax.nn.silu(r.astype(jnp.float32))).astype(q.dtype)


def mlstm_cell(q, k, v, log_i, log_f):
    bsz = q.shape[0]
    qc = to_chunks(q, MLSTM_HEADS, MLSTM_CHUNK) * MLSTM_DH ** -0.5
    kc = to_chunks(k, MLSTM_HEADS, MLSTM_CHUNK)
    vc = to_chunks(v, MLSTM_HEADS, MLSTM_CHUNK)
    ic = to_chunks(log_i, MLSTM_HEADS, MLSTM_CHUNK)[..., 0]
    fc = to_chunks(log_f, MLSTM_HEADS, MLSTM_CHUNK)[..., 0]
    causal = jnp.tril(jnp.ones((MLSTM_CHUNK, MLSTM_CHUNK), dtype=bool))

    def step(carry, xs):
        c_st, n_st, m_st = carry
        q_c, k_c, v_c, i_c, f_c = xs
        b = jnp.cumsum(f_c, axis=-1)
        d_log = jnp.where(causal, b[..., :, None] - b[..., None, :] + i_c[..., None, :], -jnp.inf)
        m_inter = b + m_st[..., None]
        m_t = jnp.maximum(m_inter, jnp.max(d_log, axis=-1))
        w_intra = jnp.exp(d_log - m_t[..., None])
        w_inter = jnp.exp(m_inter - m_t)
        s = jnp.einsum('bhtd,bhsd->bhts', q_c, k_c) * w_intra
        num = jnp.einsum('bhts,bhsv->bhtv', s, v_c) + w_inter[..., None] * jnp.einsum('bhtd,bhdv->bhtv', q_c, c_st)
        qn = jnp.sum(s, axis=-1) + w_inter * jnp.einsum('bhtd,bhd->bht', q_c, n_st)
        h = num / jnp.maximum(jnp.abs(qn), jnp.exp(-m_t))[..., None]
        g = b[..., -1]
        a = g[..., None] - b + i_c
        m_new = jnp.maximum(g + m_st, jnp.max(a, axis=-1))
        wa = jnp.exp(a - m_new[..., None])
        dec = jnp.exp(g + m_st - m_new)
        c_st = dec[..., None, None] * c_st + jnp.einsum('bhs,bhsd,bhsv->bhdv', wa, k_c, v_c)
        n_st = dec[..., None] * n_st + jnp.einsum('bhs,bhsd->bhd', wa, k_c)
        return (c_st, n_st, m_new), h

    c0 = jnp.zeros((bsz, MLSTM_HEADS, MLSTM_DH, MLSTM_DH), jnp.float32)
    n0 = jnp.zeros((bsz, MLSTM_HEADS, MLSTM_DH), jnp.float32)
    m0 = jnp.zeros((bsz, MLSTM_HEADS), jnp.float32)
    _, h = lax.scan(step, (c0, n0, m0), (qc, kc, vc, ic, fc))
    return from_chunks(h)


def mlstm_branch(xm, z, conv_w, conv_b, wq, wk, wv, w_i, b_i, w_f, b_f, skip, g_norm):
    bsz, seq, _ = xm.shape
    xc = jax.nn.silu(causal_dwconv(xm, conv_w, conv_b))

    def headwise(t, w):
        tb = t.reshape(bsz, seq, MLSTM_NBLK, MLSTM_BLOCK)
        return jnp.einsum('btnc,ncd->btnd', tb, w).reshape(bsz, seq, MLSTM_INNER)

    q = headwise(xc, wq)
    k = headwise(xc, wk)
    v = headwise(xm, wv)
    qkv = jnp.concatenate([q, k, v], axis=-1)
    log_i = (qkv @ w_i + b_i).astype(jnp.float32)
    log_f = jax.nn.log_sigmoid((qkv @ w_f + b_f).astype(jnp.float32))
    h = head_rmsnorm(mlstm_cell(q, k, v, log_i, log_f), g_norm, MLSTM_HEADS)
    h = h + skip.astype(jnp.float32) * xc.astype(jnp.float32)
    return (h * jax.nn.silu(z.astype(jnp.float32))).astype(xm.dtype)


def fox_branch(q, k, v, fz, og, b_f):
    bsz, seq, _ = q.shape
    n_blk = seq // FOX_QBLOCK

    def heads(t):
        return t.reshape(bsz, seq, FOX_HEADS, FOX_DH).transpose(0, 2, 1, 3).astype(jnp.float32)

    qh = heads(q) * FOX_DH ** -0.5
    kh = heads(k)
    vh = heads(v)
    cum_f = jnp.cumsum(jax.nn.log_sigmoid(fz.astype(jnp.float32) + b_f), axis=1).transpose(0, 2, 1)
    q_blocks = qh.reshape(bsz, FOX_HEADS, n_blk, FOX_QBLOCK, FOX_DH).transpose(2, 0, 1, 3, 4)
    f_blocks = cum_f.reshape(bsz, FOX_HEADS, n_blk, FOX_QBLOCK).transpose(2, 0, 1, 3)
    key_pos = jnp.arange(seq)

    def attend(args):
        q_blk, f_blk, blk = args
        logits = jnp.einsum('bhqd,bhkd->bhqk', q_blk, kh) + f_blk[..., :, None] - cum_f[:, :, None, :]
        query_pos = blk * FOX_QBLOCK + jnp.arange(FOX_QBLOCK)
        logits = jnp.where(key_pos[None, :] <= query_pos[:, None], logits, -jnp.inf)
        return jnp.einsum('bhqk,bhkd->bhqd', jax.nn.softmax(logits, axis=-1), vh)

    o = lax.map(attend, (q_blocks, f_blocks, jnp.arange(n_blk)))
    o = o.transpose(1, 0, 3, 2, 4).reshape(bsz, seq, FOX_HEADS * FOX_DH)
    return (o * jax.nn.sigmoid(og.astype(jnp.float32))).astype(q.dtype)


def token_mixer(h, w_in, b_gate, gla_w_lr, gla_b_lr, gla_norm, mlstm_conv_w, mlstm_conv_b,
                mlstm_wq, mlstm_wk, mlstm_wv, mlstm_w_i, mlstm_b_i, mlstm_w_f, mlstm_b_f,
                mlstm_skip, mlstm_norm, fox_b_f, w_branch, w_out):
    bsz, seq, _ = h.shape
    proj = h @ w_in
    split_idx = [int(s) for s in np.cumsum(IN_SPLITS)[:-1]]
    (gq, gk, gv, glr, gr, mx, mz, fq, fk, fv, ff, fog, gates) = jnp.split(proj, split_idx, axis=-1)
    y_gla = gla_branch(gq, gk, gv, glr, gr, gla_w_lr, gla_b_lr, gla_norm)
    y_mlstm = mlstm_branch(mx, mz, mlstm_conv_w, mlstm_conv_b, mlstm_wq, mlstm_wk, mlstm_wv,
                           mlstm_w_i, mlstm_b_i, mlstm_w_f, mlstm_b_f, mlstm_skip, mlstm_norm)
    y_fox = fox_branch(fq, fk, fv, ff, fog, fox_b_f)
    ys = jnp.stack([y_gla, y_mlstm, y_fox], axis=2)
    branch = jnp.einsum('btnw,nwd->btnd', ys, w_branch)
    g = jax.nn.sigmoid(gates.reshape(bsz, seq, N_BRANCH, D_MODEL) + b_gate)
    merged = jnp.sum(branch * g, axis=2)
    return merged @ w_out


def conv_ffn(h, w_up, conv_w, conv_b, w_down):
    u = causal_dwconv(h @ w_up, conv_w, conv_b)
    a, g = jnp.split(u, 2, axis=-1)
    return (jax.nn.silu(g) * a) @ w_down


def setup_inputs(seed: int = 0) -> dict:
    key = jax.random.key(seed)
    ks = iter(jax.random.split(key, 32))

    def nrm(shape, scale):
        return jax.random.normal(next(ks), shape, jnp.float32) * scale

    def gain(shape):
        return 1.0 + nrm(shape, 0.02)

    L = DEPTH
    return {
        'x': nrm((BATCH, SEQ, D_MODEL), 1.0),
        'norm_mix': gain((L, D_MODEL)),
        'w_in': nrm((L, D_MODEL, N_IN), D_MODEL ** -0.5),
        'b_gate': nrm((L, N_BRANCH, D_MODEL), 0.1),
        'gla_w_lr': nrm((L, GLA_RANK, GLA_HEADS * GLA_DK), GLA_RANK ** -0.5),
        'gla_b_lr': nrm((L, GLA_HEADS * GLA_DK), 0.1),
        'gla_norm': gain((L, GLA_HEADS * GLA_DV)),
        'mlstm_conv_w': nrm((L, MLSTM_CONV, MLSTM_INNER), MLSTM_CONV ** -0.5),
        'mlstm_conv_b': nrm((L, MLSTM_INNER), 0.02),
        'mlstm_wq': nrm((L, MLSTM_NBLK, MLSTM_BLOCK, MLSTM_BLOCK), MLSTM_BLOCK ** -0.5),
        'mlstm_wk': nrm((L, MLSTM_NBLK, MLSTM_BLOCK, MLSTM_BLOCK), MLSTM_BLOCK ** -0.5),
        'mlstm_wv': nrm((L, MLSTM_NBLK, MLSTM_BLOCK, MLSTM_BLOCK), MLSTM_BLOCK ** -0.5),
        'mlstm_w_i': nrm((L, 3 * MLSTM_INNER, MLSTM_HEADS), (3 * MLSTM_INNER) ** -0.5),
        'mlstm_b_i': nrm((L, MLSTM_HEADS), 0.1),
        'mlstm_w_f': nrm((L, 3 * MLSTM_INNER, MLSTM_HEADS), (3 * MLSTM_INNER) ** -0.5),
        'mlstm_b_f': jnp.linspace(3.0, 6.0, MLSTM_HEADS)[None, :] + nrm((L, MLSTM_HEADS), 0.1),
        'mlstm_skip': gain((L, MLSTM_INNER)),
        'mlstm_norm': gain((L, MLSTM_INNER)),
        'fox_b_f': 2.0 + nrm((L, FOX_HEADS), 0.1),
        'w_branch': nrm((L, N_BRANCH, D_MODEL, D_MODEL), D_MODEL ** -0.5),
        'w_out': nrm((L, D_MODEL, D_MODEL), D_MODEL ** -0.5),
        'norm_ffn': gain((L, D_MODEL)),
        'ffn_w_up': nrm((L, D_MODEL, 2 * D_FF), D_MODEL ** -0.5),
        'ffn_conv_w': nrm((L, FFN_CONV, 2 * D_FF), FFN_CONV ** -0.5),
        'ffn_conv_b': nrm((L, 2 * D_FF), 0.02),
        'ffn_w_down': nrm((L, D_FF, D_MODEL), D_FF ** -0.5),
        'norm_final': gain((D_MODEL,)),
    }


def reference(x, norm_mix, w_in, b_gate, gla_w_lr, gla_b_lr, gla_norm, mlstm_conv_w, mlstm_conv_b,
              mlstm_wq, mlstm_wk, mlstm_wv, mlstm_w_i, mlstm_b_i, mlstm_w_f, mlstm_b_f, mlstm_skip,
              mlstm_norm, fox_b_f, w_branch, w_out, norm_ffn, ffn_w_up, ffn_conv_w, ffn_conv_b,
              ffn_w_down, norm_final):
    for l in range(DEPTH):
        h = rmsnorm(x, norm_mix[l])
        x = x + token_mixer(h, w_in[l], b_gate[l], gla_w_lr[l], gla_b_lr[l], gla_norm[l],
                            mlstm_conv_w[l], mlstm_conv_b[l], mlstm_wq[l], mlstm_wk[l], mlstm_wv[l],
                            mlstm_w_i[l], mlstm_b_i[l], mlstm_w_f[l], mlstm_b_f[l], mlstm_skip[l],
                            mlstm_norm[l], fox_b_f[l], w_branch[l], w_out[l])
        h = rmsnorm(x, norm_ffn[l])
        x = x + conv_ffn(h, ffn_w_up[l], ffn_conv_w[l], ffn_conv_b[l], ffn_w_down[l])
    return rmsnorm(x, norm_final)
```

```python
import functools

import jax
import jax.numpy as jnp
from jax import lax
from jax.experimental import pallas as pl
from jax.experimental.pallas import tpu as pltpu

F32 = jnp.float32
BF16 = jnp.bfloat16
HIGHEST = lax.Precision.HIGHEST
EPS = 1e-6

LANES = 128
F32_SUBLANES = 8
BF16_SUBLANES = 16
VMEM_LIMIT_BYTES = 56 * 1024 * 1024

GLA_HEADS = 4
GLA_RANK = 16
GLA_TAU = 16.0
GLA_CHUNK = 64
MLSTM_HEADS = 4
MLSTM_CONV = 4
MLSTM_BLOCK = 4
FOX_HEADS = 8
N_BRANCH = 3
FFN_CONV = 3
SMALL_W = LANES

INPROJ_TM = 1024
INPROJ_TN = 2048
GLA_ROWS = 256
MLSTM_CHUNK = 64
MLSTM_ROWS = 256
FOXGATE_ROWS = 256
FOX_TQ = 512
MERGE_TM = 512
FFN_TM = 512
FFN_SPLIT = 2
NORM_TM = 1024


def _params(*sem):
    return pltpu.CompilerParams(dimension_semantics=sem, vmem_limit_bytes=VMEM_LIMIT_BYTES)


def _log_sigmoid(z):
    return jnp.minimum(z, 0.0) - jnp.log1p(jnp.exp(-jnp.abs(z)))


def _sigmoid(z):
    return 1.0 / (1.0 + jnp.exp(-z))


def _silu(z):
    return z * _sigmoid(z)


def _tri(n, upper=False):
    r = lax.broadcasted_iota(jnp.int32, (n, n), 0)
    c = lax.broadcasted_iota(jnp.int32, (n, n), 1)
    return (r <= c) if upper else (r >= c)


def _dot(a, b, **kw):
    return jnp.dot(a, b, preferred_element_type=F32, **kw)


def _dot_nt(a, b):
    return lax.dot_general(a, b, (((1,), (1,)), ((), ())), preferred_element_type=F32)


def _dot_tn(a, b):
    return lax.dot_general(a, b, (((0,), (0,)), ((), ())), preferred_element_type=F32)


def _inproj_body(x_ref, g_ref, w_ref, ws_ref, o_ref, os_ref, h_ref):
    @pl.when(pl.program_id(1) == 0)
    def _():
        x = x_ref[...]
        ms = jnp.mean(x * x, axis=-1, keepdims=True)
        h = (x * lax.rsqrt(ms + EPS) * g_ref[...]).astype(BF16)
        h_ref[...] = h
        os_ref[...] = _dot(h, ws_ref[...])

    o_ref[...] = _dot(h_ref[...], w_ref[...]).astype(BF16)


def _inproj(x2, g, w_big, w_small):
    n, d = x2.shape
    c = w_big.shape[1]
    tm, tn = min(INPROJ_TM, n), min(INPROJ_TN, c)
    return pl.pallas_call(
        _inproj_body,
        grid=(n // tm, c // tn),
        in_specs=[
            pl.BlockSpec((tm, d), lambda i, j: (i, 0)),
            pl.BlockSpec((1, d), lambda i, j: (0, 0)),
            pl.BlockSpec((d, tn), lambda i, j: (0, j)),
            pl.BlockSpec((d, SMALL_W), lambda i, j: (0, 0)),
        ],
        out_specs=[
            pl.BlockSpec((tm, tn), lambda i, j: (i, j)),
            pl.BlockSpec((tm, SMALL_W), lambda i, j: (i, 0)),
        ],
        out_shape=[jax.ShapeDtypeStruct((n, c), BF16), jax.ShapeDtypeStruct((n, SMALL_W), F32)],
        scratch_shapes=[pltpu.VMEM((tm, d), BF16)],
        compiler_params=_params("parallel", "arbitrary"),
        name="inproj",
    )(x2, g, w_big, w_small)


def _gla_body(q_ref, k_ref, v_ref, r_ref, s_ref, wlr_ref, blr_ref, gn_ref, o_ref, st_ref, *, chunk, nchunk, dk, dv):
    @pl.when(pl.program_id(1) == 0)
    def _():
        st_ref[...] = jnp.zeros_like(st_ref)

    z = _dot(s_ref[...], wlr_ref[...], precision=HIGHEST) + blr_ref[...]
    log_a = _log_sigmoid(z) * (1.0 / GLA_TAU)
    tri = _tri(chunk).astype(F32)
    causal = _tri(chunk)
    scale = dk ** -0.5
    gn = gn_ref[...]
    for c in range(nchunk):
        rows = slice(c * chunk, (c + 1) * chunk)
        bc = _dot(tri, log_a[rows], precision=HIGHEST)
        b_last = bc[chunk - 1:chunk, :]
        q = q_ref[rows, :].astype(F32) * scale
        k = k_ref[rows, :].astype(F32)
        q_in = (q * jnp.exp(bc)).astype(BF16)
        k_in = (k * jnp.exp(-bc)).astype(BF16)
        k_st = (k * jnp.exp(b_last - bc)).astype(BF16)
        decay = jnp.exp(b_last)
        for h in range(GLA_HEADS):
            ks = slice(h * dk, (h + 1) * dk)
            vs = slice(h * dv, (h + 1) * dv)
            qh = q_in[:, ks]
            vh = v_ref[rows, vs]
            att = jnp.where(causal, _dot_nt(qh, k_in[:, ks]), 0.0)
            st = st_ref[h]
            o = _dot(att.astype(BF16), vh) + _dot_nt(qh, st.astype(BF16))
            st_ref[h] = st * decay[:, ks] + _dot_tn(vh, k_st[:, ks])
            on = o * lax.rsqrt(jnp.mean(o * o, axis=-1, keepdims=True) + EPS) * gn[:, vs]
            o_ref[rows, vs] = (on * _silu(r_ref[rows, vs].astype(F32))).astype(BF16)


def _gla(proj, small, w_lr_pad, b_lr, g_norm, bsz, seq):
    n = proj.shape[0]
    hdk = w_lr_pad.shape[1]
    dk = hdk // GLA_HEADS
    hdv = g_norm.shape[1]
    dv = hdv // GLA_HEADS
    rows = min(GLA_ROWS, seq)
    nt = seq // rows
    assert hdv == 2 * hdk
    row = lambda b, t: b * nt + t
    return pl.pallas_call(
        functools.partial(_gla_body, chunk=GLA_CHUNK, nchunk=rows // GLA_CHUNK, dk=dk, dv=dv),
        grid=(bsz, nt),
        in_specs=[
            pl.BlockSpec((rows, hdk), lambda b, t: (row(b, t), 0)),
            pl.BlockSpec((rows, hdk), lambda b, t: (row(b, t), 1)),
            pl.BlockSpec((rows, hdv), lambda b, t: (row(b, t), 1)),
            pl.BlockSpec((rows, hdv), lambda b, t: (row(b, t), 2)),
            pl.BlockSpec((rows, SMALL_W), lambda b, t: (row(b, t), 0)),
            pl.BlockSpec((SMALL_W, hdk), lambda b, t: (0, 0)),
            pl.BlockSpec((1, hdk), lambda b, t: (0, 0)),
            pl.BlockSpec((1, hdv), lambda b, t: (0, 0)),
        ],
        out_specs=pl.BlockSpec((rows, hdv), lambda b, t: (row(b, t), 0)),
        out_shape=jax.ShapeDtypeStruct((n, hdv), BF16),
        scratch_shapes=[pltpu.VMEM((GLA_HEADS, dv, dk), F32)],
        compiler_params=_params("parallel", "arbitrary"),
        name="gla",
    )(proj, proj, proj, proj, small, w_lr_pad, b_lr, g_norm)


def _mlstm_body(xm_ref, z_ref, cw_ref, cb_ref, wq_ref, wk_ref, wv_ref, wif_ref, wift_ref, bif_ref, bift_ref,
                skip_ref, gn_ref, o_ref, xf_ref, q_sc, k_sc, v_sc, h_sc, c_sc, m_sc, *, chunk, nchunk, dh):
    rows_blk = chunk * nchunk
    halo = F32_SUBLANES

    @pl.when(pl.program_id(1) == 0)
    def _():
        xf_ref[0:halo, :] = jnp.zeros((halo, xf_ref.shape[1]), F32)
        c_sc[...] = jnp.zeros_like(c_sc)
        m_sc[...] = jnp.zeros_like(m_sc)

    xmb = xm_ref[...]
    xf_ref[halo:halo + rows_blk, :] = xmb.astype(F32)
    xf = xf_ref[...]
    cw = cw_ref[...]
    conv = cb_ref[...]
    for j in range(MLSTM_CONV - 1):
        back = MLSTM_CONV - 1 - j
        conv = conv + cw[j:j + 1, :] * pltpu.roll(xf, back, 0)[halo:, :]
    conv = conv + cw[MLSTM_CONV - 1:MLSTM_CONV, :] * xf[halo:, :]
    xf_ref[0:halo, :] = xf[rows_blk:rows_blk + halo, :]
    xc = _silu(conv)
    xcb = xc.astype(BF16)

    for h in range(MLSTM_HEADS):
        hs = slice(h * dh, (h + 1) * dh)
        q_sc[:, hs] = _dot(xcb[:, hs], wq_ref[h]).astype(BF16)
        k_sc[:, hs] = _dot(xcb[:, hs], wk_ref[h]).astype(BF16)
        v_sc[:, hs] = _dot(xmb[:, hs], wv_ref[h]).astype(BF16)
    qa, ka, va = q_sc[...], k_sc[...], v_sc[...]

    gcol = _dot(qa, wif_ref[0]) + _dot(ka, wif_ref[1]) + _dot(va, wif_ref[2]) + bif_ref[...]
    grow = _dot_nt(wift_ref[0], qa) + _dot_nt(wift_ref[1], ka) + _dot_nt(wift_ref[2], va) + bift_ref[...]
    lane = lax.broadcasted_iota(jnp.int32, gcol.shape, 1)
    gcol = jnp.where(lane < MLSTM_HEADS, gcol, _log_sigmoid(gcol))
    sub = lax.broadcasted_iota(jnp.int32, grow.shape, 0)
    grow = jnp.where(sub < MLSTM_HEADS, grow, _log_sigmoid(grow))

    tri = _tri(chunk).astype(F32)
    tri_t = _tri(chunk, upper=True).astype(F32)
    causal = _tri(chunk)
    qscale = dh ** -0.5
    ones_aug = jnp.ones((chunk, LANES), BF16)
    for c in range(nchunk):
        rows = slice(c * chunk, (c + 1) * chunk)
        gc = gcol[rows]
        gr = grow[:, rows]
        cum_c = _dot(tri, gc, precision=HIGHEST)
        cum_r = _dot(gr, tri_t, precision=HIGHEST)
        for h in range(MLSTM_HEADS):
            hs = slice(h * dh, (h + 1) * dh)
            fh = MLSTM_HEADS + h
            i_col, b_col = gc[:, h:h + 1], cum_c[:, fh:fh + 1]
            i_row, b_row = gr[h:h + 1, :], cum_r[fh:fh + 1, :]
            m_st = m_sc[h][:, 0:1]
            d_log = jnp.where(causal, b_col - b_row + i_row, -jnp.inf)
            m_inter = b_col + m_st
            m_t = jnp.maximum(m_inter, jnp.max(d_log, axis=-1, keepdims=True))
            w_intra = jnp.exp(d_log - m_t)
            w_inter = jnp.exp(m_inter - m_t)
            qc = (q_sc[rows, hs].astype(F32) * qscale).astype(BF16)
            kc = k_sc[rows, hs]
            vc = v_sc[rows, hs]
            s = _dot_nt(qc, kc) * w_intra
            c_aug = c_sc[h]
            q_state = _dot(qc, c_aug.astype(BF16))
            num = _dot(s.astype(BF16), vc) + w_inter * q_state[:, :dh]
            qn = jnp.sum(s, axis=-1, keepdims=True) + w_inter * q_state[:, dh:dh + 1]
            h_sc[rows, hs] = num / jnp.maximum(jnp.abs(qn), jnp.exp(-m_t))
            g = b_row[:, chunk - 1:chunk]
            m_new = jnp.maximum(g + m_st, jnp.max(g - b_row + i_row, axis=-1, keepdims=True))
            wa = jnp.exp(g - b_col + i_col - m_new)
            dec = jnp.exp(g + m_st - m_new)
            kw = (kc.astype(F32) * wa).astype(BF16)
            v_aug = jnp.concatenate([vc, ones_aug], axis=1)
            c_sc[h] = dec * c_aug + _dot_tn(kw, v_aug)
            m_sc[h] = jnp.broadcast_to(m_new, (1, LANES))

    gn = gn_ref[...]
    skip = skip_ref[...]
    zg = _silu(z_ref[...].astype(F32))
    for h in range(MLSTM_HEADS):
        hs = slice(h * dh, (h + 1) * dh)
        hh = h_sc[:, hs]
        hn = hh * lax.rsqrt(jnp.mean(hh * hh, axis=-1, keepdims=True) + EPS) * gn[:, hs]
        o_ref[:, hs] = ((hn + skip[:, hs] * xc[:, hs]) * zg[:, hs]).astype(BF16)


def _mlstm(proj, conv_w, conv_b, wq_bd, wk_bd, wv_bd, w_if, w_if_t, b_if, b_if_t, skip, g_norm, bsz, seq, col_blk):
    n = proj.shape[0]
    inner = conv_w.shape[1]
    dh = inner // MLSTM_HEADS
    rows = min(MLSTM_ROWS, seq)
    chunk = min(MLSTM_CHUNK, rows)
    nt = seq // rows
    row = lambda b, t: b * nt + t
    full = lambda shape: pl.BlockSpec(shape, lambda b, t: (0,) * len(shape))
    return pl.pallas_call(
        functools.partial(_mlstm_body, chunk=chunk, nchunk=rows // chunk, dh=dh),
        grid=(bsz, nt),
        in_specs=[
            pl.BlockSpec((rows, inner), lambda b, t: (row(b, t), col_blk)),
            pl.BlockSpec((rows, inner), lambda b, t: (row(b, t), col_blk + 1)),
            full((MLSTM_CONV, inner)),
            full((1, inner)),
            full((MLSTM_HEADS, dh, dh)),
            full((MLSTM_HEADS, dh, dh)),
            full((MLSTM_HEADS, dh, dh)),
            full((3, inner, LANES)),
            full((3, F32_SUBLANES, inner)),
            full((1, LANES)),
            full((F32_SUBLANES, 1)),
            full((1, inner)),
            full((1, inner)),
        ],
        out_specs=pl.BlockSpec((rows, inner), lambda b, t: (row(b, t), 0)),
        out_shape=jax.ShapeDtypeStruct((n, inner), BF16),
        scratch_shapes=[
            pltpu.VMEM((rows + F32_SUBLANES, inner), F32),
            pltpu.VMEM((rows, inner), BF16),
            pltpu.VMEM((rows, inner), BF16),
            pltpu.VMEM((rows, inner), BF16),
            pltpu.VMEM((rows, inner), F32),
            pltpu.VMEM((MLSTM_HEADS, dh, dh + LANES), F32),
            pltpu.VMEM((MLSTM_HEADS, 1, LANES), F32),
        ],
        compiler_params=_params("parallel", "arbitrary"),
        name="mlstm",
    )(proj, proj, conv_w, conv_b, wq_bd, wk_bd, wv_bd, w_if, w_if_t, b_if, b_if_t, skip, g_norm)


def _foxgate_body(s_ref, b_ref, o_ref, carry_ref, *, rows):
    @pl.when(pl.program_id(1) == 0)
    def _():
        carry_ref[...] = jnp.zeros_like(carry_ref)

    lf = _log_sigmoid(s_ref[...] + b_ref[...])
    cum = _dot(_tri(rows).astype(F32), lf, precision=HIGHEST) + carry_ref[...]
    o_ref[...] = cum
    carry_ref[...] = cum[rows - 1:rows, :]


def _foxgate(small, b_pad, bsz, seq):
    n = small.shape[0]
    rows = min(FOXGATE_ROWS, seq)
    nt = seq // rows
    return pl.pallas_call(
        functools.partial(_foxgate_body, rows=rows),
        grid=(bsz, nt),
        in_specs=[
            pl.BlockSpec((rows, SMALL_W), lambda b, t: (b * nt + t, 0)),
            pl.BlockSpec((1, SMALL_W), lambda b, t: (0, 0)),
        ],
        out_specs=pl.BlockSpec((rows, SMALL_W), lambda b, t: (b * nt + t, 0)),
        out_shape=jax.ShapeDtypeStruct((n, SMALL_W), F32),
        scratch_shapes=[pltpu.VMEM((1, SMALL_W), F32)],
        compiler_params=_params("parallel", "arbitrary"),
        name="foxgate",
    )(small, b_pad)


def _fox_body(q_ref, k_ref, v_ref, og_ref, fc_ref, fr_ref, o_ref, *, tq, dh):
    qi = pl.program_id(2)
    q = (q_ref[...].astype(F32) * dh ** -0.5).astype(BF16)
    f_q = fc_ref[0, 0]

    def step(j, carry, masked):
        m, l, acc = carry
        start = pl.multiple_of(j * tq, tq)
        k = k_ref[pl.ds(start, tq), :]
        v = v_ref[pl.ds(start, tq), :]
        s = _dot_nt(q, k) + (f_q - fr_ref[0, 0, j])
        if masked:
            s = jnp.where(_tri(tq), s, -jnp.inf)
        m_new = jnp.maximum(m, jnp.max(s, axis=-1, keepdims=True))
        alpha = jnp.exp(m - m_new)
        p = jnp.exp(s - m_new)
        l = alpha * l + jnp.sum(p, axis=-1, keepdims=True)
        acc = alpha * acc + _dot(p.astype(BF16), v)
        return m_new, l, acc

    init = (jnp.full((tq, 1), -jnp.inf, F32), jnp.zeros((tq, 1), F32), jnp.zeros((tq, dh), F32))
    carry = lax.fori_loop(0, qi, lambda j, c: step(j, c, False), init)
    _, l, acc = step(qi, carry, True)
    o_ref[...] = (acc / l * _sigmoid(og_ref[...].astype(F32))).astype(BF16)


def _fox(proj, f_col, f_row, bsz, seq, dh, q_blk, k_blk, v_blk, og_blk):
    n = proj.shape[0]
    tq = min(FOX_TQ, seq)
    nq = seq // tq
    return pl.pallas_call(
        functools.partial(_fox_body, tq=tq, dh=dh),
        grid=(bsz, FOX_HEADS, nq),
        in_specs=[
            pl.BlockSpec((tq, dh), lambda b, h, i: (b * nq + i, q_blk + h)),
            pl.BlockSpec((seq, dh), lambda b, h, i: (b, k_blk + h)),
            pl.BlockSpec((seq, dh), lambda b, h, i: (b, v_blk + h)),
            pl.BlockSpec((tq, dh), lambda b, h, i: (b * nq + i, og_blk + h)),
            pl.BlockSpec((1, 1, tq, 1), lambda b, h, i: (b, h, i, 0)),
            pl.BlockSpec((1, 1, nq, 1, tq), lambda b, h, i: (b, h, 0, 0, 0)),
        ],
        out_specs=pl.BlockSpec((tq, dh), lambda b, h, i: (b * nq + i, h)),
        out_shape=jax.ShapeDtypeStruct((n, FOX_HEADS * dh), BF16),
        compiler_params=_params("parallel", "parallel", "arbitrary"),
        name="fox",
    )(proj, proj, proj, proj, f_col, f_row)


def _merge_body(x_ref, y0_ref, y1_ref, y2_ref, gt_ref, bg_ref, wb_ref, wo_ref, o_ref, *, d):
    merged = None
    for n, y_ref in enumerate((y0_ref, y1_ref, y2_ref)):
        cs = slice(n * d, (n + 1) * d)
        gate = _sigmoid(gt_ref[:, cs].astype(F32) + bg_ref[:, cs])
        term = _dot(y_ref[...], wb_ref[n]) * gate
        merged = term if merged is None else merged + term
    o_ref[...] = x_ref[...] + _dot(merged.astype(BF16), wo_ref[...])


def _merge(x2, y_gla, y_mlstm, y_fox, proj, b_gate, w_branch, w_out, gate_blk):
    n, d = x2.shape
    tm = min(MERGE_TM, n)
    rowblk = lambda shape: pl.BlockSpec(shape, lambda i: (i, 0))
    return pl.pallas_call(
        functools.partial(_merge_body, d=d),
        grid=(n // tm,),
        in_specs=[
            rowblk((tm, d)), rowblk((tm, d)), rowblk((tm, d)), rowblk((tm, d)),
            pl.BlockSpec((tm, N_BRANCH * d), lambda i: (i, gate_blk)),
            pl.BlockSpec((1, N_BRANCH * d), lambda i: (0, 0)),
            pl.BlockSpec((N_BRANCH, d, d), lambda i: (0, 0, 0)),
            pl.BlockSpec((d, d), lambda i: (0, 0)),
        ],
        out_specs=rowblk((tm, d)),
        out_shape=jax.ShapeDtypeStruct((n, d), F32),
        compiler_params=_params("parallel"),
        name="merge",
    )(x2, y_gla, y_mlstm, y_fox, proj, b_gate, w_branch, w_out)


def _ffn_body(x_ref, xp_ref, g_ref, wa_ref, wg_ref, cwa_ref, cwg_ref, cba_ref, cbg_ref, wd_ref, o_ref, h_ref,
              *, tm, halo, blocks_per_seq):
    i = pl.program_id(0)
    f = pl.program_id(1)

    def norm(x):
        return (x * lax.rsqrt(jnp.mean(x * x, axis=-1, keepdims=True) + EPS) * g_ref[...]).astype(BF16)

    @pl.when(f == 0)
    def _():
        x = x_ref[...]
        hp = norm(xp_ref[...])
        h_ref[0:halo, :] = jnp.where(i % blocks_per_seq != 0, hp, jnp.zeros_like(hp))
        h_ref[halo:halo + tm, :] = norm(x)
        o_ref[...] = x

    h = h_ref[...]

    def conv(w_ref, cw_ref, cb_ref):
        u = _dot(h, w_ref[...])
        cw = cw_ref[...]
        y = cb_ref[...]
        for j in range(FFN_CONV - 1):
            y = y + cw[j:j + 1, :] * pltpu.roll(u, FFN_CONV - 1 - j, 0)[halo:, :]
        return y + cw[FFN_CONV - 1:FFN_CONV, :] * u[halo:, :]

    a = conv(wa_ref, cwa_ref, cba_ref)
    g = conv(wg_ref, cwg_ref, cbg_ref)
    o_ref[...] += _dot((_silu(g) * a).astype(BF16), wd_ref[...])


def _ffn(x2, g, w_a, w_g, cw_a, cw_g, cb_a, cb_g, w_down, seq):
    n, d = x2.shape
    dff = w_a.shape[1]
    tm = min(FFN_TM, seq)
    tf = dff // FFN_SPLIT
    halo = BF16_SUBLANES
    hb = tm // halo
    return pl.pallas_call(
        functools.partial(_ffn_body, tm=tm, halo=halo, blocks_per_seq=seq // tm),
        grid=(n // tm, FFN_SPLIT),
        in_specs=[
            pl.BlockSpec((tm, d), lambda i, f: (i, 0)),
            pl.BlockSpec((halo, d), lambda i, f: (jnp.maximum(i * hb - 1, 0), 0)),
            pl.BlockSpec((1, d), lambda i, f: (0, 0)),
            pl.BlockSpec((d, tf), lambda i, f: (0, f)),
            pl.BlockSpec((d, tf), lambda i, f: (0, f)),
            pl.BlockSpec((FFN_CONV, tf), lambda i, f: (0, f)),
            pl.BlockSpec((FFN_CONV, tf), lambda i, f: (0, f)),
            pl.BlockSpec((1, tf), lambda i, f: (0, f)),
            pl.BlockSpec((1, tf), lambda i, f: (0, f)),
            pl.BlockSpec((tf, d), lambda i, f: (f, 0)),
        ],
        out_specs=pl.BlockSpec((tm, d), lambda i, f: (i, 0)),
        out_shape=jax.ShapeDtypeStruct((n, d), F32),
        scratch_shapes=[pltpu.VMEM((halo + tm, d), BF16)],
        compiler_params=_params("parallel", "arbitrary"),
        name="ffn",
    )(x2, x2, g, w_a, w_g, cw_a, cw_g, cb_a, cb_g, w_down)


def _norm_body(x_ref, g_ref, o_ref):
    x = x_ref[...]
    o_ref[...] = x * lax.rsqrt(jnp.mean(x * x, axis=-1, keepdims=True) + EPS) * g_ref[...]


def _final_norm(x2, g):
    n, d = x2.shape
    tm = min(NORM_TM, n)
    return pl.pallas_call(
        _norm_body,
        grid=(n // tm,),
        in_specs=[pl.BlockSpec((tm, d), lambda i: (i, 0)), pl.BlockSpec((1, d), lambda i: (0, 0))],
        out_specs=pl.BlockSpec((tm, d), lambda i: (i, 0)),
        out_shape=jax.ShapeDtypeStruct((n, d), F32),
        compiler_params=_params("parallel"),
        name="final_norm",
    )(x2, g)


def _block_diag_heads(w, n_heads):
    nblk, bc, bd = w.shape
    per = nblk // n_heads
    w4 = w.reshape(n_heads, per, bc, bd)
    eye = jnp.eye(per, dtype=w.dtype)
    return jnp.einsum("hncd,nm->hncmd", w4, eye).reshape(n_heads, per * bc, per * bd)


def _pad_lanes(a, width):
    return jnp.pad(a, ((0, 0),) * (a.ndim - 1) + ((0, width - a.shape[-1]),))


def kernel(x, norm_mix, w_in, b_gate, gla_w_lr, gla_b_lr, gla_norm, mlstm_conv_w, mlstm_conv_b, mlstm_wq, mlstm_wk,
           mlstm_wv, mlstm_w_i, mlstm_b_i, mlstm_w_f, mlstm_b_f, mlstm_skip, mlstm_norm, fox_b_f, w_branch, w_out,
           norm_ffn, ffn_w_up, ffn_conv_w, ffn_conv_b, ffn_w_down, norm_final):
    bsz, seq, d = x.shape
    depth = w_in.shape[0]
    hdk = gla_w_lr.shape[2]
    hdv = gla_norm.shape[1]
    inner = mlstm_conv_w.shape[2]
    fox_w = w_branch.shape[2]
    fox_dh = fox_w // FOX_HEADS
    dff = ffn_w_down.shape[1]
    assert hdv == d and inner == d and fox_w == d and 2 * hdk == d

    o_glr = 2 * hdk + hdv
    o_gr = o_glr + GLA_RANK
    o_ff = o_gr + hdv + 2 * inner + 3 * fox_w
    o_fog = o_ff + FOX_HEADS
    blk = {"gq": 0, "gk": 1, "gv": 1, "gr": 2, "mx": 3, "fq": 5 * d // fox_dh, "fk": 6 * d // fox_dh,
           "fv": 7 * d // fox_dh, "fog": 8 * d // fox_dh, "gates": 3}

    x2 = x.reshape(bsz * seq, d)
    tq = min(FOX_TQ, seq)
    for l in range(depth):
        wl = w_in[l]
        w_big = jnp.concatenate([wl[:, :o_glr], wl[:, o_gr:o_ff], wl[:, o_fog:]], axis=1).astype(BF16)
        w_small = _pad_lanes(jnp.concatenate([wl[:, o_glr:o_gr], wl[:, o_ff:o_fog]], axis=1), SMALL_W).astype(BF16)
        proj, small = _inproj(x2, norm_mix[l][None, :], w_big, w_small)

        w_lr_pad = jnp.pad(gla_w_lr[l], ((0, SMALL_W - GLA_RANK), (0, 0)))
        y_gla = _gla(proj, small, w_lr_pad, gla_b_lr[l][None, :], gla_norm[l][None, :], bsz, seq)

        w_if = _pad_lanes(jnp.concatenate([mlstm_w_i[l], mlstm_w_f[l]], axis=1), LANES)
        b_if = _pad_lanes(jnp.concatenate([mlstm_b_i[l], mlstm_b_f[l]])[None, :], LANES)
        w_if3 = w_if.reshape(3, inner, LANES).astype(BF16)
        w_if3_t = jnp.transpose(w_if3[:, :, :F32_SUBLANES], (0, 2, 1))
        b_if_t = b_if[0, :F32_SUBLANES][:, None]
        y_mlstm = _mlstm(
            proj, mlstm_conv_w[l], mlstm_conv_b[l][None, :],
            _block_diag_heads(mlstm_wq[l], MLSTM_HEADS).astype(BF16),
            _block_diag_heads(mlstm_wk[l], MLSTM_HEADS).astype(BF16),
            _block_diag_heads(mlstm_wv[l], MLSTM_HEADS).astype(BF16),
            w_if3, w_if3_t, b_if, b_if_t, mlstm_skip[l][None, :], mlstm_norm[l][None, :], bsz, seq, blk["mx"])

        b_fox = jnp.pad(fox_b_f[l], (GLA_RANK, SMALL_W - GLA_RANK - FOX_HEADS))[None, :]
        cum_f = _foxgate(small, b_fox, bsz, seq)[:, GLA_RANK:GLA_RANK + FOX_HEADS]
        cum_f = jnp.transpose(cum_f.reshape(bsz, seq, FOX_HEADS), (0, 2, 1))
        f_col = cum_f[..., None]
        f_row = cum_f.reshape(bsz, FOX_HEADS, seq // tq, 1, tq)
        y_fox = _fox(proj, f_col, f_row, bsz, seq, fox_dh, blk["fq"], blk["fk"], blk["fv"], blk["fog"])

        x2 = _merge(x2, y_gla, y_mlstm, y_fox, proj, b_gate[l].reshape(1, N_BRANCH * d),
                    w_branch[l].astype(BF16), w_out[l].astype(BF16), blk["gates"])

        w_up = ffn_w_up[l].astype(BF16)
        cw, cb = ffn_conv_w[l], ffn_conv_b[l][None, :]
        x2 = _ffn(x2, norm_ffn[l][None, :], w_up[:, :dff], w_up[:, dff:], cw[:, :dff], cw[:, dff:],
                  cb[:, :dff], cb[:, dff:], ffn_w_down[l].astype(BF16), seq)
    return _final_norm(x2, norm_final[None, :]).reshape(bsz, seq, d)
```

```python
import functools

import jax
import jax.numpy as jnp
from jax import lax
from jax.experimental import pallas as pl
from jax.experimental.pallas import tpu as pltpu

F32 = jnp.float32
BF16 = jnp.bfloat16
HIGHEST = lax.Precision.HIGHEST
EPS = 1e-6
LOG2E = 1.4426950408889634

LANES = 128
F32_SUBLANES = 8
BF16_SUBLANES = 16
VMEM_LIMIT_BYTES = 56 * 1024 * 1024

GLA_HEADS = 4
GLA_RANK = 16
GLA_TAU = 16.0
GLA_CHUNK = 64
MLSTM_HEADS = 4
MLSTM_CONV = 4
MLSTM_BLOCK = 4
FOX_HEADS = 8
N_BRANCH = 3
FFN_CONV = 3
SMALL_W = LANES

INPROJ_TM = 1024
INPROJ_TN = 2048
GLA_ROWS = 256
MLSTM_CHUNK = 64
MLSTM_ROWS = 256
FOXGATE_ROWS = 256
FOX_TQ = 512
FOX_HEADS_PER_STEP = 2
MERGE_TM = 512
FFN_TM = 512
FFN_SPLIT = 2
NORM_TM = 1024


def _params(*sem):
    return pltpu.CompilerParams(dimension_semantics=sem, vmem_limit_bytes=VMEM_LIMIT_BYTES)


def _log_sigmoid(z):
    return jnp.minimum(z, 0.0) - jnp.log1p(jnp.exp(-jnp.abs(z)))


def _sigmoid(z):
    return 1.0 / (1.0 + jnp.exp(-z))


def _silu(z):
    return z * _sigmoid(z)


def _tri(n, upper=False):
    r = lax.broadcasted_iota(jnp.int32, (n, n), 0)
    c = lax.broadcasted_iota(jnp.int32, (n, n), 1)
    return (r <= c) if upper else (r >= c)


def _dot(a, b, **kw):
    return jnp.dot(a, b, preferred_element_type=F32, **kw)


def _dot_nt(a, b):
    return lax.dot_general(a, b, (((1,), (1,)), ((), ())), preferred_element_type=F32)


def _dot_tn(a, b):
    return lax.dot_general(a, b, (((0,), (0,)), ((), ())), preferred_element_type=F32)


def _inproj_body(x_ref, g_ref, w_ref, ws_ref, o_ref, os_ref, h_ref):
    @pl.when(pl.program_id(1) == 0)
    def _():
        x = x_ref[...]
        ms = jnp.mean(x * x, axis=-1, keepdims=True)
        h = (x * lax.rsqrt(ms + EPS) * g_ref[...]).astype(BF16)
        h_ref[...] = h
        os_ref[...] = _dot(h, ws_ref[...])

    o_ref[...] = _dot(h_ref[...], w_ref[...]).astype(BF16)


def _inproj(x2, g, w_big, w_small):
    n, d = x2.shape
    c = w_big.shape[1]
    tm, tn = min(INPROJ_TM, n), min(INPROJ_TN, c)
    return pl.pallas_call(
        _inproj_body,
        grid=(n // tm, c // tn),
        in_specs=[
            pl.BlockSpec((tm, d), lambda i, j: (i, 0)),
            pl.BlockSpec((1, d), lambda i, j: (0, 0)),
            pl.BlockSpec((d, tn), lambda i, j: (0, j)),
            pl.BlockSpec((d, SMALL_W), lambda i, j: (0, 0)),
        ],
        out_specs=[
            pl.BlockSpec((tm, tn), lambda i, j: (i, j)),
            pl.BlockSpec((tm, SMALL_W), lambda i, j: (i, 0)),
        ],
        out_shape=[jax.ShapeDtypeStruct((n, c), BF16), jax.ShapeDtypeStruct((n, SMALL_W), F32)],
        scratch_shapes=[pltpu.VMEM((tm, d), BF16)],
        compiler_params=_params("parallel", "arbitrary"),
        name="inproj",
    )(x2, g, w_big, w_small)


def _gla_body(q_ref, k_ref, v_ref, r_ref, s_ref, wlr_ref, blr_ref, gn_ref, o_ref, st_ref, *, chunk, nchunk, dk, dv):
    @pl.when(pl.program_id(1) == 0)
    def _():
        st_ref[...] = jnp.zeros_like(st_ref)

    z = _dot(s_ref[...], wlr_ref[...], precision=HIGHEST) + blr_ref[...]
    log_a = _log_sigmoid(z) * (1.0 / GLA_TAU)
    tri = _tri(chunk).astype(F32)
    causal = _tri(chunk)
    scale = dk ** -0.5
    gn = gn_ref[...]
    for c in range(nchunk):
        rows = slice(c * chunk, (c + 1) * chunk)
        bc = _dot(tri, log_a[rows], precision=HIGHEST)
        b_last = bc[chunk - 1:chunk, :]
        q = q_ref[rows, :].astype(F32) * scale
        k = k_ref[rows, :].astype(F32)
        q_in = (q * jnp.exp(bc)).astype(BF16)
        k_in = (k * jnp.exp(-bc)).astype(BF16)
        k_st = (k * jnp.exp(b_last - bc)).astype(BF16)
        decay = jnp.exp(b_last)
        for h in range(GLA_HEADS):
            ks = slice(h * dk, (h + 1) * dk)
            vs = slice(h * dv, (h + 1) * dv)
            qh = q_in[:, ks]
            vh = v_ref[rows, vs]
            att = jnp.where(causal, _dot_nt(qh, k_in[:, ks]), 0.0)
            st = st_ref[h]
            o = _dot(att.astype(BF16), vh) + _dot_nt(qh, st.astype(BF16))
            st_ref[h] = st * decay[:, ks] + _dot_tn(vh, k_st[:, ks])
            on = o * lax.rsqrt(jnp.mean(o * o, axis=-1, keepdims=True) + EPS) * gn[:, vs]
            o_ref[rows, vs] = (on * _silu(r_ref[rows, vs].astype(F32))).astype(BF16)


def _gla(proj, small, w_lr_pad, b_lr, g_norm, bsz, seq):
    n = proj.shape[0]
    hdk = w_lr_pad.shape[1]
    dk = hdk // GLA_HEADS
    hdv = g_norm.shape[1]
    dv = hdv // GLA_HEADS
    rows = min(GLA_ROWS, seq)
    nt = seq // rows
    assert hdv == 2 * hdk
    row = lambda b, t: b * nt + t
    return pl.pallas_call(
        functools.partial(_gla_body, chunk=GLA_CHUNK, nchunk=rows // GLA_CHUNK, dk=dk, dv=dv),
        grid=(bsz, nt),
        in_specs=[
            pl.BlockSpec((rows, hdk), lambda b, t: (row(b, t), 0)),
            pl.BlockSpec((rows, hdk), lambda b, t: (row(b, t), 1)),
            pl.BlockSpec((rows, hdv), lambda b, t: (row(b, t), 1)),
            pl.BlockSpec((rows, hdv), lambda b, t: (row(b, t), 2)),
            pl.BlockSpec((rows, SMALL_W), lambda b, t: (row(b, t), 0)),
            pl.BlockSpec((SMALL_W, hdk), lambda b, t: (0, 0)),
            pl.BlockSpec((1, hdk), lambda b, t: (0, 0)),
            pl.BlockSpec((1, hdv), lambda b, t: (0, 0)),
        ],
        out_specs=pl.BlockSpec((rows, hdv), lambda b, t: (row(b, t), 0)),
        out_shape=jax.ShapeDtypeStruct((n, hdv), BF16),
        scratch_shapes=[pltpu.VMEM((GLA_HEADS, dv, dk), F32)],
        compiler_params=_params("parallel", "arbitrary"),
        name="gla",
    )(proj, proj, proj, proj, small, w_lr_pad, b_lr, g_norm)


def _mlstm_body(xm_ref, z_ref, cw_ref, cb_ref, wq_ref, wk_ref, wv_ref, wif_ref, wift_ref, bif_ref, bift_ref,
                skip_ref, gn_ref, o_ref, xf_ref, q_sc, k_sc, v_sc, h_sc, c_sc, m_sc, *, chunk, nchunk, dh):
    rows_blk = chunk * nchunk
    halo = F32_SUBLANES

    @pl.when(pl.program_id(1) == 0)
    def _():
        xf_ref[0:halo, :] = jnp.zeros((halo, xf_ref.shape[1]), F32)
        c_sc[...] = jnp.zeros_like(c_sc)
        m_sc[...] = jnp.zeros_like(m_sc)

    xmb = xm_ref[...]
    xf_ref[halo:halo + rows_blk, :] = xmb.astype(F32)
    xf = xf_ref[...]
    cw = cw_ref[...]
    conv = cb_ref[...]
    for j in range(MLSTM_CONV - 1):
        back = MLSTM_CONV - 1 - j
        conv = conv + cw[j:j + 1, :] * pltpu.roll(xf, back, 0)[halo:, :]
    conv = conv + cw[MLSTM_CONV - 1:MLSTM_CONV, :] * xf[halo:, :]
    xf_ref[0:halo, :] = xf[rows_blk:rows_blk + halo, :]
    xc = _silu(conv)
    xcb = xc.astype(BF16)

    for h in range(MLSTM_HEADS):
        hs = slice(h * dh, (h + 1) * dh)
        q_sc[:, hs] = _dot(xcb[:, hs], wq_ref[h]).astype(BF16)
        k_sc[:, hs] = _dot(xcb[:, hs], wk_ref[h]).astype(BF16)
        v_sc[:, hs] = _dot(xmb[:, hs], wv_ref[h]).astype(BF16)
    qa, ka, va = q_sc[...], k_sc[...], v_sc[...]

    gcol = _dot(qa, wif_ref[0]) + _dot(ka, wif_ref[1]) + _dot(va, wif_ref[2]) + bif_ref[...]
    grow = _dot_nt(wift_ref[0], qa) + _dot_nt(wift_ref[1], ka) + _dot_nt(wift_ref[2], va) + bift_ref[...]
    lane = lax.broadcasted_iota(jnp.int32, gcol.shape, 1)
    gcol = jnp.where(lane < MLSTM_HEADS, gcol, _log_sigmoid(gcol))
    sub = lax.broadcasted_iota(jnp.int32, grow.shape, 0)
    grow = jnp.where(sub < MLSTM_HEADS, grow, _log_sigmoid(grow))

    tri = _tri(chunk).astype(F32)
    tri_t = _tri(chunk, upper=True).astype(F32)
    causal = _tri(chunk)
    qscale = dh ** -0.5
    ones_aug = jnp.ones((chunk, LANES), BF16)
    for c in range(nchunk):
        rows = slice(c * chunk, (c + 1) * chunk)
        gc = gcol[rows]
        gr = grow[:, rows]
        cum_c = _dot(tri, gc, precision=HIGHEST)
        cum_r = _dot(gr, tri_t, precision=HIGHEST)
        for h in range(MLSTM_HEADS):
            hs = slice(h * dh, (h + 1) * dh)
            fh = MLSTM_HEADS + h
            i_col, b_col = gc[:, h:h + 1], cum_c[:, fh:fh + 1]
            i_row, b_row = gr[h:h + 1, :], cum_r[fh:fh + 1, :]
            m_st = m_sc[h][:, 0:1]
            d_log = jnp.where(causal, b_col - b_row + i_row, -jnp.inf)
            m_inter = b_col + m_st
            m_t = jnp.maximum(m_inter, jnp.max(d_log, axis=-1, keepdims=True))
            w_intra = jnp.exp(d_log - m_t)
            w_inter = jnp.exp(m_inter - m_t)
            qc = (q_sc[rows, hs].astype(F32) * qscale).astype(BF16)
            kc = k_sc[rows, hs]
            vc = v_sc[rows, hs]
            s = _dot_nt(qc, kc) * w_intra
            c_aug = c_sc[h]
            q_state = _dot(qc, c_aug.astype(BF16))
            num = _dot(s.astype(BF16), vc) + w_inter * q_state[:, :dh]
            qn = jnp.sum(s, axis=-1, keepdims=True) + w_inter * q_state[:, dh:dh + 1]
            h_sc[rows, hs] = num / jnp.maximum(jnp.abs(qn), jnp.exp(-m_t))
            g = b_row[:, chunk - 1:chunk]
            m_new = jnp.maximum(g + m_st, jnp.max(g - b_row + i_row, axis=-1, keepdims=True))
            wa = jnp.exp(g - b_col + i_col - m_new)
            dec = jnp.exp(g + m_st - m_new)
            kw = (kc.astype(F32) * wa).astype(BF16)
            v_aug = jnp.concatenate([vc, ones_aug], axis=1)
            c_sc[h] = dec * c_aug + _dot_tn(kw, v_aug)
            m_sc[h] = jnp.broadcast_to(m_new, (1, LANES))

    gn = gn_ref[...]
    skip = skip_ref[...]
    zg = _silu(z_ref[...].astype(F32))
    for h in range(MLSTM_HEADS):
        hs = slice(h * dh, (h + 1) * dh)
        hh = h_sc[:, hs]
        hn = hh * lax.rsqrt(jnp.mean(hh * hh, axis=-1, keepdims=True) + EPS) * gn[:, hs]
        o_ref[:, hs] = ((hn + skip[:, hs] * xc[:, hs]) * zg[:, hs]).astype(BF16)


def _mlstm(proj, conv_w, conv_b, wq_bd, wk_bd, wv_bd, w_if, w_if_t, b_if, b_if_t, skip, g_norm, bsz, seq, col_blk):
    n = proj.shape[0]
    inner = conv_w.shape[1]
    dh = inner // MLSTM_HEADS
    rows = min(MLSTM_ROWS, seq)
    chunk = min(MLSTM_CHUNK, rows)
    nt = seq // rows
    row = lambda b, t: b * nt + t
    full = lambda shape: pl.BlockSpec(shape, lambda b, t: (0,) * len(shape))
    return pl.pallas_call(
        functools.partial(_mlstm_body, chunk=chunk, nchunk=rows // chunk, dh=dh),
        grid=(bsz, nt),
        in_specs=[
            pl.BlockSpec((rows, inner), lambda b, t: (row(b, t), col_blk)),
            pl.BlockSpec((rows, inner), lambda b, t: (row(b, t), col_blk + 1)),
            full((MLSTM_CONV, inner)),
            full((1, inner)),
            full((MLSTM_HEADS, dh, dh)),
            full((MLSTM_HEADS, dh, dh)),
            full((MLSTM_HEADS, dh, dh)),
            full((3, inner, LANES)),
            full((3, F32_SUBLANES, inner)),
            full((1, LANES)),
            full((F32_SUBLANES, 1)),
            full((1, inner)),
            full((1, inner)),
        ],
        out_specs=pl.BlockSpec((rows, inner), lambda b, t: (row(b, t), 0)),
        out_shape=jax.ShapeDtypeStruct((n, inner), BF16),
        scratch_shapes=[
            pltpu.VMEM((rows + F32_SUBLANES, inner), F32),
            pltpu.VMEM((rows, inner), BF16),
            pltpu.VMEM((rows, inner), BF16),
            pltpu.VMEM((rows, inner), BF16),
            pltpu.VMEM((rows, inner), F32),
            pltpu.VMEM((MLSTM_HEADS, dh, dh + LANES), F32),
            pltpu.VMEM((MLSTM_HEADS, 1, LANES), F32),
        ],
        compiler_params=_params("parallel", "arbitrary"),
        name="mlstm",
    )(proj, proj, conv_w, conv_b, wq_bd, wk_bd, wv_bd, w_if, w_if_t, b_if, b_if_t, skip, g_norm)


def _foxgate_body(s_ref, b_ref, o_ref, carry_ref, *, rows):
    @pl.when(pl.program_id(1) == 0)
    def _():
        carry_ref[...] = jnp.zeros_like(carry_ref)

    lf = _log_sigmoid(s_ref[...] + b_ref[...])
    cum = _dot(_tri(rows).astype(F32), lf, precision=HIGHEST) + carry_ref[...]
    o_ref[...] = cum
    carry_ref[...] = cum[rows - 1:rows, :]


def _foxgate(small, b_pad, bsz, seq):
    n = small.shape[0]
    rows = min(FOXGATE_ROWS, seq)
    nt = seq // rows
    return pl.pallas_call(
        functools.partial(_foxgate_body, rows=rows),
        grid=(bsz, nt),
        in_specs=[
            pl.BlockSpec((rows, SMALL_W), lambda b, t: (b * nt + t, 0)),
            pl.BlockSpec((1, SMALL_W), lambda b, t: (0, 0)),
        ],
        out_specs=pl.BlockSpec((rows, SMALL_W), lambda b, t: (b * nt + t, 0)),
        out_shape=jax.ShapeDtypeStruct((n, SMALL_W), F32),
        scratch_shapes=[pltpu.VMEM((1, SMALL_W), F32)],
        compiler_params=_params("parallel", "arbitrary"),
        name="foxgate",
    )(small, b_pad)


def _fox_body(q_ref, k_ref, v_ref, og_ref, fr_ref, o_ref, va_ref, acc_ref, s_ref, *, tq, dh, hpb):
    qi = pl.program_id(2)
    wa = dh + LANES

    @pl.when(qi == 0)
    def _():
        for h in range(hpb):
            va_ref[h, :, :dh] = v_ref[:, h * dh:(h + 1) * dh]
            va_ref[h, :, dh:] = jnp.ones((va_ref.shape[1], LANES), BF16)

    qs = [(q_ref[:, h * dh:(h + 1) * dh].astype(F32) * (dh ** -0.5 * LOG2E)).astype(BF16) for h in range(hpb)]
    acc_ref[...] = jnp.zeros_like(acc_ref)

    def scores(h, j):
        start = pl.multiple_of(j * tq, tq)
        return _dot_nt(qs[h], k_ref[pl.ds(start, tq), h * dh:(h + 1) * dh]) - fr_ref[0, h, j] * LOG2E

    def step(j, ms, slot, last):
        start = pl.multiple_of(j * tq, tq)
        out = []
        for h in range(hpb):
            s = s_ref[slot, h]
            if last:
                s = jnp.where(_tri(tq), s, -jnp.inf)
            else:
                s_ref[1 - slot, h] = scores(h, j + 1)
            m_new = jnp.maximum(ms[h], jnp.max(s, axis=-1, keepdims=True))
            p = jnp.exp2(s - m_new).astype(BF16)
            acc_ref[h] = jnp.exp2(ms[h] - m_new) * acc_ref[h] + _dot(p, va_ref[h, pl.ds(start, tq), :])
            out.append(m_new)
        return tuple(out)

    for h in range(hpb):
        s_ref[0, h] = scores(h, 0)
    m0 = tuple(jnp.full((tq, 1), -jnp.inf, F32) for _ in range(hpb))
    def pair(t, ms):
        return step(2 * t + 1, step(2 * t, ms, 0, False), 1, False)

    ms = lax.fori_loop(0, qi // 2, pair, m0)

    def finish(ms, slot):
        step(qi, ms, slot, True)
        for h in range(hpb):
            acc = acc_ref[h]
            gate = _sigmoid(og_ref[:, h * dh:(h + 1) * dh].astype(F32))
            o_ref[:, h * dh:(h + 1) * dh] = (acc[:, :dh] / acc[:, dh:] * gate).astype(BF16)

    @pl.when(qi % 2 == 0)
    def _():
        finish(ms, 0)

    @pl.when(qi % 2 == 1)
    def _():
        finish(step(qi - 1, ms, 0, False), 1)


def _fox(proj, f_row, bsz, seq, dh, q_blk, k_blk, v_blk, og_blk):
    n = proj.shape[0]
    tq = min(FOX_TQ, seq)
    nq = seq // tq
    hpb = FOX_HEADS_PER_STEP
    w = hpb * dh
    assert FOX_HEADS % hpb == 0 and q_blk % hpb == 0 and k_blk % hpb == 0 and v_blk % hpb == 0 and og_blk % hpb == 0
    return pl.pallas_call(
        functools.partial(_fox_body, tq=tq, dh=dh, hpb=hpb),
        grid=(bsz, FOX_HEADS // hpb, nq),
        in_specs=[
            pl.BlockSpec((tq, w), lambda b, h, i: (b * nq + i, q_blk // hpb + h)),
            pl.BlockSpec((seq, w), lambda b, h, i: (b, k_blk // hpb + h)),
            pl.BlockSpec((seq, w), lambda b, h, i: (b, v_blk // hpb + h)),
            pl.BlockSpec((tq, w), lambda b, h, i: (b * nq + i, og_blk // hpb + h)),
            pl.BlockSpec((1, hpb, nq, 1, tq), lambda b, h, i: (b, h, 0, 0, 0)),
        ],
        out_specs=pl.BlockSpec((tq, w), lambda b, h, i: (b * nq + i, h)),
        out_shape=jax.ShapeDtypeStruct((n, FOX_HEADS * dh), BF16),
        scratch_shapes=[pltpu.VMEM((hpb, seq, dh + LANES), BF16), pltpu.VMEM((hpb, tq, dh + LANES), F32),
                        pltpu.VMEM((2, hpb, tq, tq), F32)],
        compiler_params=_params("parallel", "parallel", "arbitrary"),
        name="fox",
    )(proj, proj, proj, proj, f_row)


def _merge_body(x_ref, y0_ref, y1_ref, y2_ref, gt_ref, bg_ref, wb_ref, wo_ref, o_ref, *, d):
    merged = None
    for n, y_ref in enumerate((y0_ref, y1_ref, y2_ref)):
        cs = slice(n * d, (n + 1) * d)
        gate = _sigmoid(gt_ref[:, cs].astype(F32) + bg_ref[:, cs])
        term = _dot(y_ref[...], wb_ref[n]) * gate
        merged = term if merged is None else merged + term
    o_ref[...] = x_ref[...] + _dot(merged.astype(BF16), wo_ref[...])


def _merge(x2, y_gla, y_mlstm, y_fox, proj, b_gate, w_branch, w_out, gate_blk):
    n, d = x2.shape
    tm = min(MERGE_TM, n)
    rowblk = lambda shape: pl.BlockSpec(shape, lambda i: (i, 0))
    return pl.pallas_call(
        functools.partial(_merge_body, d=d),
        grid=(n // tm,),
        in_specs=[
            rowblk((tm, d)), rowblk((tm, d)), rowblk((tm, d)), rowblk((tm, d)),
            pl.BlockSpec((tm, N_BRANCH * d), lambda i: (i, gate_blk)),
            pl.BlockSpec((1, N_BRANCH * d), lambda i: (0, 0)),
            pl.BlockSpec((N_BRANCH, d, d), lambda i: (0, 0, 0)),
            pl.BlockSpec((d, d), lambda i: (0, 0)),
        ],
        out_specs=rowblk((tm, d)),
        out_shape=jax.ShapeDtypeStruct((n, d), F32),
        compiler_params=_params("parallel"),
        name="merge",
    )(x2, y_gla, y_mlstm, y_fox, proj, b_gate, w_branch, w_out)


def _ffn_body(x_ref, xp_ref, g_ref, wa_ref, wg_ref, cwa_ref, cwg_ref, cba_ref, cbg_ref, wd_ref, o_ref, h_ref,
              *, tm, halo, blocks_per_seq):
    i = pl.program_id(0)
    f = pl.program_id(1)

    def norm(x):
        return (x * lax.rsqrt(jnp.mean(x * x, axis=-1, keepdims=True) + EPS) * g_ref[...]).astype(BF16)

    @pl.when(f == 0)
    def _():
        x = x_ref[...]
        hp = norm(xp_ref[...])
        h_ref[0:halo, :] = jnp.where(i % blocks_per_seq != 0, hp, jnp.zeros_like(hp))
        h_ref[halo:halo + tm, :] = norm(x)
        o_ref[...] = x

    h = h_ref[...]

    def conv(w_ref, cw_ref, cb_ref):
        u = _dot(h, w_ref[...])
        cw = cw_ref[...]
        y = cb_ref[...]
        for j in range(FFN_CONV - 1):
            y = y + cw[j:j + 1, :] * pltpu.roll(u, FFN_CONV - 1 - j, 0)[halo:, :]
        return y + cw[FFN_CONV - 1:FFN_CONV, :] * u[halo:, :]

    a = conv(wa_ref, cwa_ref, cba_ref)
    g = conv(wg_ref, cwg_ref, cbg_ref)
    o_ref[...] += _dot((_silu(g) * a).astype(BF16), wd_ref[...])


def _ffn(x2, g, w_a, w_g, cw_a, cw_g, cb_a, cb_g, w_down, seq):
    n, d = x2.shape
    dff = w_a.shape[1]
    tm = min(FFN_TM, seq)
    tf = dff // FFN_SPLIT
    halo = BF16_SUBLANES
    hb = tm // halo
    return pl.pallas_call(
        functools.partial(_ffn_body, tm=tm, halo=halo, blocks_per_seq=seq // tm),
        grid=(n // tm, FFN_SPLIT),
        in_specs=[
            pl.BlockSpec((tm, d), lambda i, f: (i, 0)),
            pl.BlockSpec((halo, d), lambda i, f: (jnp.maximum(i * hb - 1, 0), 0)),
            pl.BlockSpec((1, d), lambda i, f: (0, 0)),
            pl.BlockSpec((d, tf), lambda i, f: (0, f)),
            pl.BlockSpec((d, tf), lambda i, f: (0, f)),
            pl.BlockSpec((FFN_CONV, tf), lambda i, f: (0, f)),
            pl.BlockSpec((FFN_CONV, tf), lambda i, f: (0, f)),
            pl.BlockSpec((1, tf), lambda i, f: (0, f)),
            pl.BlockSpec((1, tf), lambda i, f: (0, f)),
            pl.BlockSpec((tf, d), lambda i, f: (f, 0)),
        ],
        out_specs=pl.BlockSpec((tm, d), lambda i, f: (i, 0)),
        out_shape=jax.ShapeDtypeStruct((n, d), F32),
        scratch_shapes=[pltpu.VMEM((halo + tm, d), BF16)],
        compiler_params=_params("parallel", "arbitrary"),
        name="ffn",
    )(x2, x2, g, w_a, w_g, cw_a, cw_g, cb_a, cb_g, w_down)


def _norm_body(x_ref, g_ref, o_ref):
    x = x_ref[...]
    o_ref[...] = x * lax.rsqrt(jnp.mean(x * x, axis=-1, keepdims=True) + EPS) * g_ref[...]


def _final_norm(x2, g):
    n, d = x2.shape
    tm = min(NORM_TM, n)
    return pl.pallas_call(
        _norm_body,
        grid=(n // tm,),
        in_specs=[pl.BlockSpec((tm, d), lambda i: (i, 0)), pl.BlockSpec((1, d), lambda i: (0, 0))],
        out_specs=pl.BlockSpec((tm, d), lambda i: (i, 0)),
        out_shape=jax.ShapeDtypeStruct((n, d), F32),
        compiler_params=_params("parallel"),
        name="final_norm",
    )(x2, g)


def _block_diag_heads(w, n_heads):
    nblk, bc, bd = w.shape
    per = nblk // n_heads
    w4 = w.reshape(n_heads, per, bc, bd)
    eye = jnp.eye(per, dtype=w.dtype)
    return jnp.einsum("hncd,nm->hncmd", w4, eye).reshape(n_heads, per * bc, per * bd)


def _pad_lanes(a, width):
    return jnp.pad(a, ((0, 0),) * (a.ndim - 1) + ((0, width - a.shape[-1]),))


def kernel(x, norm_mix, w_in, b_gate, gla_w_lr, gla_b_lr, gla_norm, mlstm_conv_w, mlstm_conv_b, mlstm_wq, mlstm_wk,
           mlstm_wv, mlstm_w_i, mlstm_b_i, mlstm_w_f, mlstm_b_f, mlstm_skip, mlstm_norm, fox_b_f, w_branch, w_out,
           norm_ffn, ffn_w_up, ffn_conv_w, ffn_conv_b, ffn_w_down, norm_final):
    bsz, seq, d = x.shape
    depth = w_in.shape[0]
    hdk = gla_w_lr.shape[2]
    hdv = gla_norm.shape[1]
    inner = mlstm_conv_w.shape[2]
    fox_w = w_branch.shape[2]
    fox_dh = fox_w // FOX_HEADS
    dff = ffn_w_down.shape[1]
    assert hdv == d and inner == d and fox_w == d and 2 * hdk == d

    o_glr = 2 * hdk + hdv
    o_gr = o_glr + GLA_RANK
    o_ff = o_gr + hdv + 2 * inner + 3 * fox_w
    o_fog = o_ff + FOX_HEADS
    blk = {"gq": 0, "gk": 1, "gv": 1, "gr": 2, "mx": 3, "fq": 5 * d // fox_dh, "fk": 6 * d // fox_dh,
           "fv": 7 * d // fox_dh, "fog": 8 * d // fox_dh, "gates": 3}

    x2 = x.reshape(bsz * seq, d)
    tq = min(FOX_TQ, seq)
    for l in range(depth):
        wl = w_in[l]
        w_big = jnp.concatenate([wl[:, :o_glr], wl[:, o_gr:o_ff], wl[:, o_fog:]], axis=1).astype(BF16)
        w_small = _pad_lanes(jnp.concatenate([wl[:, o_glr:o_gr], wl[:, o_ff:o_fog]], axis=1), SMALL_W).astype(BF16)
        proj, small = _inproj(x2, norm_mix[l][None, :], w_big, w_small)

        w_lr_pad = jnp.pad(gla_w_lr[l], ((0, SMALL_W - GLA_RANK), (0, 0)))
        y_gla = _gla(proj, small, w_lr_pad, gla_b_lr[l][None, :], gla_norm[l][None, :], bsz, seq)

        w_if = _pad_lanes(jnp.concatenate([mlstm_w_i[l], mlstm_w_f[l]], axis=1), LANES)
        b_if = _pad_lanes(jnp.concatenate([mlstm_b_i[l], mlstm_b_f[l]])[None, :], LANES)
        w_if3 = w_if.reshape(3, inner, LANES).astype(BF16)
        w_if3_t = jnp.transpose(w_if3[:, :, :F32_SUBLANES], (0, 2, 1))
        b_if_t = b_if[0, :F32_SUBLANES][:, None]
        y_mlstm = _mlstm(
            proj, mlstm_conv_w[l], mlstm_conv_b[l][None, :],
            _block_diag_heads(mlstm_wq[l], MLSTM_HEADS).astype(BF16),
            _block_diag_heads(mlstm_wk[l], MLSTM_HEADS).astype(BF16),
            _block_diag_heads(mlstm_wv[l], MLSTM_HEADS).astype(BF16),
            w_if3, w_if3_t, b_if, b_if_t, mlstm_skip[l][None, :], mlstm_norm[l][None, :], bsz, seq, blk["mx"])

        b_fox = jnp.pad(fox_b_f[l], (GLA_RANK, SMALL_W - GLA_RANK - FOX_HEADS))[None, :]
        cum_f = _foxgate(small, b_fox, bsz, seq)[:, GLA_RANK:GLA_RANK + FOX_HEADS]
        cum_f = jnp.transpose(cum_f.reshape(bsz, seq, FOX_HEADS), (0, 2, 1))
        f_row = cum_f.reshape(bsz, FOX_HEADS, seq // tq, 1, tq)
        y_fox = _fox(proj, f_row, bsz, seq, fox_dh, blk["fq"], blk["fk"], blk["fv"], blk["fog"])

        x2 = _merge(x2, y_gla, y_mlstm, y_fox, proj, b_gate[l].reshape(1, N_BRANCH * d),
                    w_branch[l].astype(BF16), w_out[l].astype(BF16), blk["gates"])

        w_up = ffn_w_up[l].astype(BF16)
        cw, cb = ffn_conv_w[l], ffn_conv_b[l][None, :]
        x2 = _ffn(x2, norm_ffn[l][None, :], w_up[:, :dff], w_up[:, dff:], cw[:, :dff], cw[:, dff:],
                  cb[:, :dff], cb[:, dff:], ffn_w_down[l].astype(BF16), seq)
    return _final_norm(x2, norm_final[None, :]).reshape(bsz, seq, d)
```

```python
import functools

import jax
import jax.numpy as jnp
from jax import lax
from jax.experimental import pallas as pl
from jax.experimental.pallas import tpu as pltpu

F32 = jnp.float32
BF16 = jnp.bfloat16
HIGHEST = lax.Precision.HIGHEST
EPS = 1e-6
LOG2E = 1.4426950408889634

LANES = 128
F32_SUBLANES = 8
BF16_SUBLANES = 16
VMEM_LIMIT_BYTES = 56 * 1024 * 1024

GLA_HEADS = 4
GLA_RANK = 16
GLA_TAU = 16.0
GLA_CHUNK = 64
MLSTM_HEADS = 4
MLSTM_CONV = 4
MLSTM_BLOCK = 4
FOX_HEADS = 8
N_BRANCH = 3
FFN_CONV = 3
SMALL_W = LANES

INPROJ_TM = 1024
INPROJ_TN = 2048
GLA_ROWS = 256
MLSTM_CHUNK = 256
MLSTM_ROWS = 256
FOXGATE_ROWS = 256
FOX_TQ = 512
FOX_HEADS_PER_STEP = 2
MERGE_TM = 512
FFN_TM = 512
FFN_CHUNK = 1024


def _params(*sem):
    return pltpu.CompilerParams(dimension_semantics=sem, vmem_limit_bytes=VMEM_LIMIT_BYTES)


def _log_sigmoid(z):
    return jnp.minimum(z, 0.0) - jnp.log1p(jnp.exp(-jnp.abs(z)))


def _sigmoid(z):
    return 1.0 / (1.0 + jnp.exp(-z))


def _silu(z):
    return z * _sigmoid(z)


def _tri(n, upper=False):
    r = lax.broadcasted_iota(jnp.int32, (n, n), 0)
    c = lax.broadcasted_iota(jnp.int32, (n, n), 1)
    return (r <= c) if upper else (r >= c)


def _dot(a, b, **kw):
    return jnp.dot(a, b, preferred_element_type=F32, **kw)


def _dot_nt(a, b):
    return lax.dot_general(a, b, (((1,), (1,)), ((), ())), preferred_element_type=F32)


def _dot_tn(a, b):
    return lax.dot_general(a, b, (((0,), (0,)), ((), ())), preferred_element_type=F32)


def _inproj_body(x_ref, g_ref, w_ref, ws_ref, o_ref, os_ref, h_ref):
    @pl.when(pl.program_id(1) == 0)
    def _():
        x = x_ref[...]
        ms = jnp.mean(x * x, axis=-1, keepdims=True)
        h = (x * lax.rsqrt(ms + EPS) * g_ref[...]).astype(BF16)
        h_ref[...] = h
        os_ref[...] = _dot(h, ws_ref[...])

    o_ref[...] = _dot(h_ref[...], w_ref[...]).astype(BF16)


def _inproj(x2, g, w_big, w_small):
    n, d = x2.shape
    c = w_big.shape[1]
    tm, tn = min(INPROJ_TM, n), min(INPROJ_TN, c)
    return pl.pallas_call(
        _inproj_body,
        grid=(n // tm, c // tn),
        in_specs=[
            pl.BlockSpec((tm, d), lambda i, j: (i, 0)),
            pl.BlockSpec((1, d), lambda i, j: (0, 0)),
            pl.BlockSpec((d, tn), lambda i, j: (0, j)),
            pl.BlockSpec((d, SMALL_W), lambda i, j: (0, 0)),
        ],
        out_specs=[
            pl.BlockSpec((tm, tn), lambda i, j: (i, j)),
            pl.BlockSpec((tm, SMALL_W), lambda i, j: (i, 0)),
        ],
        out_shape=[jax.ShapeDtypeStruct((n, c), BF16), jax.ShapeDtypeStruct((n, SMALL_W), F32)],
        scratch_shapes=[pltpu.VMEM((tm, d), BF16)],
        compiler_params=_params("parallel", "arbitrary"),
        name="inproj",
    )(x2, g, w_big, w_small)


def _gla_body(q_ref, k_ref, v_ref, r_ref, s_ref, wlr_ref, blr_ref, gn_ref, o_ref, st_ref, *, chunk, nchunk, dk, dv):
    @pl.when(pl.program_id(1) == 0)
    def _():
        st_ref[...] = jnp.zeros_like(st_ref)

    z = _dot(s_ref[...], wlr_ref[...], precision=HIGHEST) + blr_ref[...]
    log_a = _log_sigmoid(z) * (1.0 / GLA_TAU)
    tri = _tri(chunk).astype(F32)
    causal = _tri(chunk)
    scale = dk ** -0.5
    gn = gn_ref[...]
    for c in range(nchunk):
        rows = slice(c * chunk, (c + 1) * chunk)
        bc = _dot(tri, log_a[rows], precision=HIGHEST)
        b_last = bc[chunk - 1:chunk, :]
        q = q_ref[rows, :].astype(F32) * scale
        k = k_ref[rows, :].astype(F32)
        q_in = (q * jnp.exp(bc)).astype(BF16)
        k_in = (k * jnp.exp(-bc)).astype(BF16)
        k_st = (k * jnp.exp(b_last - bc)).astype(BF16)
        decay = jnp.exp(b_last)
        for h in range(GLA_HEADS):
            ks = slice(h * dk, (h + 1) * dk)
            vs = slice(h * dv, (h + 1) * dv)
            qh = q_in[:, ks]
            vh = v_ref[rows, vs]
            att = jnp.where(causal, _dot_nt(qh, k_in[:, ks]), 0.0)
            st = st_ref[h]
            o = _dot(att.astype(BF16), vh) + _dot_nt(qh, st.astype(BF16))
            st_ref[h] = st * decay[:, ks] + _dot_tn(vh, k_st[:, ks])
            on = o * lax.rsqrt(jnp.mean(o * o, axis=-1, keepdims=True) + EPS) * gn[:, vs]
            o_ref[rows, vs] = (on * _silu(r_ref[rows, vs].astype(F32))).astype(BF16)


def _gla(proj, small, w_lr_pad, b_lr, g_norm, bsz, seq):
    n = proj.shape[0]
    hdk = w_lr_pad.shape[1]
    dk = hdk // GLA_HEADS
    hdv = g_norm.shape[1]
    dv = hdv // GLA_HEADS
    rows = min(GLA_ROWS, seq)
    nt = seq // rows
    assert hdv == 2 * hdk
    row = lambda b, t: b * nt + t
    return pl.pallas_call(
        functools.partial(_gla_body, chunk=GLA_CHUNK, nchunk=rows // GLA_CHUNK, dk=dk, dv=dv),
        grid=(bsz, nt),
        in_specs=[
            pl.BlockSpec((rows, hdk), lambda b, t: (row(b, t), 0)),
            pl.BlockSpec((rows, hdk), lambda b, t: (row(b, t), 1)),
            pl.BlockSpec((rows, hdv), lambda b, t: (row(b, t), 1)),
            pl.BlockSpec((rows, hdv), lambda b, t: (row(b, t), 2)),
            pl.BlockSpec((rows, SMALL_W), lambda b, t: (row(b, t), 0)),
            pl.BlockSpec((SMALL_W, hdk), lambda b, t: (0, 0)),
            pl.BlockSpec((1, hdk), lambda b, t: (0, 0)),
            pl.BlockSpec((1, hdv), lambda b, t: (0, 0)),
        ],
        out_specs=pl.BlockSpec((rows, hdv), lambda b, t: (row(b, t), 0)),
        out_shape=jax.ShapeDtypeStruct((n, hdv), BF16),
        scratch_shapes=[pltpu.VMEM((GLA_HEADS, dv, dk), F32)],
        compiler_params=_params("parallel", "arbitrary"),
        name="gla",
    )(proj, proj, proj, proj, small, w_lr_pad, b_lr, g_norm)


def _mlstm_body(xm_ref, z_ref, cw_ref, cb_ref, wq_ref, wk_ref, wv_ref, wif_ref, wift_ref, bif_ref, bift_ref,
                skip_ref, gn_ref, o_ref, xf_ref, q_sc, k_sc, v_sc, h_sc, c_sc, m_sc, *, chunk, nchunk, dh):
    rows_blk = chunk * nchunk
    halo = F32_SUBLANES

    @pl.when(pl.program_id(1) == 0)
    def _():
        xf_ref[0:halo, :] = jnp.zeros((halo, xf_ref.shape[1]), F32)
        c_sc[...] = jnp.zeros_like(c_sc)
        m_sc[...] = jnp.zeros_like(m_sc)

    xmb = xm_ref[...]
    xf_ref[halo:halo + rows_blk, :] = xmb.astype(F32)
    xf = xf_ref[...]
    cw = cw_ref[...]
    conv = cb_ref[...]
    for j in range(MLSTM_CONV - 1):
        back = MLSTM_CONV - 1 - j
        conv = conv + cw[j:j + 1, :] * pltpu.roll(xf, back, 0)[halo:, :]
    conv = conv + cw[MLSTM_CONV - 1:MLSTM_CONV, :] * xf[halo:, :]
    xf_ref[0:halo, :] = xf[rows_blk:rows_blk + halo, :]
    xc = _silu(conv)
    xcb = xc.astype(BF16)

    for h in range(MLSTM_HEADS):
        hs = slice(h * dh, (h + 1) * dh)
        q_sc[:, hs] = _dot(xcb[:, hs], wq_ref[h]).astype(BF16)
        k_sc[:, hs] = _dot(xcb[:, hs], wk_ref[h]).astype(BF16)
        v_sc[:, hs] = _dot(xmb[:, hs], wv_ref[h]).astype(BF16)
    qa, ka, va = q_sc[...], k_sc[...], v_sc[...]

    gcol = _dot(qa, wif_ref[0]) + _dot(ka, wif_ref[1]) + _dot(va, wif_ref[2]) + bif_ref[...]
    grow = _dot_nt(wift_ref[0], qa) + _dot_nt(wift_ref[1], ka) + _dot_nt(wift_ref[2], va) + bift_ref[...]
    lane = lax.broadcasted_iota(jnp.int32, gcol.shape, 1)
    gcol = jnp.where(lane < MLSTM_HEADS, gcol, _log_sigmoid(gcol))
    sub = lax.broadcasted_iota(jnp.int32, grow.shape, 0)
    grow = jnp.where(sub < MLSTM_HEADS, grow, _log_sigmoid(grow))

    tri = _tri(chunk).astype(F32)
    tri_t = _tri(chunk, upper=True).astype(F32)
    causal = _tri(chunk)
    qscale = dh ** -0.5
    ones_aug = jnp.ones((chunk, LANES), BF16)
    for c in range(nchunk):
        rows = slice(c * chunk, (c + 1) * chunk)
        gc = gcol[rows]
        gr = grow[:, rows]
        cum_c = _dot(tri, gc, precision=HIGHEST)
        cum_r = _dot(gr, tri_t, precision=HIGHEST)
        for h in range(MLSTM_HEADS):
            hs = slice(h * dh, (h + 1) * dh)
            fh = MLSTM_HEADS + h
            i_col, b_col = gc[:, h:h + 1], cum_c[:, fh:fh + 1]
            i_row, b_row = gr[h:h + 1, :], cum_r[fh:fh + 1, :]
            m_st = m_sc[h][:, 0:1]
            d_log = jnp.where(causal, b_col - b_row + i_row, -jnp.inf)
            m_inter = b_col + m_st
            m_t = jnp.maximum(m_inter, jnp.max(d_log, axis=-1, keepdims=True))
            w_intra = jnp.exp(d_log - m_t)
            w_inter = jnp.exp(m_inter - m_t)
            qc = (q_sc[rows, hs].astype(F32) * qscale).astype(BF16)
            kc = k_sc[rows, hs]
            vc = v_sc[rows, hs]
            s = _dot_nt(qc, kc) * w_intra
            c_aug = c_sc[h]
            q_state = _dot(qc, c_aug.astype(BF16))
            num = _dot(s.astype(BF16), vc) + w_inter * q_state[:, :dh]
            qn = jnp.sum(s, axis=-1, keepdims=True) + w_inter * q_state[:, dh:dh + 1]
            h_sc[rows, hs] = num / jnp.maximum(jnp.abs(qn), jnp.exp(-m_t))
            g = b_row[:, chunk - 1:chunk]
            m_new = jnp.maximum(g + m_st, jnp.max(g - b_row + i_row, axis=-1, keepdims=True))
            wa = jnp.exp(g - b_col + i_col - m_new)
            dec = jnp.exp(g + m_st - m_new)
            kw = (kc.astype(F32) * wa).astype(BF16)
            v_aug = jnp.concatenate([vc, ones_aug], axis=1)
            c_sc[h] = dec * c_aug + _dot_tn(kw, v_aug)
            m_sc[h] = jnp.broadcast_to(m_new, (1, LANES))

    gn = gn_ref[...]
    skip = skip_ref[...]
    zg = _silu(z_ref[...].astype(F32))
    for h in range(MLSTM_HEADS):
        hs = slice(h * dh, (h + 1) * dh)
        hh = h_sc[:, hs]
        hn = hh * lax.rsqrt(jnp.mean(hh * hh, axis=-1, keepdims=True) + EPS) * gn[:, hs]
        o_ref[:, hs] = ((hn + skip[:, hs] * xc[:, hs]) * zg[:, hs]).astype(BF16)


def _mlstm(proj, conv_w, conv_b, wq_bd, wk_bd, wv_bd, w_if, w_if_t, b_if, b_if_t, skip, g_norm, bsz, seq, col_blk):
    n = proj.shape[0]
    inner = conv_w.shape[1]
    dh = inner // MLSTM_HEADS
    rows = min(MLSTM_ROWS, seq)
    chunk = min(MLSTM_CHUNK, rows)
    nt = seq // rows
    row = lambda b, t: b * nt + t
    full = lambda shape: pl.BlockSpec(shape, lambda b, t: (0,) * len(shape))
    return pl.pallas_call(
        functools.partial(_mlstm_body, chunk=chunk, nchunk=rows // chunk, dh=dh),
        grid=(bsz, nt),
        in_specs=[
            pl.BlockSpec((rows, inner), lambda b, t: (row(b, t), col_blk)),
            pl.BlockSpec((rows, inner), lambda b, t: (row(b, t), col_blk + 1)),
            full((MLSTM_CONV, inner)),
            full((1, inner)),
            full((MLSTM_HEADS, dh, dh)),
            full((MLSTM_HEADS, dh, dh)),
            full((MLSTM_HEADS, dh, dh)),
            full((3, inner, LANES)),
            full((3, F32_SUBLANES, inner)),
            full((1, LANES)),
            full((F32_SUBLANES, 1)),
            full((1, inner)),
            full((1, inner)),
        ],
        out_specs=pl.BlockSpec((rows, inner), lambda b, t: (row(b, t), 0)),
        out_shape=jax.ShapeDtypeStruct((n, inner), BF16),
        scratch_shapes=[
            pltpu.VMEM((rows + F32_SUBLANES, inner), F32),
            pltpu.VMEM((rows, inner), BF16),
            pltpu.VMEM((rows, inner), BF16),
            pltpu.VMEM((rows, inner), BF16),
            pltpu.VMEM((rows, inner), F32),
            pltpu.VMEM((MLSTM_HEADS, dh, dh + LANES), F32),
            pltpu.VMEM((MLSTM_HEADS, 1, LANES), F32),
        ],
        compiler_params=_params("parallel", "arbitrary"),
        name="mlstm",
    )(proj, proj, conv_w, conv_b, wq_bd, wk_bd, wv_bd, w_if, w_if_t, b_if, b_if_t, skip, g_norm)


def _foxgate_body(s_ref, b_ref, o_ref, carry_ref, *, rows):
    @pl.when(pl.program_id(1) == 0)
    def _():
        carry_ref[...] = jnp.zeros_like(carry_ref)

    lf = _log_sigmoid(s_ref[...] + b_ref[...])
    cum = _dot(_tri(rows).astype(F32), lf, precision=HIGHEST) + carry_ref[...]
    o_ref[...] = cum
    carry_ref[...] = cum[rows - 1:rows, :]


def _foxgate(small, b_pad, bsz, seq):
    n = small.shape[0]
    rows = min(FOXGATE_ROWS, seq)
    nt = seq // rows
    return pl.pallas_call(
        functools.partial(_foxgate_body, rows=rows),
        grid=(bsz, nt),
        in_specs=[
            pl.BlockSpec((rows, SMALL_W), lambda b, t: (b * nt + t, 0)),
            pl.BlockSpec((1, SMALL_W), lambda b, t: (0, 0)),
        ],
        out_specs=pl.BlockSpec((rows, SMALL_W), lambda b, t: (b * nt + t, 0)),
        out_shape=jax.ShapeDtypeStruct((n, SMALL_W), F32),
        scratch_shapes=[pltpu.VMEM((1, SMALL_W), F32)],
        compiler_params=_params("parallel", "arbitrary"),
        name="foxgate",
    )(small, b_pad)


def _fox_body(q_ref, k_ref, v_ref, og_ref, fr_ref, o_ref, va_ref, acc_ref, s_ref, *, tq, dh, hpb):
    qi = pl.program_id(2)
    wa = dh + LANES

    @pl.when(qi == 0)
    def _():
        for h in range(hpb):
            va_ref[h, :, :dh] = v_ref[:, h * dh:(h + 1) * dh]
            va_ref[h, :, dh:] = jnp.ones((va_ref.shape[1], LANES), BF16)

    qs = [(q_ref[:, h * dh:(h + 1) * dh].astype(F32) * (dh ** -0.5 * LOG2E)).astype(BF16) for h in range(hpb)]
    acc_ref[...] = jnp.zeros_like(acc_ref)

    def scores(h, j):
        start = pl.multiple_of(j * tq, tq)
        return _dot_nt(qs[h], k_ref[pl.ds(start, tq), h * dh:(h + 1) * dh]) - fr_ref[0, h, j] * LOG2E

    def step(j, ms, slot, last):
        start = pl.multiple_of(j * tq, tq)
        out = []
        for h in range(hpb):
            s = s_ref[slot, h]
            if last:
                s = jnp.where(_tri(tq), s, -jnp.inf)
            else:
                s_ref[1 - slot, h] = scores(h, j + 1)
            m_new = jnp.maximum(ms[h], jnp.max(s, axis=-1, keepdims=True))
            p = jnp.exp2(s - m_new).astype(BF16)
            acc_ref[h] = jnp.exp2(ms[h] - m_new) * acc_ref[h] + _dot(p, va_ref[h, pl.ds(start, tq), :])
            out.append(m_new)
        return tuple(out)

    for h in range(hpb):
        s_ref[0, h] = scores(h, 0)
    m0 = tuple(jnp.full((tq, 1), -jnp.inf, F32) for _ in range(hpb))
    def pair(t, ms):
        return step(2 * t + 1, step(2 * t, ms, 0, False), 1, False)

    ms = lax.fori_loop(0, qi // 2, pair, m0)

    def finish(ms, slot):
        step(qi, ms, slot, True)
        for h in range(hpb):
            acc = acc_ref[h]
            gate = _sigmoid(og_ref[:, h * dh:(h + 1) * dh].astype(F32))
            o_ref[:, h * dh:(h + 1) * dh] = (acc[:, :dh] / acc[:, dh:] * gate).astype(BF16)

    @pl.when(qi % 2 == 0)
    def _():
        finish(ms, 0)

    @pl.when(qi % 2 == 1)
    def _():
        finish(step(qi - 1, ms, 0, False), 1)


def _fox(proj, f_row, bsz, seq, dh, q_blk, k_blk, v_blk, og_blk):
    n = proj.shape[0]
    tq = min(FOX_TQ, seq)
    nq = seq // tq
    hpb = FOX_HEADS_PER_STEP
    w = hpb * dh
    assert FOX_HEADS % hpb == 0 and q_blk % hpb == 0 and k_blk % hpb == 0 and v_blk % hpb == 0 and og_blk % hpb == 0
    return pl.pallas_call(
        functools.partial(_fox_body, tq=tq, dh=dh, hpb=hpb),
        grid=(bsz, FOX_HEADS // hpb, nq),
        in_specs=[
            pl.BlockSpec((tq, w), lambda b, h, i: (b * nq + i, q_blk // hpb + h)),
            pl.BlockSpec((seq, w), lambda b, h, i: (b, k_blk // hpb + h)),
            pl.BlockSpec((seq, w), lambda b, h, i: (b, v_blk // hpb + h)),
            pl.BlockSpec((tq, w), lambda b, h, i: (b * nq + i, og_blk // hpb + h)),
            pl.BlockSpec((1, hpb, nq, 1, tq), lambda b, h, i: (b, h, 0, 0, 0)),
        ],
        out_specs=pl.BlockSpec((tq, w), lambda b, h, i: (b * nq + i, h)),
        out_shape=jax.ShapeDtypeStruct((n, FOX_HEADS * dh), BF16),
        scratch_shapes=[pltpu.VMEM((hpb, seq, dh + LANES), BF16), pltpu.VMEM((hpb, tq, dh + LANES), F32),
                        pltpu.VMEM((2, hpb, tq, tq), F32)],
        compiler_params=_params("parallel", "parallel", "arbitrary"),
        name="fox",
    )(proj, proj, proj, proj, f_row)


def _merge_body(x_ref, y0_ref, y1_ref, y2_ref, gt_ref, bg_ref, wb_ref, wo_ref, o_ref, *, d):
    merged = None
    for n, y_ref in enumerate((y0_ref, y1_ref, y2_ref)):
        cs = slice(n * d, (n + 1) * d)
        gate = _sigmoid(gt_ref[:, cs].astype(F32) + bg_ref[:, cs])
        term = _dot(y_ref[...], wb_ref[n]) * gate
        merged = term if merged is None else merged + term
    o_ref[...] = x_ref[...] + _dot(merged.astype(BF16), wo_ref[...])


def _merge(x2, y_gla, y_mlstm, y_fox, proj, b_gate, w_branch, w_out, gate_blk):
    n, d = x2.shape
    tm = min(MERGE_TM, n)
    rowblk = lambda shape: pl.BlockSpec(shape, lambda i: (i, 0))
    return pl.pallas_call(
        functools.partial(_merge_body, d=d),
        grid=(n // tm,),
        in_specs=[
            rowblk((tm, d)), rowblk((tm, d)), rowblk((tm, d)), rowblk((tm, d)),
            pl.BlockSpec((tm, N_BRANCH * d), lambda i: (i, gate_blk)),
            pl.BlockSpec((1, N_BRANCH * d), lambda i: (0, 0)),
            pl.BlockSpec((N_BRANCH, d, d), lambda i: (0, 0, 0)),
            pl.BlockSpec((d, d), lambda i: (0, 0)),
        ],
        out_specs=rowblk((tm, d)),
        out_shape=jax.ShapeDtypeStruct((n, d), F32),
        compiler_params=_params("parallel"),
        name="merge",
    )(x2, y_gla, y_mlstm, y_fox, proj, b_gate, w_branch, w_out)


def _ffn_body(x_ref, xp_ref, g_ref, wu_ref, cw_ref, cb_ref, wd_ref, gf_ref, o_ref, h_ref,
              *, tm, halo, blocks_per_seq, dff, chunk, final):
    i = pl.program_id(0)

    def norm(x, gain):
        return x * lax.rsqrt(jnp.mean(x * x, axis=-1, keepdims=True) + EPS) * gain

    x = x_ref[...]
    hp = norm(xp_ref[...], g_ref[...]).astype(BF16)
    h_ref[0:halo, :] = jnp.where(i % blocks_per_seq != 0, hp, jnp.zeros_like(hp))
    h_ref[halo:halo + tm, :] = norm(x, g_ref[...]).astype(BF16)
    h = h_ref[...]

    def conv(lo, hi):
        u = _dot(h, wu_ref[:, lo:hi])
        y = cb_ref[:, lo:hi]
        for j in range(FFN_CONV - 1):
            y = y + cw_ref[j:j + 1, lo:hi] * pltpu.roll(u, FFN_CONV - 1 - j, 0)[halo:, :]
        return y + cw_ref[FFN_CONV - 1:FFN_CONV, lo:hi] * u[halo:, :]

    acc = x
    for lo in range(0, dff, chunk):
        hi = min(lo + chunk, dff)
        act = _silu(conv(dff + lo, dff + hi)) * conv(lo, hi)
        acc = acc + _dot(act.astype(BF16), wd_ref[lo:hi, :])
    o_ref[...] = norm(acc, gf_ref[...]) if final else acc


def _ffn(x2, g, w_up, conv_w, conv_b, w_down, seq, final_gain=None):
    n, d = x2.shape
    dff = w_down.shape[0]
    tm = min(FFN_TM, seq)
    halo = BF16_SUBLANES
    hb = tm // halo
    final = final_gain is not None
    resident = lambda shape: pl.BlockSpec(shape, lambda i: (0,) * len(shape), pipeline_mode=pl.Buffered(1))
    return pl.pallas_call(
        functools.partial(_ffn_body, tm=tm, halo=halo, blocks_per_seq=seq // tm, dff=dff, chunk=FFN_CHUNK,
                          final=final),
        grid=(n // tm,),
        in_specs=[
            pl.BlockSpec((tm, d), lambda i: (i, 0)),
            pl.BlockSpec((halo, d), lambda i: (jnp.maximum(i * hb - 1, 0), 0)),
            resident((1, d)),
            resident((d, 2 * dff)),
            resident((FFN_CONV, 2 * dff)),
            resident((1, 2 * dff)),
            resident((dff, d)),
            resident((1, d)),
        ],
        out_specs=pl.BlockSpec((tm, d), lambda i: (i, 0)),
        out_shape=jax.ShapeDtypeStruct((n, d), F32),
        scratch_shapes=[pltpu.VMEM((halo + tm, d), BF16)],
        compiler_params=_params("parallel"),
        name="ffn_final" if final else "ffn",
    )(x2, x2, g, w_up, conv_w, conv_b, w_down, final_gain if final else g)


def _block_diag_heads(w, n_heads):
    nblk, bc, bd = w.shape
    per = nblk // n_heads
    w4 = w.reshape(n_heads, per, bc, bd)
    eye = jnp.eye(per, dtype=w.dtype)
    return jnp.einsum("hncd,nm->hncmd", w4, eye).reshape(n_heads, per * bc, per * bd)


def _pad_lanes(a, width):
    return jnp.pad(a, ((0, 0),) * (a.ndim - 1) + ((0, width - a.shape[-1]),))


def kernel(x, norm_mix, w_in, b_gate, gla_w_lr, gla_b_lr, gla_norm, mlstm_conv_w, mlstm_conv_b, mlstm_wq, mlstm_wk,
           mlstm_wv, mlstm_w_i, mlstm_b_i, mlstm_w_f, mlstm_b_f, mlstm_skip, mlstm_norm, fox_b_f, w_branch, w_out,
           norm_ffn, ffn_w_up, ffn_conv_w, ffn_conv_b, ffn_w_down, norm_final):
    bsz, seq, d = x.shape
    depth = w_in.shape[0]
    hdk = gla_w_lr.shape[2]
    hdv = gla_norm.shape[1]
    inner = mlstm_conv_w.shape[2]
    fox_w = w_branch.shape[2]
    fox_dh = fox_w // FOX_HEADS
    dff = ffn_w_down.shape[1]
    assert hdv == d and inner == d and fox_w == d and 2 * hdk == d

    o_glr = 2 * hdk + hdv
    o_gr = o_glr + GLA_RANK
    o_ff = o_gr + hdv + 2 * inner + 3 * fox_w
    o_fog = o_ff + FOX_HEADS
    blk = {"gq": 0, "gk": 1, "gv": 1, "gr": 2, "mx": 3, "fq": 5 * d // fox_dh, "fk": 6 * d // fox_dh,
           "fv": 7 * d // fox_dh, "fog": 8 * d // fox_dh, "gates": 3}

    x2 = x.reshape(bsz * seq, d)
    tq = min(FOX_TQ, seq)
    for l in range(depth):
        wl = w_in[l]
        w_big = jnp.concatenate([wl[:, :o_glr], wl[:, o_gr:o_ff], wl[:, o_fog:]], axis=1).astype(BF16)
        w_small = _pad_lanes(jnp.concatenate([wl[:, o_glr:o_gr], wl[:, o_ff:o_fog]], axis=1), SMALL_W).astype(BF16)
        proj, small = _inproj(x2, norm_mix[l][None, :], w_big, w_small)

        w_lr_pad = jnp.pad(gla_w_lr[l], ((0, SMALL_W - GLA_RANK), (0, 0)))
        y_gla = _gla(proj, small, w_lr_pad, gla_b_lr[l][None, :], gla_norm[l][None, :], bsz, seq)

        w_if = _pad_lanes(jnp.concatenate([mlstm_w_i[l], mlstm_w_f[l]], axis=1), LANES)
        b_if = _pad_lanes(jnp.concatenate([mlstm_b_i[l], mlstm_b_f[l]])[None, :], LANES)
        w_if3 = w_if.reshape(3, inner, LANES).astype(BF16)
        w_if3_t = jnp.transpose(w_if3[:, :, :F32_SUBLANES], (0, 2, 1))
        b_if_t = b_if[0, :F32_SUBLANES][:, None]
        y_mlstm = _mlstm(
            proj, mlstm_conv_w[l], mlstm_conv_b[l][None, :],
            _block_diag_heads(mlstm_wq[l], MLSTM_HEADS).astype(BF16),
            _block_diag_heads(mlstm_wk[l], MLSTM_HEADS).astype(BF16),
            _block_diag_heads(mlstm_wv[l], MLSTM_HEADS).astype(BF16),
            w_if3, w_if3_t, b_if, b_if_t, mlstm_skip[l][None, :], mlstm_norm[l][None, :], bsz, seq, blk["mx"])

        b_fox = jnp.pad(fox_b_f[l], (GLA_RANK, SMALL_W - GLA_RANK - FOX_HEADS))[None, :]
        cum_f = _foxgate(small, b_fox, bsz, seq)[:, GLA_RANK:GLA_RANK + FOX_HEADS]
        cum_f = jnp.transpose(cum_f.reshape(bsz, seq, FOX_HEADS), (0, 2, 1))
        f_row = cum_f.reshape(bsz, FOX_HEADS, seq // tq, 1, tq)
        y_fox = _fox(proj, f_row, bsz, seq, fox_dh, blk["fq"], blk["fk"], blk["fv"], blk["fog"])

        x2 = _merge(x2, y_gla, y_mlstm, y_fox, proj, b_gate[l].reshape(1, N_BRANCH * d),
                    w_branch[l].astype(BF16), w_out[l].astype(BF16), blk["gates"])

        x2 = _ffn(x2, norm_ffn[l][None, :], ffn_w_up[l].astype(BF16), ffn_conv_w[l], ffn_conv_b[l][None, :],
                  ffn_w_down[l].astype(BF16), seq, final_gain=norm_final[None, :] if l == depth - 1 else None)
    return x2.reshape(bsz, seq, d)
```

```python
import functools

import jax
import jax.numpy as jnp
from jax import lax
from jax.experimental import pallas as pl
from jax.experimental.pallas import tpu as pltpu

F32 = jnp.float32
BF16 = jnp.bfloat16
HIGHEST = lax.Precision.HIGHEST
EPS = 1e-6
LOG2E = 1.4426950408889634

LANES = 128
F32_SUBLANES = 8
BF16_SUBLANES = 16
VMEM_LIMIT_BYTES = 56 * 1024 * 1024

GLA_HEADS = 4
GLA_RANK = 16
GLA_TAU = 16.0
GLA_CHUNK = 64
MLSTM_HEADS = 4
MLSTM_CONV = 4
MLSTM_BLOCK = 4
FOX_HEADS = 8
N_BRANCH = 3
FFN_CONV = 3
SMALL_W = LANES

INPROJ_TM = 1024
INPROJ_TN = 2048
GLA_ROWS = 256
MLSTM_CHUNK = 256
MLSTM_ROWS = 256
FOX_TQ = 512
FOX_HEADS_PER_STEP = 2
MERGE_TM = 512
FFN_TM = 512
FFN_CHUNK = 1024


def _params(*sem):
    return pltpu.CompilerParams(dimension_semantics=sem, vmem_limit_bytes=VMEM_LIMIT_BYTES)


def _log_sigmoid(z):
    return jnp.minimum(z, 0.0) - jnp.log1p(jnp.exp(-jnp.abs(z)))


def _sigmoid(z):
    return 1.0 / (1.0 + jnp.exp(-z))


def _silu(z):
    return z * _sigmoid(z)


def _tri(n, upper=False):
    r = lax.broadcasted_iota(jnp.int32, (n, n), 0)
    c = lax.broadcasted_iota(jnp.int32, (n, n), 1)
    return (r <= c) if upper else (r >= c)


def _dot(a, b, **kw):
    return jnp.dot(a, b, preferred_element_type=F32, **kw)


def _dot_nt(a, b):
    return lax.dot_general(a, b, (((1,), (1,)), ((), ())), preferred_element_type=F32)


def _dot_tn(a, b):
    return lax.dot_general(a, b, (((0,), (0,)), ((), ())), preferred_element_type=F32)


def _inproj_body(x_ref, g_ref, w_ref, ws_ref, wt_ref, o_ref, os_ref, ot_ref, h_ref):
    @pl.when(pl.program_id(1) == 0)
    def _():
        x = x_ref[...]
        ms = jnp.mean(x * x, axis=-1, keepdims=True)
        h = (x * lax.rsqrt(ms + EPS) * g_ref[...]).astype(BF16)
        h_ref[...] = h
        os_ref[...] = _dot(h, ws_ref[...])
        ot_ref[...] = _dot_nt(wt_ref[...], h)

    o_ref[...] = _dot(h_ref[...], w_ref[...]).astype(BF16)


def _inproj(x2, g, w_big, w_small, w_rows):
    n, d = x2.shape
    c = w_big.shape[1]
    nr = w_rows.shape[0]
    tm, tn = min(INPROJ_TM, n), min(INPROJ_TN, c)
    return pl.pallas_call(
        _inproj_body,
        grid=(n // tm, c // tn),
        in_specs=[
            pl.BlockSpec((tm, d), lambda i, j: (i, 0)),
            pl.BlockSpec((1, d), lambda i, j: (0, 0)),
            pl.BlockSpec((d, tn), lambda i, j: (0, j)),
            pl.BlockSpec((d, SMALL_W), lambda i, j: (0, 0)),
            pl.BlockSpec((nr, d), lambda i, j: (0, 0)),
        ],
        out_specs=[
            pl.BlockSpec((tm, tn), lambda i, j: (i, j)),
            pl.BlockSpec((tm, SMALL_W), lambda i, j: (i, 0)),
            pl.BlockSpec((nr, tm), lambda i, j: (0, i)),
        ],
        out_shape=[jax.ShapeDtypeStruct((n, c), BF16), jax.ShapeDtypeStruct((n, SMALL_W), F32),
                   jax.ShapeDtypeStruct((nr, n), F32)],
        scratch_shapes=[pltpu.VMEM((tm, d), BF16)],
        compiler_params=_params("parallel", "arbitrary"),
        name="inproj",
    )(x2, g, w_big, w_small, w_rows)


def _gla_body(q_ref, k_ref, v_ref, r_ref, s_ref, wlr_ref, blr_ref, gn_ref, o_ref, st_ref, *, chunk, nchunk, dk, dv):
    @pl.when(pl.program_id(1) == 0)
    def _():
        st_ref[...] = jnp.zeros_like(st_ref)

    z = _dot(s_ref[...], wlr_ref[...], precision=HIGHEST) + blr_ref[...]
    log_a = _log_sigmoid(z) * (1.0 / GLA_TAU)
    tri = _tri(chunk).astype(F32)
    causal = _tri(chunk)
    scale = dk ** -0.5
    gn = gn_ref[...]
    for c in range(nchunk):
        rows = slice(c * chunk, (c + 1) * chunk)
        bc = _dot(tri, log_a[rows], precision=HIGHEST)
        b_last = bc[chunk - 1:chunk, :]
        q = q_ref[rows, :].astype(F32) * scale
        k = k_ref[rows, :].astype(F32)
        q_in = (q * jnp.exp(bc)).astype(BF16)
        k_in = (k * jnp.exp(-bc)).astype(BF16)
        k_st = (k * jnp.exp(b_last - bc)).astype(BF16)
        decay = jnp.exp(b_last)
        for h in range(GLA_HEADS):
            ks = slice(h * dk, (h + 1) * dk)
            vs = slice(h * dv, (h + 1) * dv)
            qh = q_in[:, ks]
            vh = v_ref[rows, vs]
            att = jnp.where(causal, _dot_nt(qh, k_in[:, ks]), 0.0)
            st = st_ref[h]
            o = _dot(att.astype(BF16), vh) + _dot_nt(qh, st.astype(BF16))
            st_ref[h] = st * decay[:, ks] + _dot_tn(vh, k_st[:, ks])
            on = o * lax.rsqrt(jnp.mean(o * o, axis=-1, keepdims=True) + EPS) * gn[:, vs]
            o_ref[rows, vs] = (on * _silu(r_ref[rows, vs].astype(F32))).astype(BF16)


def _gla(proj, small, w_lr_pad, b_lr, g_norm, bsz, seq):
    n = proj.shape[0]
    hdk = w_lr_pad.shape[1]
    dk = hdk // GLA_HEADS
    hdv = g_norm.shape[1]
    dv = hdv // GLA_HEADS
    rows = min(GLA_ROWS, seq)
    nt = seq // rows
    assert hdv == 2 * hdk
    row = lambda b, t: b * nt + t
    return pl.pallas_call(
        functools.partial(_gla_body, chunk=GLA_CHUNK, nchunk=rows // GLA_CHUNK, dk=dk, dv=dv),
        grid=(bsz, nt),
        in_specs=[
            pl.BlockSpec((rows, hdk), lambda b, t: (row(b, t), 0)),
            pl.BlockSpec((rows, hdk), lambda b, t: (row(b, t), 1)),
            pl.BlockSpec((rows, hdv), lambda b, t: (row(b, t), 1)),
            pl.BlockSpec((rows, hdv), lambda b, t: (row(b, t), 2)),
            pl.BlockSpec((rows, SMALL_W), lambda b, t: (row(b, t), 0)),
            pl.BlockSpec((SMALL_W, hdk), lambda b, t: (0, 0)),
            pl.BlockSpec((1, hdk), lambda b, t: (0, 0)),
            pl.BlockSpec((1, hdv), lambda b, t: (0, 0)),
        ],
        out_specs=pl.BlockSpec((rows, hdv), lambda b, t: (row(b, t), 0)),
        out_shape=jax.ShapeDtypeStruct((n, hdv), BF16),
        scratch_shapes=[pltpu.VMEM((GLA_HEADS, dv, dk), F32)],
        compiler_params=_params("parallel", "arbitrary"),
        name="gla",
    )(proj, proj, proj, proj, small, w_lr_pad, b_lr, g_norm)


def _mlstm_body(xm_ref, z_ref, cw_ref, cb_ref, wq_ref, wk_ref, wv_ref, wif_ref, bif_ref,
                skip_ref, gn_ref, o_ref, xf_ref, q_sc, k_sc, v_sc, h_sc, xc_sc, c_sc, m_sc, *, chunk, nchunk, dh):
    rows_blk = chunk * nchunk
    halo = F32_SUBLANES

    @pl.when(pl.program_id(1) == 0)
    def _():
        xf_ref[0:halo, :] = jnp.zeros((halo, xf_ref.shape[1]), F32)
        c_sc[...] = jnp.zeros_like(c_sc)
        m_sc[...] = jnp.zeros_like(m_sc)

    xf_ref[halo:halo + rows_blk, :] = xm_ref[...].astype(F32)
    for h in range(MLSTM_HEADS):
        hs = slice(h * dh, (h + 1) * dh)
        xf = xf_ref[:, hs]
        conv = cb_ref[:, hs]
        for j in range(MLSTM_CONV - 1):
            conv = conv + cw_ref[j:j + 1, hs] * pltpu.roll(xf, MLSTM_CONV - 1 - j, 0)[halo:, :]
        conv = conv + cw_ref[MLSTM_CONV - 1:MLSTM_CONV, hs] * xf[halo:, :]
        xc = _silu(conv)
        xc_sc[:, hs] = xc
        xcb = xc.astype(BF16)
        q_sc[:, hs] = _dot(xcb, wq_ref[h]).astype(BF16)
        k_sc[:, hs] = _dot(xcb, wk_ref[h]).astype(BF16)
        v_sc[:, hs] = _dot(xm_ref[:, hs], wv_ref[h]).astype(BF16)
    xf_ref[0:halo, :] = xf_ref[rows_blk:rows_blk + halo, :]
    qa, ka, va = q_sc[...], k_sc[...], v_sc[...]

    gcol = _dot(qa, wif_ref[0]) + _dot(ka, wif_ref[1]) + _dot(va, wif_ref[2]) + bif_ref[...]
    lane = lax.broadcasted_iota(jnp.int32, gcol.shape, 1)
    gcol = jnp.where(lane < MLSTM_HEADS, gcol, _log_sigmoid(gcol))

    tri = _tri(chunk).astype(F32)
    causal = _tri(chunk)
    qscale = dh ** -0.5
    ones_aug = jnp.ones((chunk, LANES), BF16)
    for c in range(nchunk):
        rows = slice(c * chunk, (c + 1) * chunk)
        gc = gcol[rows]
        cum_c = _dot(tri, gc, precision=HIGHEST)
        gr = gc.T
        cum_r = cum_c.T
        for h in range(MLSTM_HEADS):
            hs = slice(h * dh, (h + 1) * dh)
            fh = MLSTM_HEADS + h
            i_col, b_col = gc[:, h:h + 1], cum_c[:, fh:fh + 1]
            i_row, b_row = gr[h:h + 1, :], cum_r[fh:fh + 1, :]
            m_st = m_sc[h][:, 0:1]
            d_log = jnp.where(causal, b_col - b_row + i_row, -jnp.inf)
            m_inter = b_col + m_st
            m_t = jnp.maximum(m_inter, jnp.max(d_log, axis=-1, keepdims=True))
            w_intra = jnp.exp(d_log - m_t)
            w_inter = jnp.exp(m_inter - m_t)
            qc = (q_sc[rows, hs].astype(F32) * qscale).astype(BF16)
            kc = k_sc[rows, hs]
            vc = v_sc[rows, hs]
            s = _dot_nt(qc, kc) * w_intra
            c_aug = c_sc[h]
            q_state = _dot(qc, c_aug.astype(BF16))
            num = _dot(s.astype(BF16), vc) + w_inter * q_state[:, :dh]
            qn = jnp.sum(s, axis=-1, keepdims=True) + w_inter * q_state[:, dh:dh + 1]
            h_sc[rows, hs] = num / jnp.maximum(jnp.abs(qn), jnp.exp(-m_t))
            g = b_row[:, chunk - 1:chunk]
            m_new = jnp.maximum(g + m_st, jnp.max(g - b_row + i_row, axis=-1, keepdims=True))
            wa = jnp.exp(g - b_col + i_col - m_new)
            dec = jnp.exp(g + m_st - m_new)
            kw = (kc.astype(F32) * wa).astype(BF16)
            v_aug = jnp.concatenate([vc, ones_aug], axis=1)
            c_sc[h] = dec * c_aug + _dot_tn(kw, v_aug)
            m_sc[h] = jnp.broadcast_to(m_new, (1, LANES))

    for h in range(MLSTM_HEADS):
        hs = slice(h * dh, (h + 1) * dh)
        hh = h_sc[:, hs]
        hn = hh * lax.rsqrt(jnp.mean(hh * hh, axis=-1, keepdims=True) + EPS) * gn_ref[:, hs]
        zg = _silu(z_ref[:, hs].astype(F32))
        o_ref[:, hs] = ((hn + skip_ref[:, hs] * xc_sc[:, hs]) * zg).astype(BF16)


def _mlstm(proj, conv_w, conv_b, wq_bd, wk_bd, wv_bd, w_if, b_if, skip, g_norm, bsz, seq, col_blk):
    n = proj.shape[0]
    inner = conv_w.shape[1]
    dh = inner // MLSTM_HEADS
    rows = min(MLSTM_ROWS, seq)
    chunk = min(MLSTM_CHUNK, rows)
    nt = seq // rows
    row = lambda b, t: b * nt + t
    full = lambda shape: pl.BlockSpec(shape, lambda b, t: (0,) * len(shape))
    return pl.pallas_call(
        functools.partial(_mlstm_body, chunk=chunk, nchunk=rows // chunk, dh=dh),
        grid=(bsz, nt),
        in_specs=[
            pl.BlockSpec((rows, inner), lambda b, t: (row(b, t), col_blk)),
            pl.BlockSpec((rows, inner), lambda b, t: (row(b, t), col_blk + 1)),
            full((MLSTM_CONV, inner)),
            full((1, inner)),
            full((MLSTM_HEADS, dh, dh)),
            full((MLSTM_HEADS, dh, dh)),
            full((MLSTM_HEADS, dh, dh)),
            full((3, inner, LANES)),
            full((1, LANES)),
            full((1, inner)),
            full((1, inner)),
        ],
        out_specs=pl.BlockSpec((rows, inner), lambda b, t: (row(b, t), 0)),
        out_shape=jax.ShapeDtypeStruct((n, inner), BF16),
        scratch_shapes=[
            pltpu.VMEM((rows + F32_SUBLANES, inner), F32),
            pltpu.VMEM((rows, inner), BF16),
            pltpu.VMEM((rows, inner), BF16),
            pltpu.VMEM((rows, inner), BF16),
            pltpu.VMEM((rows, inner), F32),
            pltpu.VMEM((rows, inner), F32),
            pltpu.VMEM((MLSTM_HEADS, dh, dh + LANES), F32),
            pltpu.VMEM((MLSTM_HEADS, 1, LANES), F32),
        ],
        compiler_params=_params("parallel", "arbitrary"),
        name="mlstm",
    )(proj, proj, conv_w, conv_b, wq_bd, wk_bd, wv_bd, w_if, b_if, skip, g_norm)


def _fox_body(q_ref, k_ref, v_ref, og_ref, ft_ref, bf_ref, o_ref, va_ref, acc_ref, s_ref, fr_ref, *, tq, nq, dh, hpb):
    hp = pl.program_id(1)
    qi = pl.program_id(2)

    @pl.when(qi == 0)
    def _():
        for h in range(hpb):
            va_ref[h, :, :dh] = v_ref[:, h * dh:(h + 1) * dh]
            va_ref[h, :, dh:] = jnp.ones((va_ref.shape[1], LANES), BF16)
        log_f = _log_sigmoid(ft_ref[...] + bf_ref[...])
        tri_u = _tri(tq, upper=True).astype(F32)
        carry = jnp.zeros((FOX_HEADS, 1), F32)
        for j in range(nq):
            cum = _dot(log_f[:, j * tq:(j + 1) * tq], tri_u, precision=HIGHEST) + carry
            carry = cum[:, tq - 1:tq]
            for r in range(FOX_HEADS):
                fr_ref[j, r] = cum[r:r + 1, :] * LOG2E

    qs = [(q_ref[:, h * dh:(h + 1) * dh].astype(F32) * (dh ** -0.5 * LOG2E)).astype(BF16) for h in range(hpb)]
    acc_ref[...] = jnp.zeros_like(acc_ref)

    def scores(h, j):
        start = pl.multiple_of(j * tq, tq)
        return _dot_nt(qs[h], k_ref[pl.ds(start, tq), h * dh:(h + 1) * dh]) - fr_ref[j, hp * hpb + h]

    def step(j, ms, slot, last):
        start = pl.multiple_of(j * tq, tq)
        out = []
        for h in range(hpb):
            s = s_ref[slot, h]
            if last:
                s = jnp.where(_tri(tq), s, -jnp.inf)
            else:
                s_ref[1 - slot, h] = scores(h, j + 1)
            m_new = jnp.maximum(ms[h], jnp.max(s, axis=-1, keepdims=True))
            p = jnp.exp2(s - m_new).astype(BF16)
            acc_ref[h] = jnp.exp2(ms[h] - m_new) * acc_ref[h] + _dot(p, va_ref[h, pl.ds(start, tq), :])
            out.append(m_new)
        return tuple(out)

    for h in range(hpb):
        s_ref[0, h] = scores(h, 0)
    m0 = tuple(jnp.full((tq, 1), -jnp.inf, F32) for _ in range(hpb))
    def pair(t, ms):
        return step(2 * t + 1, step(2 * t, ms, 0, False), 1, False)

    ms = lax.fori_loop(0, qi // 2, pair, m0)

    def finish(ms, slot):
        step(qi, ms, slot, True)
        for h in range(hpb):
            acc = acc_ref[h]
            gate = _sigmoid(og_ref[:, h * dh:(h + 1) * dh].astype(F32))
            o_ref[:, h * dh:(h + 1) * dh] = (acc[:, :dh] / acc[:, dh:] * gate).astype(BF16)

    @pl.when(qi % 2 == 0)
    def _():
        finish(ms, 0)

    @pl.when(qi % 2 == 1)
    def _():
        finish(step(qi - 1, ms, 0, False), 1)


def _fox(proj, f_logit_t, b_f, bsz, seq, dh, q_blk, k_blk, v_blk, og_blk):
    n = proj.shape[0]
    tq = min(FOX_TQ, seq)
    nq = seq // tq
    hpb = FOX_HEADS_PER_STEP
    w = hpb * dh
    assert FOX_HEADS % hpb == 0 and q_blk % hpb == 0 and k_blk % hpb == 0 and v_blk % hpb == 0 and og_blk % hpb == 0
    return pl.pallas_call(
        functools.partial(_fox_body, tq=tq, nq=nq, dh=dh, hpb=hpb),
        grid=(bsz, FOX_HEADS // hpb, nq),
        in_specs=[
            pl.BlockSpec((tq, w), lambda b, h, i: (b * nq + i, q_blk // hpb + h)),
            pl.BlockSpec((seq, w), lambda b, h, i: (b, k_blk // hpb + h)),
            pl.BlockSpec((seq, w), lambda b, h, i: (b, v_blk // hpb + h)),
            pl.BlockSpec((tq, w), lambda b, h, i: (b * nq + i, og_blk // hpb + h)),
            pl.BlockSpec((FOX_HEADS, seq), lambda b, h, i: (0, b)),
            pl.BlockSpec((FOX_HEADS, 1), lambda b, h, i: (0, 0)),
        ],
        out_specs=pl.BlockSpec((tq, w), lambda b, h, i: (b * nq + i, h)),
        out_shape=jax.ShapeDtypeStruct((n, FOX_HEADS * dh), BF16),
        scratch_shapes=[pltpu.VMEM((hpb, seq, dh + LANES), BF16), pltpu.VMEM((hpb, tq, dh + LANES), F32),
                        pltpu.VMEM((2, hpb, tq, tq), F32), pltpu.VMEM((nq, FOX_HEADS, 1, tq), F32)],
        compiler_params=_params("parallel", "parallel", "arbitrary"),
        name="fox",
    )(proj, proj, proj, proj, f_logit_t, b_f)


def _merge_body(x_ref, y0_ref, y1_ref, y2_ref, gt_ref, bg_ref, wb_ref, wo_ref, o_ref, *, d):
    merged = None
    for n, y_ref in enumerate((y0_ref, y1_ref, y2_ref)):
        cs = slice(n * d, (n + 1) * d)
        gate = _sigmoid(gt_ref[:, cs].astype(F32) + bg_ref[:, cs])
        term = _dot(y_ref[...], wb_ref[n]) * gate
        merged = term if merged is None else merged + term
    o_ref[...] = x_ref[...] + _dot(merged.astype(BF16), wo_ref[...])


def _merge(x2, y_gla, y_mlstm, y_fox, proj, b_gate, w_branch, w_out, gate_blk):
    n, d = x2.shape
    tm = min(MERGE_TM, n)
    rowblk = lambda shape: pl.BlockSpec(shape, lambda i: (i, 0))
    return pl.pallas_call(
        functools.partial(_merge_body, d=d),
        grid=(n // tm,),
        in_specs=[
            rowblk((tm, d)), rowblk((tm, d)), rowblk((tm, d)), rowblk((tm, d)),
            pl.BlockSpec((tm, N_BRANCH * d), lambda i: (i, gate_blk)),
            pl.BlockSpec((1, N_BRANCH * d), lambda i: (0, 0)),
            pl.BlockSpec((N_BRANCH, d, d), lambda i: (0, 0, 0)),
            pl.BlockSpec((d, d), lambda i: (0, 0)),
        ],
        out_specs=rowblk((tm, d)),
        out_shape=jax.ShapeDtypeStruct((n, d), F32),
        compiler_params=_params("parallel"),
        name="merge",
    )(x2, y_gla, y_mlstm, y_fox, proj, b_gate, w_branch, w_out)


def _ffn_body(x_ref, xp_ref, g_ref, wu_ref, cw_ref, cb_ref, wd_ref, gf_ref, o_ref, h_ref,
              *, tm, halo, blocks_per_seq, dff, chunk, final):
    i = pl.program_id(0)

    def norm(x, gain):
        return x * lax.rsqrt(jnp.mean(x * x, axis=-1, keepdims=True) + EPS) * gain

    x = x_ref[...]
    hp = norm(xp_ref[...], g_ref[...]).astype(BF16)
    h_ref[0:halo, :] = jnp.where(i % blocks_per_seq != 0, hp, jnp.zeros_like(hp))
    h_ref[halo:halo + tm, :] = norm(x, g_ref[...]).astype(BF16)
    h = h_ref[...]

    def conv(lo, hi):
        u = _dot(h, wu_ref[:, lo:hi])
        y = cb_ref[:, lo:hi]
        for j in range(FFN_CONV - 1):
            y = y + cw_ref[j:j + 1, lo:hi] * pltpu.roll(u, FFN_CONV - 1 - j, 0)[halo:, :]
        return y + cw_ref[FFN_CONV - 1:FFN_CONV, lo:hi] * u[halo:, :]

    acc = x
    for lo in range(0, dff, chunk):
        hi = min(lo + chunk, dff)
        act = _silu(conv(dff + lo, dff + hi)) * conv(lo, hi)
        acc = acc + _dot(act.astype(BF16), wd_ref[lo:hi, :])
    o_ref[...] = norm(acc, gf_ref[...]) if final else acc


def _ffn(x2, g, w_up, conv_w, conv_b, w_down, seq, final_gain=None):
    n, d = x2.shape
    dff = w_down.shape[0]
    tm = min(FFN_TM, seq)
    halo = BF16_SUBLANES
    hb = tm // halo
    final = final_gain is not None
    resident = lambda shape: pl.BlockSpec(shape, lambda i: (0,) * len(shape), pipeline_mode=pl.Buffered(1))
    return pl.pallas_call(
        functools.partial(_ffn_body, tm=tm, halo=halo, blocks_per_seq=seq // tm, dff=dff, chunk=FFN_CHUNK,
                          final=final),
        grid=(n // tm,),
        in_specs=[
            pl.BlockSpec((tm, d), lambda i: (i, 0)),
            pl.BlockSpec((halo, d), lambda i: (jnp.maximum(i * hb - 1, 0), 0)),
            resident((1, d)),
            resident((d, 2 * dff)),
            resident((FFN_CONV, 2 * dff)),
            resident((1, 2 * dff)),
            resident((dff, d)),
            resident((1, d)),
        ],
        out_specs=pl.BlockSpec((tm, d), lambda i: (i, 0)),
        out_shape=jax.ShapeDtypeStruct((n, d), F32),
        scratch_shapes=[pltpu.VMEM((halo + tm, d), BF16)],
        compiler_params=_params("parallel"),
        name="ffn_final" if final else "ffn",
    )(x2, x2, g, w_up, conv_w, conv_b, w_down, final_gain if final else g)


def _block_diag_heads(w, n_heads):
    nblk, bc, bd = w.shape
    per = nblk // n_heads
    tiled = jnp.tile(w.reshape(n_heads, per * bc, bd), (1, 1, per))
    same_block = (jnp.arange(per * bc)[:, None] // bc) == (jnp.arange(per * bd)[None, :] // bd)
    return jnp.where(same_block, tiled, 0.0)


def _pad_lanes(a, width):
    return jnp.pad(a, ((0, 0),) * (a.ndim - 1) + ((0, width - a.shape[-1]),))


def kernel(x, norm_mix, w_in, b_gate, gla_w_lr, gla_b_lr, gla_norm, mlstm_conv_w, mlstm_conv_b, mlstm_wq, mlstm_wk,
           mlstm_wv, mlstm_w_i, mlstm_b_i, mlstm_w_f, mlstm_b_f, mlstm_skip, mlstm_norm, fox_b_f, w_branch, w_out,
           norm_ffn, ffn_w_up, ffn_conv_w, ffn_conv_b, ffn_w_down, norm_final):
    bsz, seq, d = x.shape
    depth = w_in.shape[0]
    hdk = gla_w_lr.shape[2]
    hdv = gla_norm.shape[1]
    inner = mlstm_conv_w.shape[2]
    fox_w = w_branch.shape[2]
    fox_dh = fox_w // FOX_HEADS
    dff = ffn_w_down.shape[1]
    assert hdv == d and inner == d and fox_w == d and 2 * hdk == d

    o_glr = 2 * hdk + hdv
    o_gr = o_glr + GLA_RANK
    o_ff = o_gr + hdv + 2 * inner + 3 * fox_w
    o_fog = o_ff + FOX_HEADS
    blk = {"gq": 0, "gk": 1, "gv": 1, "gr": 2, "mx": 3, "fq": 5 * d // fox_dh, "fk": 6 * d // fox_dh,
           "fv": 7 * d // fox_dh, "fog": 8 * d // fox_dh, "gates": 3}

    x2 = x.reshape(bsz * seq, d)
    for l in range(depth):
        wl = w_in[l]
        w_big = jnp.concatenate([wl[:, :o_glr], wl[:, o_gr:o_ff], wl[:, o_fog:]], axis=1).astype(BF16)
        w_small = _pad_lanes(wl[:, o_glr:o_gr], SMALL_W).astype(BF16)
        w_ff_t = jnp.transpose(wl[:, o_ff:o_fog]).astype(BF16)
        proj, small, f_logit_t = _inproj(x2, norm_mix[l][None, :], w_big, w_small, w_ff_t)

        w_lr_pad = jnp.pad(gla_w_lr[l], ((0, SMALL_W - GLA_RANK), (0, 0)))
        y_gla = _gla(proj, small, w_lr_pad, gla_b_lr[l][None, :], gla_norm[l][None, :], bsz, seq)

        w_if = _pad_lanes(jnp.concatenate([mlstm_w_i[l], mlstm_w_f[l]], axis=1), LANES)
        b_if = _pad_lanes(jnp.concatenate([mlstm_b_i[l], mlstm_b_f[l]])[None, :], LANES)
        w_if3 = w_if.reshape(3, inner, LANES).astype(BF16)
        y_mlstm = _mlstm(
            proj, mlstm_conv_w[l], mlstm_conv_b[l][None, :],
            _block_diag_heads(mlstm_wq[l], MLSTM_HEADS).astype(BF16),
            _block_diag_heads(mlstm_wk[l], MLSTM_HEADS).astype(BF16),
            _block_diag_heads(mlstm_wv[l], MLSTM_HEADS).astype(BF16),
            w_if3, b_if, mlstm_skip[l][None, :], mlstm_norm[l][None, :], bsz, seq, blk["mx"])

        y_fox = _fox(proj, f_logit_t, fox_b_f[l][:, None], bsz, seq, fox_dh,
                     blk["fq"], blk["fk"], blk["fv"], blk["fog"])

        x2 = _merge(x2, y_gla, y_mlstm, y_fox, proj, b_gate[l].reshape(1, N_BRANCH * d),
                    w_branch[l].astype(BF16), w_out[l].astype(BF16), blk["gates"])

        x2 = _ffn(x2, norm_ffn[l][None, :], ffn_w_up[l].astype(BF16), ffn_conv_w[l], ffn_conv_b[l][None, :],
                  ffn_w_down[l].astype(BF16), seq, final_gain=norm_final[None, :] if l == depth - 1 else None)
    return x2.reshape(bsz, seq, d)
```

```python
import functools

import jax
import jax.numpy as jnp
from jax import lax
from jax.experimental import pallas as pl
from jax.experimental.pallas import tpu as pltpu

F32 = jnp.float32
BF16 = jnp.bfloat16
HIGHEST = lax.Precision.HIGHEST
EPS = 1e-6
LOG2E = 1.4426950408889634

LANES = 128
F32_SUBLANES = 8
BF16_SUBLANES = 16
VMEM_LIMIT_BYTES = 56 * 1024 * 1024

GLA_HEADS = 4
GLA_RANK = 16
GLA_TAU = 16.0
GLA_CHUNK = 64
MLSTM_HEADS = 4
MLSTM_CONV = 4
MLSTM_BLOCK = 4
FOX_HEADS = 8
N_BRANCH = 3
FFN_CONV = 3
SMALL_W = LANES

INPROJ_TM = 1024
INPROJ_TN = 2048
GLA_ROWS = 256
MLSTM_CHUNK = 256
MLSTM_ROWS = 256
FOX_TQ = 512
FOX_HEADS_PER_STEP = 2
MERGE_TM = 512
FFN_TM = 512
FFN_CHUNK = 1024


def _params(*sem):
    return pltpu.CompilerParams(dimension_semantics=sem, vmem_limit_bytes=VMEM_LIMIT_BYTES)


def _log_sigmoid(z):
    return jnp.minimum(z, 0.0) - jnp.log1p(jnp.exp(-jnp.abs(z)))


def _sigmoid(z):
    return 1.0 / (1.0 + jnp.exp(-z))


def _silu(z):
    return z * _sigmoid(z)


def _tri(n, upper=False):
    r = lax.broadcasted_iota(jnp.int32, (n, n), 0)
    c = lax.broadcasted_iota(jnp.int32, (n, n), 1)
    return (r <= c) if upper else (r >= c)


def _dot(a, b, **kw):
    return jnp.dot(a, b, preferred_element_type=F32, **kw)


def _dot_nt(a, b):
    return lax.dot_general(a, b, (((1,), (1,)), ((), ())), preferred_element_type=F32)


def _dot_tn(a, b):
    return lax.dot_general(a, b, (((0,), (0,)), ((), ())), preferred_element_type=F32)


def _inproj_body(x_ref, g_ref, *refs, bounds):
    w_refs = refs[:len(bounds)]
    ws_ref, o_ref, os_ref, ot_ref, h_ref = refs[len(bounds):]
    j = pl.program_id(1)

    @pl.when(j == 0)
    def _():
        x = x_ref[...]
        ms = jnp.mean(x * x, axis=-1, keepdims=True)
        h = (x * lax.rsqrt(ms + EPS) * g_ref[...]).astype(BF16)
        h_ref[...] = h
        small = _dot(h, ws_ref[...])
        os_ref[...] = small
        ot_ref[...] = small.T

    for w_ref, (lo, hi) in zip(w_refs, bounds):
        @pl.when(jnp.logical_and(j >= lo, j < hi))
        def _():
            o_ref[...] = _dot(h_ref[...], w_ref[...]).astype(BF16)


def _inproj(x2, g, w_segments, w_small):
    n, d = x2.shape
    c = sum(w.shape[1] for w in w_segments)
    tm, tn = min(INPROJ_TM, n), min(INPROJ_TN, c)
    bounds, lo = [], 0
    for w in w_segments:
        assert w.shape[1] % tn == 0
        bounds.append((lo, lo + w.shape[1] // tn))
        lo = bounds[-1][1]
    seg_spec = lambda lo, hi: pl.BlockSpec((d, tn), lambda i, j: (0, jnp.clip(j - lo, 0, hi - lo - 1)))
    return pl.pallas_call(
        functools.partial(_inproj_body, bounds=tuple(bounds)),
        grid=(n // tm, c // tn),
        in_specs=[
            pl.BlockSpec((tm, d), lambda i, j: (i, 0)),
            pl.BlockSpec((1, d), lambda i, j: (0, 0)),
            *[seg_spec(lo, hi) for lo, hi in bounds],
            pl.BlockSpec((d, SMALL_W), lambda i, j: (0, 0)),
        ],
        out_specs=[
            pl.BlockSpec((tm, tn), lambda i, j: (i, j)),
            pl.BlockSpec((tm, SMALL_W), lambda i, j: (i, 0)),
            pl.BlockSpec((SMALL_W, tm), lambda i, j: (0, i)),
        ],
        out_shape=[jax.ShapeDtypeStruct((n, c), BF16), jax.ShapeDtypeStruct((n, SMALL_W), F32),
                   jax.ShapeDtypeStruct((SMALL_W, n), F32)],
        scratch_shapes=[pltpu.VMEM((tm, d), BF16)],
        compiler_params=_params("parallel", "arbitrary"),
        name="inproj",
    )(x2, g, *w_segments, w_small)


def _gla_body(q_ref, k_ref, v_ref, r_ref, s_ref, wlr_ref, blr_ref, gn_ref, o_ref, st_ref, *, chunk, nchunk, dk, dv):
    @pl.when(pl.program_id(1) == 0)
    def _():
        st_ref[...] = jnp.zeros_like(st_ref)

    z = _dot(s_ref[...], wlr_ref[...], precision=HIGHEST) + blr_ref[...]
    log_a = _log_sigmoid(z) * (1.0 / GLA_TAU)
    tri = _tri(chunk).astype(F32)
    causal = _tri(chunk)
    scale = dk ** -0.5
    gn = gn_ref[...]
    for c in range(nchunk):
        rows = slice(c * chunk, (c + 1) * chunk)
        bc = _dot(tri, log_a[rows], precision=HIGHEST)
        b_last = bc[chunk - 1:chunk, :]
        q = q_ref[rows, :].astype(F32) * scale
        k = k_ref[rows, :].astype(F32)
        q_in = (q * jnp.exp(bc)).astype(BF16)
        k_in = (k * jnp.exp(-bc)).astype(BF16)
        k_st = (k * jnp.exp(b_last - bc)).astype(BF16)
        decay = jnp.exp(b_last)
        for h in range(GLA_HEADS):
            ks = slice(h * dk, (h + 1) * dk)
            vs = slice(h * dv, (h + 1) * dv)
            qh = q_in[:, ks]
            vh = v_ref[rows, vs]
            att = jnp.where(causal, _dot_nt(qh, k_in[:, ks]), 0.0)
            st = st_ref[h]
            o = _dot(att.astype(BF16), vh) + _dot_nt(qh, st.astype(BF16))
            st_ref[h] = st * decay[:, ks] + _dot_tn(vh, k_st[:, ks])
            on = o * lax.rsqrt(jnp.mean(o * o, axis=-1, keepdims=True) + EPS) * gn[:, vs]
            o_ref[rows, vs] = (on * _silu(r_ref[rows, vs].astype(F32))).astype(BF16)


def _gla(proj, small, w_lr_pad, b_lr, g_norm, bsz, seq):
    n = proj.shape[0]
    hdk = w_lr_pad.shape[1]
    dk = hdk // GLA_HEADS
    hdv = g_norm.shape[1]
    dv = hdv // GLA_HEADS
    rows = min(GLA_ROWS, seq)
    nt = seq // rows
    assert hdv == 2 * hdk
    row = lambda b, t: b * nt + t
    return pl.pallas_call(
        functools.partial(_gla_body, chunk=GLA_CHUNK, nchunk=rows // GLA_CHUNK, dk=dk, dv=dv),
        grid=(bsz, nt),
        in_specs=[
            pl.BlockSpec((rows, hdk), lambda b, t: (row(b, t), 0)),
            pl.BlockSpec((rows, hdk), lambda b, t: (row(b, t), 1)),
            pl.BlockSpec((rows, hdv), lambda b, t: (row(b, t), 1)),
            pl.BlockSpec((rows, hdv), lambda b, t: (row(b, t), 2)),
            pl.BlockSpec((rows, SMALL_W), lambda b, t: (row(b, t), 0)),
            pl.BlockSpec((SMALL_W, hdk), lambda b, t: (0, 0)),
            pl.BlockSpec((1, hdk), lambda b, t: (0, 0)),
            pl.BlockSpec((1, hdv), lambda b, t: (0, 0)),
        ],
        out_specs=pl.BlockSpec((rows, hdv), lambda b, t: (row(b, t), 0)),
        out_shape=jax.ShapeDtypeStruct((n, hdv), BF16),
        scratch_shapes=[pltpu.VMEM((GLA_HEADS, dv, dk), F32)],
        compiler_params=_params("parallel", "arbitrary"),
        name="gla",
    )(proj, proj, proj, proj, small, w_lr_pad, b_lr, g_norm)


def _mlstm_body(xm_ref, z_ref, cw_ref, cb_ref, wq_ref, wk_ref, wv_ref, wif_ref, bif_ref,
                skip_ref, gn_ref, o_ref, xf_ref, q_sc, k_sc, v_sc, h_sc, xc_sc, c_sc, m_sc, *, chunk, nchunk, dh):
    rows_blk = chunk * nchunk
    halo = F32_SUBLANES

    @pl.when(pl.program_id(1) == 0)
    def _():
        xf_ref[0:halo, :] = jnp.zeros((halo, xf_ref.shape[1]), F32)
        c_sc[...] = jnp.zeros_like(c_sc)
        m_sc[...] = jnp.zeros_like(m_sc)

    xf_ref[halo:halo + rows_blk, :] = xm_ref[...].astype(F32)
    for h in range(MLSTM_HEADS):
        hs = slice(h * dh, (h + 1) * dh)
        xf = xf_ref[:, hs]
        conv = cb_ref[:, hs]
        for j in range(MLSTM_CONV - 1):
            conv = conv + cw_ref[j:j + 1, hs] * pltpu.roll(xf, MLSTM_CONV - 1 - j, 0)[halo:, :]
        conv = conv + cw_ref[MLSTM_CONV - 1:MLSTM_CONV, hs] * xf[halo:, :]
        xc = _silu(conv)
        xc_sc[:, hs] = xc
        xcb = xc.astype(BF16)
        q_sc[:, hs] = _dot(xcb, wq_ref[h]).astype(BF16)
        k_sc[:, hs] = _dot(xcb, wk_ref[h]).astype(BF16)
        v_sc[:, hs] = _dot(xm_ref[:, hs], wv_ref[h]).astype(BF16)
    xf_ref[0:halo, :] = xf_ref[rows_blk:rows_blk + halo, :]
    qa, ka, va = q_sc[...], k_sc[...], v_sc[...]

    gcol = _dot(qa, wif_ref[0]) + _dot(ka, wif_ref[1]) + _dot(va, wif_ref[2]) + bif_ref[...]
    lane = lax.broadcasted_iota(jnp.int32, gcol.shape, 1)
    gcol = jnp.where(lane < MLSTM_HEADS, gcol, _log_sigmoid(gcol))

    tri = _tri(chunk).astype(F32)
    causal = _tri(chunk)
    qscale = dh ** -0.5
    ones_aug = jnp.ones((chunk, LANES), BF16)
    for c in range(nchunk):
        rows = slice(c * chunk, (c + 1) * chunk)
        gc = gcol[rows]
        cum_c = _dot(tri, gc, precision=HIGHEST)
        gr = gc.T
        cum_r = cum_c.T
        for h in range(MLSTM_HEADS):
            hs = slice(h * dh, (h + 1) * dh)
            fh = MLSTM_HEADS + h
            i_col, b_col = gc[:, h:h + 1], cum_c[:, fh:fh + 1]
            i_row, b_row = gr[h:h + 1, :], cum_r[fh:fh + 1, :]
            m_st = m_sc[h][:, 0:1]
            d_log = jnp.where(causal, b_col - b_row + i_row, -jnp.inf)
            m_inter = b_col + m_st
            m_t = jnp.maximum(m_inter, jnp.max(d_log, axis=-1, keepdims=True))
            w_intra = jnp.exp(d_log - m_t)
            w_inter = jnp.exp(m_inter - m_t)
            qc = (q_sc[rows, hs].astype(F32) * qscale).astype(BF16)
            kc = k_sc[rows, hs]
            vc = v_sc[rows, hs]
            s = _dot_nt(qc, kc) * w_intra
            c_aug = c_sc[h]
            q_state = _dot(qc, c_aug.astype(BF16))
            num = _dot(s.astype(BF16), vc) + w_inter * q_state[:, :dh]
            qn = jnp.sum(s, axis=-1, keepdims=True) + w_inter * q_state[:, dh:dh + 1]
            h_sc[rows, hs] = num / jnp.maximum(jnp.abs(qn), jnp.exp(-m_t))
            g = b_row[:, chunk - 1:chunk]
            m_new = jnp.maximum(g + m_st, jnp.max(g - b_row + i_row, axis=-1, keepdims=True))
            wa = jnp.exp(g - b_col + i_col - m_new)
            dec = jnp.exp(g + m_st - m_new)
            kw = (kc.astype(F32) * wa).astype(BF16)
            v_aug = jnp.concatenate([vc, ones_aug], axis=1)
            c_sc[h] = dec * c_aug + _dot_tn(kw, v_aug)
            m_sc[h] = jnp.broadcast_to(m_new, (1, LANES))

    for h in range(MLSTM_HEADS):
        hs = slice(h * dh, (h + 1) * dh)
        hh = h_sc[:, hs]
        hn = hh * lax.rsqrt(jnp.mean(hh * hh, axis=-1, keepdims=True) + EPS) * gn_ref[:, hs]
        zg = _silu(z_ref[:, hs].astype(F32))
        o_ref[:, hs] = ((hn + skip_ref[:, hs] * xc_sc[:, hs]) * zg).astype(BF16)


def _mlstm(proj, conv_w, conv_b, wq_bd, wk_bd, wv_bd, w_if, b_if, skip, g_norm, bsz, seq, col_blk):
    n = proj.shape[0]
    inner = conv_w.shape[1]
    dh = inner // MLSTM_HEADS
    rows = min(MLSTM_ROWS, seq)
    chunk = min(MLSTM_CHUNK, rows)
    nt = seq // rows
    row = lambda b, t: b * nt + t
    full = lambda shape: pl.BlockSpec(shape, lambda b, t: (0,) * len(shape))
    return pl.pallas_call(
        functools.partial(_mlstm_body, chunk=chunk, nchunk=rows // chunk, dh=dh),
        grid=(bsz, nt),
        in_specs=[
            pl.BlockSpec((rows, inner), lambda b, t: (row(b, t), col_blk)),
            pl.BlockSpec((rows, inner), lambda b, t: (row(b, t), col_blk + 1)),
            full((MLSTM_CONV, inner)),
            full((1, inner)),
            full((MLSTM_HEADS, dh, dh)),
            full((MLSTM_HEADS, dh, dh)),
            full((MLSTM_HEADS, dh, dh)),
            full((3, inner, LANES)),
            full((1, LANES)),
            full((1, inner)),
            full((1, inner)),
        ],
        out_specs=pl.BlockSpec((rows, inner), lambda b, t: (row(b, t), 0)),
        out_shape=jax.ShapeDtypeStruct((n, inner), BF16),
        scratch_shapes=[
            pltpu.VMEM((rows + F32_SUBLANES, inner), F32),
            pltpu.VMEM((rows, inner), BF16),
            pltpu.VMEM((rows, inner), BF16),
            pltpu.VMEM((rows, inner), BF16),
            pltpu.VMEM((rows, inner), F32),
            pltpu.VMEM((rows, inner), F32),
            pltpu.VMEM((MLSTM_HEADS, dh, dh + LANES), F32),
            pltpu.VMEM((MLSTM_HEADS, 1, LANES), F32),
        ],
        compiler_params=_params("parallel", "arbitrary"),
        name="mlstm",
    )(proj, proj, conv_w, conv_b, wq_bd, wk_bd, wv_bd, w_if, b_if, skip, g_norm)


def _fox_body(q_ref, k_ref, v_ref, og_ref, ft_ref, bf_ref, o_ref, va_ref, acc_ref, s_ref, fr_ref, *, tq, nq, dh, hpb):
    hp = pl.program_id(1)
    qi = pl.program_id(2)

    @pl.when(qi == 0)
    def _():
        for h in range(hpb):
            va_ref[h, :, :dh] = v_ref[:, h * dh:(h + 1) * dh]
            va_ref[h, :, dh:] = jnp.ones((va_ref.shape[1], LANES), BF16)

    @pl.when(jnp.logical_and(qi == 0, hp == 0))
    def _():
        log_f = _log_sigmoid(ft_ref[...] + bf_ref[...])
        tri_u = _tri(tq, upper=True).astype(F32)
        carry = jnp.zeros((FOX_HEADS, 1), F32)
        for j in range(nq):
            cum = _dot(log_f[:, j * tq:(j + 1) * tq], tri_u, precision=HIGHEST) + carry
            carry = cum[:, tq - 1:tq]
            for r in range(FOX_HEADS):
                fr_ref[j, r] = cum[r:r + 1, :] * LOG2E

    qs = [(q_ref[:, h * dh:(h + 1) * dh].astype(F32) * (dh ** -0.5 * LOG2E)).astype(BF16) for h in range(hpb)]
    acc_ref[...] = jnp.zeros_like(acc_ref)

    def scores(h, j):
        start = pl.multiple_of(j * tq, tq)
        return _dot_nt(qs[h], k_ref[pl.ds(start, tq), h * dh:(h + 1) * dh]) - fr_ref[j, hp * hpb + h]

    def step(j, ms, slot, last):
        start = pl.multiple_of(j * tq, tq)
        out = []
        for h in range(hpb):
            s = s_ref[slot, h]
            if last:
                s = jnp.where(_tri(tq), s, -jnp.inf)
            else:
                s_ref[1 - slot, h] = scores(h, j + 1)
            m_new = jnp.maximum(ms[h], jnp.max(s, axis=-1, keepdims=True))
            p = jnp.exp2(s - m_new).astype(BF16)
            acc_ref[h] = jnp.exp2(ms[h] - m_new) * acc_ref[h] + _dot(p, va_ref[h, pl.ds(start, tq), :])
            out.append(m_new)
        return tuple(out)

    for h in range(hpb):
        s_ref[0, h] = scores(h, 0)
    m0 = tuple(jnp.full((tq, 1), -jnp.inf, F32) for _ in range(hpb))
    def pair(t, ms):
        return step(2 * t + 1, step(2 * t, ms, 0, False), 1, False)

    ms = lax.fori_loop(0, qi // 2, pair, m0)

    def finish(ms, slot):
        step(qi, ms, slot, True)
        for h in range(hpb):
            acc = acc_ref[h]
            gate = _sigmoid(og_ref[:, h * dh:(h + 1) * dh].astype(F32))
            o_ref[:, h * dh:(h + 1) * dh] = (acc[:, :dh] / acc[:, dh:] * gate).astype(BF16)

    @pl.when(qi % 2 == 0)
    def _():
        finish(ms, 0)

    @pl.when(qi % 2 == 1)
    def _():
        finish(step(qi - 1, ms, 0, False), 1)


def _fox(proj, small_t, b_f, bsz, seq, dh, q_blk, k_blk, v_blk, og_blk, f_blk):
    n = proj.shape[0]
    tq = min(FOX_TQ, seq)
    nq = seq // tq
    hpb = FOX_HEADS_PER_STEP
    w = hpb * dh
    assert FOX_HEADS % hpb == 0 and q_blk % hpb == 0 and k_blk % hpb == 0 and v_blk % hpb == 0 and og_blk % hpb == 0
    return pl.pallas_call(
        functools.partial(_fox_body, tq=tq, nq=nq, dh=dh, hpb=hpb),
        grid=(bsz, FOX_HEADS // hpb, nq),
        in_specs=[
            pl.BlockSpec((tq, w), lambda b, h, i: (b * nq + i, q_blk // hpb + h)),
            pl.BlockSpec((seq, w), lambda b, h, i: (b, k_blk // hpb + h)),
            pl.BlockSpec((seq, w), lambda b, h, i: (b, v_blk // hpb + h)),
            pl.BlockSpec((tq, w), lambda b, h, i: (b * nq + i, og_blk // hpb + h)),
            pl.BlockSpec((FOX_HEADS, seq), lambda b, h, i: (f_blk, b)),
            pl.BlockSpec((FOX_HEADS, 1), lambda b, h, i: (0, 0)),
        ],
        out_specs=pl.BlockSpec((tq, w), lambda b, h, i: (b * nq + i, h)),
        out_shape=jax.ShapeDtypeStruct((n, FOX_HEADS * dh), BF16),
        scratch_shapes=[pltpu.VMEM((hpb, seq, dh + LANES), BF16), pltpu.VMEM((hpb, tq, dh + LANES), F32),
                        pltpu.VMEM((2, hpb, tq, tq), F32), pltpu.VMEM((nq, FOX_HEADS, 1, tq), F32)],
        compiler_params=_params("parallel", "arbitrary", "arbitrary"),
        name="fox",
    )(proj, proj, proj, proj, small_t, b_f)


def _merge_body(x_ref, y0_ref, y1_ref, y2_ref, gt_ref, bg_ref, wb_ref, wo_ref, o_ref, *, d):
    merged = None
    for n, y_ref in enumerate((y0_ref, y1_ref, y2_ref)):
        cs = slice(n * d, (n + 1) * d)
        gate = _sigmoid(gt_ref[:, cs].astype(F32) + bg_ref[:, cs])
        term = _dot(y_ref[...], wb_ref[n]) * gate
        merged = term if merged is None else merged + term
    o_ref[...] = x_ref[...] + _dot(merged.astype(BF16), wo_ref[...])


def _merge(x2, y_gla, y_mlstm, y_fox, proj, b_gate, w_branch, w_out, gate_blk):
    n, d = x2.shape
    tm = min(MERGE_TM, n)
    rowblk = lambda shape: pl.BlockSpec(shape, lambda i: (i, 0))
    return pl.pallas_call(
        functools.partial(_merge_body, d=d),
        grid=(n // tm,),
        in_specs=[
            rowblk((tm, d)), rowblk((tm, d)), rowblk((tm, d)), rowblk((tm, d)),
            pl.BlockSpec((tm, N_BRANCH * d), lambda i: (i, gate_blk)),
            pl.BlockSpec((1, N_BRANCH * d), lambda i: (0, 0)),
            pl.BlockSpec((N_BRANCH, d, d), lambda i: (0, 0, 0)),
            pl.BlockSpec((d, d), lambda i: (0, 0)),
        ],
        out_specs=rowblk((tm, d)),
        out_shape=jax.ShapeDtypeStruct((n, d), F32),
        compiler_params=_params("parallel"),
        name="merge",
    )(x2, y_gla, y_mlstm, y_fox, proj, b_gate, w_branch, w_out)


def _ffn_body(x_ref, xp_ref, g_ref, wu_ref, cw_ref, cb_ref, wd_ref, gf_ref, o_ref, h_ref,
              *, tm, halo, blocks_per_seq, dff, chunk, final):
    i = pl.program_id(0)

    def norm(x, gain):
        return x * lax.rsqrt(jnp.mean(x * x, axis=-1, keepdims=True) + EPS) * gain

    x = x_ref[...]
    hp = norm(xp_ref[...], g_ref[...]).astype(BF16)
    h_ref[0:halo, :] = jnp.where(i % blocks_per_seq != 0, hp, jnp.zeros_like(hp))
    h_ref[halo:halo + tm, :] = norm(x, g_ref[...]).astype(BF16)
    h = h_ref[...]

    def conv(lo, hi):
        u = _dot(h, wu_ref[:, lo:hi])
        y = cb_ref[:, lo:hi]
        for j in range(FFN_CONV - 1):
            y = y + cw_ref[j:j + 1, lo:hi] * pltpu.roll(u, FFN_CONV - 1 - j, 0)[halo:, :]
        return y + cw_ref[FFN_CONV - 1:FFN_CONV, lo:hi] * u[halo:, :]

    acc = x
    for lo in range(0, dff, chunk):
        hi = min(lo + chunk, dff)
        act = _silu(conv(dff + lo, dff + hi)) * conv(lo, hi)
        acc = acc + _dot(act.astype(BF16), wd_ref[lo:hi, :])
    o_ref[...] = norm(acc, gf_ref[...]) if final else acc


def _ffn(x2, g, w_up, conv_w, conv_b, w_down, seq, final_gain=None):
    n, d = x2.shape
    dff = w_down.shape[0]
    tm = min(FFN_TM, seq)
    halo = BF16_SUBLANES
    hb = tm // halo
    final = final_gain is not None
    resident = lambda shape: pl.BlockSpec(shape, lambda i: (0,) * len(shape), pipeline_mode=pl.Buffered(1))
    return pl.pallas_call(
        functools.partial(_ffn_body, tm=tm, halo=halo, blocks_per_seq=seq // tm, dff=dff, chunk=FFN_CHUNK,
                          final=final),
        grid=(n // tm,),
        in_specs=[
            pl.BlockSpec((tm, d), lambda i: (i, 0)),
            pl.BlockSpec((halo, d), lambda i: (jnp.maximum(i * hb - 1, 0), 0)),
            resident((1, d)),
            resident((d, 2 * dff)),
            resident((FFN_CONV, 2 * dff)),
            resident((1, 2 * dff)),
            resident((dff, d)),
            resident((1, d)),
        ],
        out_specs=pl.BlockSpec((tm, d), lambda i: (i, 0)),
        out_shape=jax.ShapeDtypeStruct((n, d), F32),
        scratch_shapes=[pltpu.VMEM((halo + tm, d), BF16)],
        compiler_params=_params("parallel"),
        name="ffn_final" if final else "ffn",
    )(x2, x2, g, w_up, conv_w, conv_b, w_down, final_gain if final else g)


def _block_diag_heads(w, n_heads):
    nblk, bc, bd = w.shape
    per = nblk // n_heads
    tiled = jnp.tile(w.reshape(n_heads, per * bc, bd), (1, 1, per))
    same_block = (jnp.arange(per * bc)[:, None] // bc) == (jnp.arange(per * bd)[None, :] // bd)
    return jnp.where(same_block, tiled, 0.0)


def _pad_lanes(a, width):
    return jnp.pad(a, ((0, 0),) * (a.ndim - 1) + ((0, width - a.shape[-1]),))


def kernel(x, norm_mix, w_in, b_gate, gla_w_lr, gla_b_lr, gla_norm, mlstm_conv_w, mlstm_conv_b, mlstm_wq, mlstm_wk,
           mlstm_wv, mlstm_w_i, mlstm_b_i, mlstm_w_f, mlstm_b_f, mlstm_skip, mlstm_norm, fox_b_f, w_branch, w_out,
           norm_ffn, ffn_w_up, ffn_conv_w, ffn_conv_b, ffn_w_down, norm_final):
    bsz, seq, d = x.shape
    depth = w_in.shape[0]
    hdk = gla_w_lr.shape[2]
    hdv = gla_norm.shape[1]
    inner = mlstm_conv_w.shape[2]
    fox_w = w_branch.shape[2]
    fox_dh = fox_w // FOX_HEADS
    dff = ffn_w_down.shape[1]
    assert hdv == d and inner == d and fox_w == d and 2 * hdk == d

    o_glr = 2 * hdk + hdv
    o_gr = o_glr + GLA_RANK
    o_ff = o_gr + hdv + 2 * inner + 3 * fox_w
    o_fog = o_ff + FOX_HEADS
    blk = {"gq": 0, "gk": 1, "gv": 1, "gr": 2, "mx": 3, "fq": 5 * d // fox_dh, "fk": 6 * d // fox_dh,
           "fv": 7 * d // fox_dh, "fog": 8 * d // fox_dh, "gates": 3}

    x2 = x.reshape(bsz * seq, d)
    for l in range(depth):
        wl = w_in[l]
        w_wide = [wl[:, :o_glr].astype(BF16), wl[:, o_gr:o_ff].astype(BF16), wl[:, o_fog:].astype(BF16)]
        w_small = _pad_lanes(jnp.concatenate([wl[:, o_glr:o_gr], wl[:, o_ff:o_fog]], axis=1), SMALL_W).astype(BF16)
        proj, small, small_t = _inproj(x2, norm_mix[l][None, :], w_wide, w_small)

        w_lr_pad = jnp.pad(gla_w_lr[l], ((0, SMALL_W - GLA_RANK), (0, 0)))
        y_gla = _gla(proj, small, w_lr_pad, gla_b_lr[l][None, :], gla_norm[l][None, :], bsz, seq)

        w_if = _pad_lanes(jnp.concatenate([mlstm_w_i[l], mlstm_w_f[l]], axis=1), LANES)
        b_if = _pad_lanes(jnp.concatenate([mlstm_b_i[l], mlstm_b_f[l]])[None, :], LANES)
        w_if3 = w_if.reshape(3, inner, LANES).astype(BF16)
        y_mlstm = _mlstm(
            proj, mlstm_conv_w[l], mlstm_conv_b[l][None, :],
            _block_diag_heads(mlstm_wq[l], MLSTM_HEADS).astype(BF16),
            _block_diag_heads(mlstm_wk[l], MLSTM_HEADS).astype(BF16),
            _block_diag_heads(mlstm_wv[l], MLSTM_HEADS).astype(BF16),
            w_if3, b_if, mlstm_skip[l][None, :], mlstm_norm[l][None, :], bsz, seq, blk["mx"])

        y_fox = _fox(proj, small_t, fox_b_f[l][:, None], bsz, seq, fox_dh,
                     blk["fq"], blk["fk"], blk["fv"], blk["fog"], GLA_RANK // FOX_HEADS)

        x2 = _merge(x2, y_gla, y_mlstm, y_fox, proj, b_gate[l].reshape(1, N_BRANCH * d),
                    w_branch[l].astype(BF16), w_out[l].astype(BF16), blk["gates"])

        x2 = _ffn(x2, norm_ffn[l][None, :], ffn_w_up[l].astype(BF16), ffn_conv_w[l], ffn_conv_b[l][None, :],
                  ffn_w_down[l].astype(BF16), seq, final_gain=norm_final[None, :] if l == depth - 1 else None)
    return x2.reshape(bsz, seq, d)
```

```python
import functools

import jax
import jax.numpy as jnp
from jax import lax
from jax.experimental import pallas as pl
from jax.experimental.pallas import tpu as pltpu

F32 = jnp.float32
BF16 = jnp.bfloat16
HIGHEST = lax.Precision.HIGHEST
EPS = 1e-6
LOG2E = 1.4426950408889634

LANES = 128
F32_SUBLANES = 8
BF16_SUBLANES = 16
VMEM_LIMIT_BYTES = 56 * 1024 * 1024

GLA_HEADS = 4
GLA_RANK = 16
GLA_TAU = 16.0
GLA_CHUNK = 64
MLSTM_HEADS = 4
MLSTM_CONV = 4
MLSTM_BLOCK = 4
FOX_HEADS = 8
N_BRANCH = 3
FFN_CONV = 3
SMALL_W = LANES

INPROJ_TM = 1024
INPROJ_TN = 2048
GLA_ROWS = 256
MLSTM_CHUNK = 256
MLSTM_ROWS = 256
FOX_TQ = 512
FOX_HEADS_PER_STEP = 2
FOX_UNROLL = 4
MERGE_TM = 512
FFN_TM = 512
FFN_CHUNK = 1024


def _params(*sem):
    return pltpu.CompilerParams(dimension_semantics=sem, vmem_limit_bytes=VMEM_LIMIT_BYTES)


def _log_sigmoid(z):
    return jnp.minimum(z, 0.0) - jnp.log1p(jnp.exp(-jnp.abs(z)))


def _sigmoid(z):
    return 1.0 / (1.0 + jnp.exp(-z))


def _silu(z):
    return z * _sigmoid(z)


def _tri(n, upper=False):
    r = lax.broadcasted_iota(jnp.int32, (n, n), 0)
    c = lax.broadcasted_iota(jnp.int32, (n, n), 1)
    return (r <= c) if upper else (r >= c)


def _dot(a, b, **kw):
    return jnp.dot(a, b, preferred_element_type=F32, **kw)


def _dot_nt(a, b):
    return lax.dot_general(a, b, (((1,), (1,)), ((), ())), preferred_element_type=F32)


def _dot_tn(a, b):
    return lax.dot_general(a, b, (((0,), (0,)), ((), ())), preferred_element_type=F32)


def _inproj_body(x_ref, g_ref, w_ref, ws_ref, o_ref, os_ref, ot_ref, h_ref):
    @pl.when(pl.program_id(1) == 0)
    def _():
        x = x_ref[...]
        ms = jnp.mean(x * x, axis=-1, keepdims=True)
        h = (x * lax.rsqrt(ms + EPS) * g_ref[...]).astype(BF16)
        h_ref[...] = h
        small = _dot(h, ws_ref[...])
        os_ref[...] = small
        ot_ref[...] = small.T

    o_ref[...] = _dot(h_ref[...], w_ref[...]).astype(BF16)


def _inproj(x2, g, w_big, w_small):
    n, d = x2.shape
    c = w_big.shape[1]
    tm, tn = min(INPROJ_TM, n), min(INPROJ_TN, c)
    return pl.pallas_call(
        _inproj_body,
        grid=(n // tm, c // tn),
        in_specs=[
            pl.BlockSpec((tm, d), lambda i, j: (i, 0)),
            pl.BlockSpec((1, d), lambda i, j: (0, 0)),
            pl.BlockSpec((d, tn), lambda i, j: (0, j)),
            pl.BlockSpec((d, SMALL_W), lambda i, j: (0, 0)),
        ],
        out_specs=[
            pl.BlockSpec((tm, tn), lambda i, j: (i, j)),
            pl.BlockSpec((tm, SMALL_W), lambda i, j: (i, 0)),
            pl.BlockSpec((SMALL_W, tm), lambda i, j: (0, i)),
        ],
        out_shape=[jax.ShapeDtypeStruct((n, c), BF16), jax.ShapeDtypeStruct((n, SMALL_W), F32),
                   jax.ShapeDtypeStruct((SMALL_W, n), F32)],
        scratch_shapes=[pltpu.VMEM((tm, d), BF16)],
        compiler_params=_params("parallel", "arbitrary"),
        name="inproj",
    )(x2, g, w_big, w_small)


def _gla_body(q_ref, k_ref, v_ref, r_ref, s_ref, wlr_ref, blr_ref, gn_ref, o_ref, st_ref, *, chunk, nchunk, dk, dv):
    @pl.when(pl.program_id(1) == 0)
    def _():
        st_ref[...] = jnp.zeros_like(st_ref)

    z = _dot(s_ref[...], wlr_ref[...], precision=HIGHEST) + blr_ref[...]
    log_a = _log_sigmoid(z) * (1.0 / GLA_TAU)
    tri = _tri(chunk).astype(F32)
    causal = _tri(chunk)
    scale = dk ** -0.5
    gn = gn_ref[...]
    for c in range(nchunk):
        rows = slice(c * chunk, (c + 1) * chunk)
        bc = _dot(tri, log_a[rows], precision=HIGHEST)
        b_last = bc[chunk - 1:chunk, :]
        q = q_ref[rows, :].astype(F32) * scale
        k = k_ref[rows, :].astype(F32)
        q_in = (q * jnp.exp(bc)).astype(BF16)
        k_in = (k * jnp.exp(-bc)).astype(BF16)
        k_st = (k * jnp.exp(b_last - bc)).astype(BF16)
        decay = jnp.exp(b_last)
        for h in range(GLA_HEADS):
            ks = slice(h * dk, (h + 1) * dk)
            vs = slice(h * dv, (h + 1) * dv)
            qh = q_in[:, ks]
            vh = v_ref[rows, vs]
            att = jnp.where(causal, _dot_nt(qh, k_in[:, ks]), 0.0)
            st = st_ref[h]
            o = _dot(att.astype(BF16), vh) + _dot_nt(qh, st.astype(BF16))
            st_ref[h] = st * decay[:, ks] + _dot_tn(vh, k_st[:, ks])
            on = o * lax.rsqrt(jnp.mean(o * o, axis=-1, keepdims=True) + EPS) * gn[:, vs]
            o_ref[rows, vs] = (on * _silu(r_ref[rows, vs].astype(F32))).astype(BF16)


def _gla(proj, small, w_lr_pad, b_lr, g_norm, bsz, seq):
    n = proj.shape[0]
    hdk = w_lr_pad.shape[1]
    dk = hdk // GLA_HEADS
    hdv = g_norm.shape[1]
    dv = hdv // GLA_HEADS
    rows = min(GLA_ROWS, seq)
    nt = seq // rows
    assert hdv == 2 * hdk
    row = lambda b, t: b * nt + t
    return pl.pallas_call(
        functools.partial(_gla_body, chunk=GLA_CHUNK, nchunk=rows // GLA_CHUNK, dk=dk, dv=dv),
        grid=(bsz, nt),
        in_specs=[
            pl.BlockSpec((rows, hdk), lambda b, t: (row(b, t), 0)),
            pl.BlockSpec((rows, hdk), lambda b, t: (row(b, t), 1)),
            pl.BlockSpec((rows, hdv), lambda b, t: (row(b, t), 1)),
            pl.BlockSpec((rows, hdv), lambda b, t: (row(b, t), 2)),
            pl.BlockSpec((rows, SMALL_W), lambda b, t: (row(b, t), 0)),
            pl.BlockSpec((SMALL_W, hdk), lambda b, t: (0, 0)),
            pl.BlockSpec((1, hdk), lambda b, t: (0, 0)),
            pl.BlockSpec((1, hdv), lambda b, t: (0, 0)),
        ],
        out_specs=pl.BlockSpec((rows, hdv), lambda b, t: (row(b, t), 0)),
        out_shape=jax.ShapeDtypeStruct((n, hdv), BF16),
        scratch_shapes=[pltpu.VMEM((GLA_HEADS, dv, dk), F32)],
        compiler_params=_params("parallel", "arbitrary"),
        name="gla",
    )(proj, proj, proj, proj, small, w_lr_pad, b_lr, g_norm)


def _mlstm_body(xm_ref, z_ref, cw_ref, cb_ref, wq_ref, wk_ref, wv_ref, wif_ref, bif_ref,
                skip_ref, gn_ref, o_ref, xf_ref, q_sc, k_sc, v_sc, h_sc, xc_sc, c_sc, m_sc, *, chunk, nchunk, dh):
    rows_blk = chunk * nchunk
    halo = F32_SUBLANES

    @pl.when(pl.program_id(1) == 0)
    def _():
        xf_ref[0:halo, :] = jnp.zeros((halo, xf_ref.shape[1]), F32)
        c_sc[...] = jnp.zeros_like(c_sc)
        m_sc[...] = jnp.zeros_like(m_sc)

    xf_ref[halo:halo + rows_blk, :] = xm_ref[...].astype(F32)
    for h in range(MLSTM_HEADS):
        hs = slice(h * dh, (h + 1) * dh)
        xf = xf_ref[:, hs]
        conv = cb_ref[:, hs]
        for j in range(MLSTM_CONV - 1):
            conv = conv + cw_ref[j:j + 1, hs] * pltpu.roll(xf, MLSTM_CONV - 1 - j, 0)[halo:, :]
        conv = conv + cw_ref[MLSTM_CONV - 1:MLSTM_CONV, hs] * xf[halo:, :]
        xc = _silu(conv)
        xc_sc[:, hs] = xc
        xcb = xc.astype(BF16)
        q_sc[:, hs] = _dot(xcb, wq_ref[h]).astype(BF16)
        k_sc[:, hs] = _dot(xcb, wk_ref[h]).astype(BF16)
        v_sc[:, hs] = _dot(xm_ref[:, hs], wv_ref[h]).astype(BF16)
    xf_ref[0:halo, :] = xf_ref[rows_blk:rows_blk + halo, :]
    qa, ka, va = q_sc[...], k_sc[...], v_sc[...]

    gcol = _dot(qa, wif_ref[0]) + _dot(ka, wif_ref[1]) + _dot(va, wif_ref[2]) + bif_ref[...]
    lane = lax.broadcasted_iota(jnp.int32, gcol.shape, 1)
    gcol = jnp.where(lane < MLSTM_HEADS, gcol, _log_sigmoid(gcol))

    tri = _tri(chunk).astype(F32)
    causal = _tri(chunk)
    qscale = dh ** -0.5
    ones_aug = jnp.ones((chunk, LANES), BF16)
    for c in range(nchunk):
        rows = slice(c * chunk, (c + 1) * chunk)
        gc = gcol[rows]
        cum_c = _dot(tri, gc, precision=HIGHEST)
        gr = gc.T
        cum_r = cum_c.T
        for h in range(MLSTM_HEADS):
            hs = slice(h * dh, (h + 1) * dh)
            fh = MLSTM_HEADS + h
            i_col, b_col = gc[:, h:h + 1], cum_c[:, fh:fh + 1]
            i_row, b_row = gr[h:h + 1, :], cum_r[fh:fh + 1, :]
            m_st = m_sc[h][:, 0:1]
            d_log = jnp.where(causal, b_col - b_row + i_row, -jnp.inf)
            m_inter = b_col + m_st
            m_t = jnp.maximum(m_inter, jnp.max(d_log, axis=-1, keepdims=True))
            w_intra = jnp.exp(d_log - m_t)
            w_inter = jnp.exp(m_inter - m_t)
            qc = (q_sc[rows, hs].astype(F32) * qscale).astype(BF16)
            kc = k_sc[rows, hs]
            vc = v_sc[rows, hs]
            s = _dot_nt(qc, kc) * w_intra
            c_aug = c_sc[h]
            q_state = _dot(qc, c_aug.astype(BF16))
            num = _dot(s.astype(BF16), vc) + w_inter * q_state[:, :dh]
            qn = jnp.sum(s, axis=-1, keepdims=True) + w_inter * q_state[:, dh:dh + 1]
            h_sc[rows, hs] = num / jnp.maximum(jnp.abs(qn), jnp.exp(-m_t))
            g = b_row[:, chunk - 1:chunk]
            m_new = jnp.maximum(g + m_st, jnp.max(g - b_row + i_row, axis=-1, keepdims=True))
            wa = jnp.exp(g - b_col + i_col - m_new)
            dec = jnp.exp(g + m_st - m_new)
            kw = (kc.astype(F32) * wa).astype(BF16)
            v_aug = jnp.concatenate([vc, ones_aug], axis=1)
            c_sc[h] = dec * c_aug + _dot_tn(kw, v_aug)
            m_sc[h] = jnp.broadcast_to(m_new, (1, LANES))

    for h in range(MLSTM_HEADS):
        hs = slice(h * dh, (h + 1) * dh)
        hh = h_sc[:, hs]
        hn = hh * lax.rsqrt(jnp.mean(hh * hh, axis=-1, keepdims=True) + EPS) * gn_ref[:, hs]
        zg = _silu(z_ref[:, hs].astype(F32))
        o_ref[:, hs] = ((hn + skip_ref[:, hs] * xc_sc[:, hs]) * zg).astype(BF16)


def _mlstm(proj, conv_w, conv_b, wq_bd, wk_bd, wv_bd, w_if, b_if, skip, g_norm, bsz, seq, col_blk):
    n = proj.shape[0]
    inner = conv_w.shape[1]
    dh = inner // MLSTM_HEADS
    rows = min(MLSTM_ROWS, seq)
    chunk = min(MLSTM_CHUNK, rows)
    nt = seq // rows
    row = lambda b, t: b * nt + t
    full = lambda shape: pl.BlockSpec(shape, lambda b, t: (0,) * len(shape))
    return pl.pallas_call(
        functools.partial(_mlstm_body, chunk=chunk, nchunk=rows // chunk, dh=dh),
        grid=(bsz, nt),
        in_specs=[
            pl.BlockSpec((rows, inner), lambda b, t: (row(b, t), col_blk)),
            pl.BlockSpec((rows, inner), lambda b, t: (row(b, t), col_blk + 1)),
            full((MLSTM_CONV, inner)),
            full((1, inner)),
            full((MLSTM_HEADS, dh, dh)),
            full((MLSTM_HEADS, dh, dh)),
            full((MLSTM_HEADS, dh, dh)),
            full((3, inner, LANES)),
            full((1, LANES)),
            full((1, inner)),
            full((1, inner)),
        ],
        out_specs=pl.BlockSpec((rows, inner), lambda b, t: (row(b, t), 0)),
        out_shape=jax.ShapeDtypeStruct((n, inner), BF16),
        scratch_shapes=[
            pltpu.VMEM((rows + F32_SUBLANES, inner), F32),
            pltpu.VMEM((rows, inner), BF16),
            pltpu.VMEM((rows, inner), BF16),
            pltpu.VMEM((rows, inner), BF16),
            pltpu.VMEM((rows, inner), F32),
            pltpu.VMEM((rows, inner), F32),
            pltpu.VMEM((MLSTM_HEADS, dh, dh + LANES), F32),
            pltpu.VMEM((MLSTM_HEADS, 1, LANES), F32),
        ],
        compiler_params=_params("parallel", "arbitrary"),
        name="mlstm",
    )(proj, proj, conv_w, conv_b, wq_bd, wk_bd, wv_bd, w_if, b_if, skip, g_norm)


def _fox_body(q_ref, k_ref, v_ref, og_ref, ft_ref, bf_ref, o_ref, va_ref, acc_ref, s_ref, fr_ref, *, tq, nq, dh, hpb):
    hp = pl.program_id(1)
    qi = pl.program_id(2)

    @pl.when(qi == 0)
    def _():
        for h in range(hpb):
            va_ref[h, :, :dh] = v_ref[:, h * dh:(h + 1) * dh]
            va_ref[h, :, dh:] = jnp.ones((va_ref.shape[1], LANES), BF16)

    @pl.when(jnp.logical_and(qi == 0, hp == 0))
    def _():
        log_f = _log_sigmoid(ft_ref[...] + bf_ref[...])
        tri_u = _tri(tq, upper=True).astype(F32)
        carry = jnp.zeros((FOX_HEADS, 1), F32)
        for j in range(nq):
            cum = _dot(log_f[:, j * tq:(j + 1) * tq], tri_u, precision=HIGHEST) + carry
            carry = cum[:, tq - 1:tq]
            for r in range(FOX_HEADS):
                fr_ref[j, r] = cum[r:r + 1, :] * LOG2E

    qs = [(q_ref[:, h * dh:(h + 1) * dh].astype(F32) * (dh ** -0.5 * LOG2E)).astype(BF16) for h in range(hpb)]
    acc_ref[...] = jnp.zeros_like(acc_ref)

    def scores(h, j):
        start = pl.multiple_of(j * tq, tq)
        return _dot_nt(qs[h], k_ref[pl.ds(start, tq), h * dh:(h + 1) * dh]) - fr_ref[j, hp * hpb + h]

    def step(j, ms, slot, last):
        start = pl.multiple_of(j * tq, tq)
        out = []
        for h in range(hpb):
            s = s_ref[slot, h]
            if last:
                s = jnp.where(_tri(tq), s, -jnp.inf)
            else:
                s_ref[1 - slot, h] = scores(h, j + 1)
            m_new = jnp.maximum(ms[h], jnp.max(s, axis=-1, keepdims=True))
            p = jnp.exp2(s - m_new).astype(BF16)
            acc_ref[h] = jnp.exp2(ms[h] - m_new) * acc_ref[h] + _dot(p, va_ref[h, pl.ds(start, tq), :])
            out.append(m_new)
        return tuple(out)

    for h in range(hpb):
        s_ref[0, h] = scores(h, 0)
    m0 = tuple(jnp.full((tq, 1), -jnp.inf, F32) for _ in range(hpb))

    def steps(first, count, ms):
        for u in range(count):
            ms = step(first + u, ms, u % 2, False)
        return ms

    unroll = FOX_UNROLL
    ms = lax.fori_loop(0, qi // unroll, lambda t, ms: steps(unroll * t, unroll, ms), m0)

    for rem in range(unroll):
        @pl.when(qi % unroll == rem)
        def _():
            step(qi, steps(qi - rem, rem, ms), rem % 2, True)
            for h in range(hpb):
                acc = acc_ref[h]
                gate = _sigmoid(og_ref[:, h * dh:(h + 1) * dh].astype(F32))
                o_ref[:, h * dh:(h + 1) * dh] = (acc[:, :dh] / acc[:, dh:] * gate).astype(BF16)


def _fox(proj, small_t, b_f, bsz, seq, dh, q_blk, k_blk, v_blk, og_blk, f_blk):
    n = proj.shape[0]
    tq = min(FOX_TQ, seq)
    nq = seq // tq
    hpb = FOX_HEADS_PER_STEP
    w = hpb * dh
    assert FOX_HEADS % hpb == 0 and q_blk % hpb == 0 and k_blk % hpb == 0 and v_blk % hpb == 0 and og_blk % hpb == 0
    return pl.pallas_call(
        functools.partial(_fox_body, tq=tq, nq=nq, dh=dh, hpb=hpb),
        grid=(bsz, FOX_HEADS // hpb, nq),
        in_specs=[
            pl.BlockSpec((tq, w), lambda b, h, i: (b * nq + i, q_blk // hpb + h)),
            pl.BlockSpec((seq, w), lambda b, h, i: (b, k_blk // hpb + h)),
            pl.BlockSpec((seq, w), lambda b, h, i: (b, v_blk // hpb + h)),
            pl.BlockSpec((tq, w), lambda b, h, i: (b * nq + i, og_blk // hpb + h)),
            pl.BlockSpec((FOX_HEADS, seq), lambda b, h, i: (f_blk, b)),
            pl.BlockSpec((FOX_HEADS, 1), lambda b, h, i: (0, 0)),
        ],
        out_specs=pl.BlockSpec((tq, w), lambda b, h, i: (b * nq + i, h)),
        out_shape=jax.ShapeDtypeStruct((n, FOX_HEADS * dh), BF16),
        scratch_shapes=[pltpu.VMEM((hpb, seq, dh + LANES), BF16), pltpu.VMEM((hpb, tq, dh + LANES), F32),
                        pltpu.VMEM((2, hpb, tq, tq), F32), pltpu.VMEM((nq, FOX_HEADS, 1, tq), F32)],
        compiler_params=_params("parallel", "arbitrary", "arbitrary"),
        name="fox",
    )(proj, proj, proj, proj, small_t, b_f)


def _merge_body(x_ref, y0_ref, y1_ref, y2_ref, gt_ref, bg_ref, wb_ref, wo_ref, o_ref, *, d):
    merged = None
    for n, y_ref in enumerate((y0_ref, y1_ref, y2_ref)):
        cs = slice(n * d, (n + 1) * d)
        gate = _sigmoid(gt_ref[:, cs].astype(F32) + bg_ref[:, cs])
        term = _dot(y_ref[...], wb_ref[n]) * gate
        merged = term if merged is None else merged + term
    o_ref[...] = x_ref[...] + _dot(merged.astype(BF16), wo_ref[...])


def _merge(x2, y_gla, y_mlstm, y_fox, proj, b_gate, w_branch, w_out, gate_blk):
    n, d = x2.shape
    tm = min(MERGE_TM, n)
    rowblk = lambda shape: pl.BlockSpec(shape, lambda i: (i, 0))
    return pl.pallas_call(
        functools.partial(_merge_body, d=d),
        grid=(n // tm,),
        in_specs=[
            rowblk((tm, d)), rowblk((tm, d)), rowblk((tm, d)), rowblk((tm, d)),
            pl.BlockSpec((tm, N_BRANCH * d), lambda i: (i, gate_blk)),
            pl.BlockSpec((1, N_BRANCH * d), lambda i: (0, 0)),
            pl.BlockSpec((N_BRANCH, d, d), lambda i: (0, 0, 0)),
            pl.BlockSpec((d, d), lambda i: (0, 0)),
        ],
        out_specs=rowblk((tm, d)),
        out_shape=jax.ShapeDtypeStruct((n, d), F32),
        compiler_params=_params("parallel"),
        name="merge",
    )(x2, y_gla, y_mlstm, y_fox, proj, b_gate, w_branch, w_out)


def _ffn_body(x_ref, xp_ref, g_ref, wu_ref, cw_ref, cb_ref, wd_ref, gf_ref, o_ref, h_ref,
              *, tm, halo, blocks_per_seq, dff, chunk, final):
    i = pl.program_id(0)

    def norm(x, gain):
        return x * lax.rsqrt(jnp.mean(x * x, axis=-1, keepdims=True) + EPS) * gain

    x = x_ref[...]
    hp = norm(xp_ref[...], g_ref[...]).astype(BF16)
    h_ref[0:halo, :] = jnp.where(i % blocks_per_seq != 0, hp, jnp.zeros_like(hp))
    h_ref[halo:halo + tm, :] = norm(x, g_ref[...]).astype(BF16)
    h = h_ref[...]

    def conv(lo, hi):
        u = _dot(h, wu_ref[:, lo:hi])
        y = cb_ref[:, lo:hi]
        for j in range(FFN_CONV - 1):
            y = y + cw_ref[j:j + 1, lo:hi] * pltpu.roll(u, FFN_CONV - 1 - j, 0)[halo:, :]
        return y + cw_ref[FFN_CONV - 1:FFN_CONV, lo:hi] * u[halo:, :]

    acc = x
    for lo in range(0, dff, chunk):
        hi = min(lo + chunk, dff)
        act = _silu(conv(dff + lo, dff + hi)) * conv(lo, hi)
        acc = acc + _dot(act.astype(BF16), wd_ref[lo:hi, :])
    o_ref[...] = norm(acc, gf_ref[...]) if final else acc


def _ffn(x2, g, w_up, conv_w, conv_b, w_down, seq, final_gain=None):
    n, d = x2.shape
    dff = w_down.shape[0]
    tm = min(FFN_TM, seq)
    halo = BF16_SUBLANES
    hb = tm // halo
    final = final_gain is not None
    resident = lambda shape: pl.BlockSpec(shape, lambda i: (0,) * len(shape), pipeline_mode=pl.Buffered(1))
    return pl.pallas_call(
        functools.partial(_ffn_body, tm=tm, halo=halo, blocks_per_seq=seq // tm, dff=dff, chunk=FFN_CHUNK,
                          final=final),
        grid=(n // tm,),
        in_specs=[
            pl.BlockSpec((tm, d), lambda i: (i, 0)),
            pl.BlockSpec((halo, d), lambda i: (jnp.maximum(i * hb - 1, 0), 0)),
            resident((1, d)),
            resident((d, 2 * dff)),
            resident((FFN_CONV, 2 * dff)),
            resident((1, 2 * dff)),
            resident((dff, d)),
            resident((1, d)),
        ],
        out_specs=pl.BlockSpec((tm, d), lambda i: (i, 0)),
        out_shape=jax.ShapeDtypeStruct((n, d), F32),
        scratch_shapes=[pltpu.VMEM((halo + tm, d), BF16)],
        compiler_params=_params("parallel"),
        name="ffn_final" if final else "ffn",
    )(x2, x2, g, w_up, conv_w, conv_b, w_down, final_gain if final else g)


def _block_diag_heads(w, n_heads):
    nblk, bc, bd = w.shape
    per = nblk // n_heads
    tiled = jnp.tile(w.reshape(n_heads, per * bc, bd), (1, 1, per))
    same_block = (jnp.arange(per * bc)[:, None] // bc) == (jnp.arange(per * bd)[None, :] // bd)
    return jnp.where(same_block, tiled, 0.0)


def _pad_lanes(a, width):
    return jnp.pad(a, ((0, 0),) * (a.ndim - 1) + ((0, width - a.shape[-1]),))


def kernel(x, norm_mix, w_in, b_gate, gla_w_lr, gla_b_lr, gla_norm, mlstm_conv_w, mlstm_conv_b, mlstm_wq, mlstm_wk,
           mlstm_wv, mlstm_w_i, mlstm_b_i, mlstm_w_f, mlstm_b_f, mlstm_skip, mlstm_norm, fox_b_f, w_branch, w_out,
           norm_ffn, ffn_w_up, ffn_conv_w, ffn_conv_b, ffn_w_down, norm_final):
    bsz, seq, d = x.shape
    depth = w_in.shape[0]
    hdk = gla_w_lr.shape[2]
    hdv = gla_norm.shape[1]
    inner = mlstm_conv_w.shape[2]
    fox_w = w_branch.shape[2]
    fox_dh = fox_w // FOX_HEADS
    dff = ffn_w_down.shape[1]
    assert hdv == d and inner == d and fox_w == d and 2 * hdk == d

    o_glr = 2 * hdk + hdv
    o_gr = o_glr + GLA_RANK
    o_ff = o_gr + hdv + 2 * inner + 3 * fox_w
    o_fog = o_ff + FOX_HEADS
    blk = {"gq": 0, "gk": 1, "gv": 1, "gr": 2, "mx": 3, "fq": 5 * d // fox_dh, "fk": 6 * d // fox_dh,
           "fv": 7 * d // fox_dh, "fog": 8 * d // fox_dh, "gates": 3}

    x2 = x.reshape(bsz * seq, d)
    for l in range(depth):
        wl = w_in[l]
        w_big = jnp.concatenate([wl[:, :o_glr], wl[:, o_gr:o_ff], wl[:, o_fog:]], axis=1).astype(BF16)
        w_small = _pad_lanes(jnp.concatenate([wl[:, o_glr:o_gr], wl[:, o_ff:o_fog]], axis=1), SMALL_W).astype(BF16)
        proj, small, small_t = _inproj(x2, norm_mix[l][None, :], w_big, w_small)

        w_lr_pad = jnp.pad(gla_w_lr[l], ((0, SMALL_W - GLA_RANK), (0, 0)))
        y_gla = _gla(proj, small, w_lr_pad, gla_b_lr[l][None, :], gla_norm[l][None, :], bsz, seq)

        w_if = _pad_lanes(jnp.concatenate([mlstm_w_i[l], mlstm_w_f[l]], axis=1), LANES)
        b_if = _pad_lanes(jnp.concatenate([mlstm_b_i[l], mlstm_b_f[l]])[None, :], LANES)
        w_if3 = w_if.reshape(3, inner, LANES).astype(BF16)
        y_mlstm = _mlstm(
            proj, mlstm_conv_w[l], mlstm_conv_b[l][None, :],
            _block_diag_heads(mlstm_wq[l], MLSTM_HEADS).astype(BF16),
            _block_diag_heads(mlstm_wk[l], MLSTM_HEADS).astype(BF16),
            _block_diag_heads(mlstm_wv[l], MLSTM_HEADS).astype(BF16),
            w_if3, b_if, mlstm_skip[l][None, :], mlstm_norm[l][None, :], bsz, seq, blk["mx"])

        y_fox = _fox(proj, small_t, fox_b_f[l][:, None], bsz, seq, fox_dh,
                     blk["fq"], blk["fk"], blk["fv"], blk["fog"], GLA_RANK // FOX_HEADS)

        x2 = _merge(x2, y_gla, y_mlstm, y_fox, proj, b_gate[l].reshape(1, N_BRANCH * d),
                    w_branch[l].astype(BF16), w_out[l].astype(BF16), blk["gates"])

        x2 = _ffn(x2, norm_ffn[l][None, :], ffn_w_up[l].astype(BF16), ffn_conv_w[l], ffn_conv_b[l][None, :],
                  ffn_w_down[l].astype(BF16), seq, final_gain=norm_final[None, :] if l == depth - 1 else None)
    return x2.reshape(bsz, seq, d)
```

```python
import functools

import jax
import jax.numpy as jnp
from jax import lax
from jax.experimental import pallas as pl
from jax.experimental.pallas import tpu as pltpu

F32 = jnp.float32
BF16 = jnp.bfloat16
EPS = 1e-6
LOG2E = 1.4426950408889634

LANES = 128
F32_SUBLANES = 8
BF16_SUBLANES = 16
VMEM_LIMIT_BYTES = 56 * 1024 * 1024

GLA_HEADS = 4
GLA_RANK = 16
GLA_TAU = 16.0
GLA_CHUNK = 64
MLSTM_HEADS = 4
MLSTM_CONV = 4
MLSTM_BLOCK = 4
FOX_HEADS = 8
N_BRANCH = 3
FFN_CONV = 3
SMALL_W = LANES

INPROJ_TM = 1024
INPROJ_TN = 2048
GLA_ROWS = 256
MLSTM_CHUNK = 256
MLSTM_ROWS = 256
FOX_TQ = 512
FOX_HEADS_PER_STEP = 2
FOX_UNROLL = 4
MERGE_TM = 512
FFN_TM = 512
FFN_CHUNK = 1024


def _params(*sem):
    return pltpu.CompilerParams(dimension_semantics=sem, vmem_limit_bytes=VMEM_LIMIT_BYTES)


def _log_sigmoid(z, wide=False):
    e = jnp.exp(-jnp.abs(z))
    return jnp.minimum(z, 0.0) - (jnp.log(1.0 + e) if wide else jnp.log1p(e))


def _sigmoid(z):
    return 1.0 / (1.0 + jnp.exp(-z))


def _silu(z):
    return z * _sigmoid(z)


def _tri(n, upper=False):
    r = lax.broadcasted_iota(jnp.int32, (n, n), 0)
    c = lax.broadcasted_iota(jnp.int32, (n, n), 1)
    return (r <= c) if upper else (r >= c)


def _dot(a, b, **kw):
    return jnp.dot(a, b, preferred_element_type=F32, **kw)


def _split3(x):
    hi = x.astype(BF16)
    r1 = x - hi.astype(F32)
    mid = r1.astype(BF16)
    lo = (r1 - mid.astype(F32)).astype(BF16)
    return hi, mid, lo


def _cumsum_rows(tri, x):
    return sum(_dot(tri, t) for t in _split3(x))


def _dot_nt(a, b):
    return lax.dot_general(a, b, (((1,), (1,)), ((), ())), preferred_element_type=F32)


def _dot_tn(a, b):
    return lax.dot_general(a, b, (((0,), (0,)), ((), ())), preferred_element_type=F32)


def _inproj_body(x_ref, g_ref, w_ref, ws_ref, o_ref, os_ref, ot_ref, h_ref):
    @pl.when(pl.program_id(1) == 0)
    def _():
        x = x_ref[...]
        ms = jnp.mean(x * x, axis=-1, keepdims=True)
        h = (x * lax.rsqrt(ms + EPS) * g_ref[...]).astype(BF16)
        h_ref[...] = h
        small = _dot(h, ws_ref[...])
        os_ref[...] = small
        ot_ref[...] = small.T

    o_ref[...] = _dot(h_ref[...], w_ref[...]).astype(BF16)


def _inproj(x2, g, w_big, w_small):
    n, d = x2.shape
    c = w_big.shape[1]
    tm, tn = min(INPROJ_TM, n), min(INPROJ_TN, c)
    return pl.pallas_call(
        _inproj_body,
        grid=(n // tm, c // tn),
        in_specs=[
            pl.BlockSpec((tm, d), lambda i, j: (i, 0)),
            pl.BlockSpec((1, d), lambda i, j: (0, 0)),
            pl.BlockSpec((d, tn), lambda i, j: (0, j)),
            pl.BlockSpec((d, SMALL_W), lambda i, j: (0, 0)),
        ],
        out_specs=[
            pl.BlockSpec((tm, tn), lambda i, j: (i, j)),
            pl.BlockSpec((tm, SMALL_W), lambda i, j: (i, 0)),
            pl.BlockSpec((SMALL_W, tm), lambda i, j: (0, i)),
        ],
        out_shape=[jax.ShapeDtypeStruct((n, c), BF16), jax.ShapeDtypeStruct((n, SMALL_W), F32),
                   jax.ShapeDtypeStruct((SMALL_W, n), F32)],
        scratch_shapes=[pltpu.VMEM((tm, d), BF16)],
        compiler_params=_params("parallel", "arbitrary"),
        name="inproj",
    )(x2, g, w_big, w_small)


def _gla_body(q_ref, k_ref, v_ref, r_ref, s_ref, wlr_ref, blr_ref, gn_ref, o_ref, st_ref, *, chunk, nchunk, dk, dv):
    @pl.when(pl.program_id(1) == 0)
    def _():
        st_ref[...] = jnp.zeros_like(st_ref)

    z = _dot(s_ref[...].astype(BF16), wlr_ref[...]) + blr_ref[...]
    log_a = _log_sigmoid(z, wide=True) * (1.0 / GLA_TAU)
    tri = _tri(chunk).astype(BF16)
    causal = _tri(chunk)
    scale = dk ** -0.5
    gn = gn_ref[...]
    for c in range(nchunk):
        rows = slice(c * chunk, (c + 1) * chunk)
        bc = _cumsum_rows(tri, log_a[rows])
        b_last = bc[chunk - 1:chunk, :]
        q = q_ref[rows, :].astype(F32) * scale
        k = k_ref[rows, :].astype(F32)
        q_in = (q * jnp.exp(bc)).astype(BF16)
        k_in = (k * jnp.exp(-bc)).astype(BF16)
        k_st = (k * jnp.exp(b_last - bc)).astype(BF16)
        decay = jnp.exp(b_last)
        for h in range(GLA_HEADS):
            ks = slice(h * dk, (h + 1) * dk)
            vs = slice(h * dv, (h + 1) * dv)
            qh = q_in[:, ks]
            vh = v_ref[rows, vs]
            att = jnp.where(causal, _dot_nt(qh, k_in[:, ks]), 0.0)
            st = st_ref[h]
            o = _dot(att.astype(BF16), vh) + _dot_nt(qh, st.astype(BF16))
            st_ref[h] = st * decay[:, ks] + _dot_tn(vh, k_st[:, ks])
            on = o * lax.rsqrt(jnp.mean(o * o, axis=-1, keepdims=True) + EPS) * gn[:, vs]
            o_ref[rows, vs] = (on * _silu(r_ref[rows, vs].astype(F32))).astype(BF16)


def _gla(proj, small, w_lr_pad, b_lr, g_norm, bsz, seq):
    n = proj.shape[0]
    hdk = w_lr_pad.shape[1]
    dk = hdk // GLA_HEADS
    hdv = g_norm.shape[1]
    dv = hdv // GLA_HEADS
    rows = min(GLA_ROWS, seq)
    nt = seq // rows
    assert hdv == 2 * hdk
    row = lambda b, t: b * nt + t
    return pl.pallas_call(
        functools.partial(_gla_body, chunk=GLA_CHUNK, nchunk=rows // GLA_CHUNK, dk=dk, dv=dv),
        grid=(bsz, nt),
        in_specs=[
            pl.BlockSpec((rows, hdk), lambda b, t: (row(b, t), 0)),
            pl.BlockSpec((rows, hdk), lambda b, t: (row(b, t), 1)),
            pl.BlockSpec((rows, hdv), lambda b, t: (row(b, t), 1)),
            pl.BlockSpec((rows, hdv), lambda b, t: (row(b, t), 2)),
            pl.BlockSpec((rows, SMALL_W), lambda b, t: (row(b, t), 0)),
            pl.BlockSpec((SMALL_W, hdk), lambda b, t: (0, 0)),
            pl.BlockSpec((1, hdk), lambda b, t: (0, 0)),
            pl.BlockSpec((1, hdv), lambda b, t: (0, 0)),
        ],
        out_specs=pl.BlockSpec((rows, hdv), lambda b, t: (row(b, t), 0)),
        out_shape=jax.ShapeDtypeStruct((n, hdv), BF16),
        scratch_shapes=[pltpu.VMEM((GLA_HEADS, dv, dk), F32)],
        compiler_params=_params("parallel", "arbitrary"),
        name="gla",
    )(proj, proj, proj, proj, small, w_lr_pad, b_lr, g_norm)


def _mlstm_body(xm_ref, z_ref, cw_ref, cb_ref, wq_ref, wk_ref, wv_ref, wif_ref, bif_ref,
                skip_ref, gn_ref, o_ref, xf_ref, q_sc, k_sc, v_sc, h_sc, xc_sc, c_sc, m_sc, *, chunk, nchunk, dh):
    rows_blk = chunk * nchunk
    halo = F32_SUBLANES

    @pl.when(pl.program_id(1) == 0)
    def _():
        xf_ref[0:halo, :] = jnp.zeros((halo, xf_ref.shape[1]), F32)
        c_sc[...] = jnp.zeros_like(c_sc)
        m_sc[...] = jnp.zeros_like(m_sc)

    xf_ref[halo:halo + rows_blk, :] = xm_ref[...].astype(F32)
    for h in range(MLSTM_HEADS):
        hs = slice(h * dh, (h + 1) * dh)
        xf = xf_ref[:, hs]
        conv = cb_ref[:, hs]
        for j in range(MLSTM_CONV - 1):
            conv = conv + cw_ref[j:j + 1, hs] * pltpu.roll(xf, MLSTM_CONV - 1 - j, 0)[halo:, :]
        conv = conv + cw_ref[MLSTM_CONV - 1:MLSTM_CONV, hs] * xf[halo:, :]
        xc = _silu(conv)
        xc_sc[:, hs] = xc
        xcb = xc.astype(BF16)
        q_sc[:, hs] = _dot(xcb, wq_ref[h]).astype(BF16)
        k_sc[:, hs] = _dot(xcb, wk_ref[h]).astype(BF16)
        v_sc[:, hs] = _dot(xm_ref[:, hs], wv_ref[h]).astype(BF16)
    xf_ref[0:halo, :] = xf_ref[rows_blk:rows_blk + halo, :]
    qa, ka, va = q_sc[...], k_sc[...], v_sc[...]

    gcol = _dot(qa, wif_ref[0]) + _dot(ka, wif_ref[1]) + _dot(va, wif_ref[2]) + bif_ref[...]
    lane = lax.broadcasted_iota(jnp.int32, gcol.shape, 1)
    gcol = jnp.where(lane < MLSTM_HEADS, gcol, _log_sigmoid(gcol))

    tri = _tri(chunk).astype(BF16)
    causal = _tri(chunk)
    qscale = dh ** -0.5
    ones_aug = jnp.ones((chunk, LANES), BF16)
    for c in range(nchunk):
        rows = slice(c * chunk, (c + 1) * chunk)
        gc = gcol[rows]
        cum_c = _cumsum_rows(tri, gc)
        gr = gc.T
        cum_r = cum_c.T
        for h in range(MLSTM_HEADS):
            hs = slice(h * dh, (h + 1) * dh)
            fh = MLSTM_HEADS + h
            i_col, b_col = gc[:, h:h + 1], cum_c[:, fh:fh + 1]
            i_row, b_row = gr[h:h + 1, :], cum_r[fh:fh + 1, :]
            m_st = m_sc[h][:, 0:1]
            d_log = jnp.where(causal, b_col - b_row + i_row, -jnp.inf)
            m_inter = b_col + m_st
            m_t = jnp.maximum(m_inter, jnp.max(d_log, axis=-1, keepdims=True))
            w_intra = jnp.exp(d_log - m_t)
            w_inter = jnp.exp(m_inter - m_t)
            qc = (q_sc[rows, hs].astype(F32) * qscale).astype(BF16)
            kc = k_sc[rows, hs]
            vc = v_sc[rows, hs]
            s = _dot_nt(qc, kc) * w_intra
            c_aug = c_sc[h]
            q_state = _dot(qc, c_aug.astype(BF16))
            num = _dot(s.astype(BF16), vc) + w_inter * q_state[:, :dh]
            qn = jnp.sum(s, axis=-1, keepdims=True) + w_inter * q_state[:, dh:dh + 1]
            h_sc[rows, hs] = num / jnp.maximum(jnp.abs(qn), jnp.exp(-m_t))
            g = b_row[:, chunk - 1:chunk]
            m_new = jnp.maximum(g + m_st, jnp.max(g - b_row + i_row, axis=-1, keepdims=True))
            wa = jnp.exp(g - b_col + i_col - m_new)
            dec = jnp.exp(g + m_st - m_new)
            kw = (kc.astype(F32) * wa).astype(BF16)
            v_aug = jnp.concatenate([vc, ones_aug], axis=1)
            c_sc[h] = dec * c_aug + _dot_tn(kw, v_aug)
            m_sc[h] = jnp.broadcast_to(m_new, (1, LANES))

    for h in range(MLSTM_HEADS):
        hs = slice(h * dh, (h + 1) * dh)
        hh = h_sc[:, hs]
        hn = hh * lax.rsqrt(jnp.mean(hh * hh, axis=-1, keepdims=True) + EPS) * gn_ref[:, hs]
        zg = _silu(z_ref[:, hs].astype(F32))
        o_ref[:, hs] = ((hn + skip_ref[:, hs] * xc_sc[:, hs]) * zg).astype(BF16)


def _mlstm(proj, conv_w, conv_b, wq_bd, wk_bd, wv_bd, w_if, b_if, skip, g_norm, bsz, seq, col_blk):
    n = proj.shape[0]
    inner = conv_w.shape[1]
    dh = inner // MLSTM_HEADS
    rows = min(MLSTM_ROWS, seq)
    chunk = min(MLSTM_CHUNK, rows)
    nt = seq // rows
    row = lambda b, t: b * nt + t
    full = lambda shape: pl.BlockSpec(shape, lambda b, t: (0,) * len(shape))
    return pl.pallas_call(
        functools.partial(_mlstm_body, chunk=chunk, nchunk=rows // chunk, dh=dh),
        grid=(bsz, nt),
        in_specs=[
            pl.BlockSpec((rows, inner), lambda b, t: (row(b, t), col_blk)),
            pl.BlockSpec((rows, inner), lambda b, t: (row(b, t), col_blk + 1)),
            full((MLSTM_CONV, inner)),
            full((1, inner)),
            full((MLSTM_HEADS, dh, dh)),
            full((MLSTM_HEADS, dh, dh)),
            full((MLSTM_HEADS, dh, dh)),
            full((3, inner, LANES)),
            full((1, LANES)),
            full((1, inner)),
            full((1, inner)),
        ],
        out_specs=pl.BlockSpec((rows, inner), lambda b, t: (row(b, t), 0)),
        out_shape=jax.ShapeDtypeStruct((n, inner), BF16),
        scratch_shapes=[
            pltpu.VMEM((rows + F32_SUBLANES, inner), F32),
            pltpu.VMEM((rows, inner), BF16),
            pltpu.VMEM((rows, inner), BF16),
            pltpu.VMEM((rows, inner), BF16),
            pltpu.VMEM((rows, inner), F32),
            pltpu.VMEM((rows, inner), F32),
            pltpu.VMEM((MLSTM_HEADS, dh, dh + LANES), F32),
            pltpu.VMEM((MLSTM_HEADS, 1, LANES), F32),
        ],
        compiler_params=_params("parallel", "arbitrary"),
        name="mlstm",
    )(proj, proj, conv_w, conv_b, wq_bd, wk_bd, wv_bd, w_if, b_if, skip, g_norm)


def _fox_body(q_ref, k_ref, v_ref, og_ref, ft_ref, bf_ref, o_ref, va_ref, acc_ref, s_ref, fr_ref, *, tq, nq, dh, hpb):
    hp = pl.program_id(1)
    qi = pl.program_id(2)

    @pl.when(qi == 0)
    def _():
        for h in range(hpb):
            va_ref[h, :, :dh] = v_ref[:, h * dh:(h + 1) * dh]
            va_ref[h, :, dh:] = jnp.ones((va_ref.shape[1], LANES), BF16)

    @pl.when(jnp.logical_and(qi == 0, hp == 0))
    def _():
        log_f = _log_sigmoid(ft_ref[...] + bf_ref[...])
        tri_u = _tri(tq, upper=True).astype(BF16)
        carry = jnp.zeros((FOX_HEADS, 1), F32)
        for j in range(nq):
            cum = sum(_dot(t, tri_u) for t in _split3(log_f[:, j * tq:(j + 1) * tq])) + carry
            carry = cum[:, tq - 1:tq]
            for r in range(FOX_HEADS):
                fr_ref[j, r] = cum[r:r + 1, :] * LOG2E

    qs = [(q_ref[:, h * dh:(h + 1) * dh].astype(F32) * (dh ** -0.5 * LOG2E)).astype(BF16) for h in range(hpb)]
    acc_ref[...] = jnp.zeros_like(acc_ref)

    def scores(h, j):
        start = pl.multiple_of(j * tq, tq)
        return _dot_nt(qs[h], k_ref[pl.ds(start, tq), h * dh:(h + 1) * dh]) - fr_ref[j, hp * hpb + h]

    def step(j, ms, slot, last):
        start = pl.multiple_of(j * tq, tq)
        out = []
        for h in range(hpb):
            s = s_ref[slot, h]
            if last:
                s = jnp.where(_tri(tq), s, -jnp.inf)
            else:
                s_ref[1 - slot, h] = scores(h, j + 1)
            m_new = jnp.maximum(ms[h], jnp.max(s, axis=-1, keepdims=True))
            p = jnp.exp2(s - m_new).astype(BF16)
            acc_ref[h] = jnp.exp2(ms[h] - m_new) * acc_ref[h] + _dot(p, va_ref[h, pl.ds(start, tq), :])
            out.append(m_new)
        return tuple(out)

    for h in range(hpb):
        s_ref[0, h] = scores(h, 0)
    m0 = tuple(jnp.full((tq, 1), -jnp.inf, F32) for _ in range(hpb))

    def steps(first, count, ms):
        for u in range(count):
            ms = step(first + u, ms, u % 2, False)
        return ms

    unroll = FOX_UNROLL
    ms = lax.fori_loop(0, qi // unroll, lambda t, ms: steps(unroll * t, unroll, ms), m0)

    for rem in range(unroll):
        @pl.when(qi % unroll == rem)
        def _():
            step(qi, steps(qi - rem, rem, ms), rem % 2, True)
            for h in range(hpb):
                acc = acc_ref[h]
                gate = _sigmoid(og_ref[:, h * dh:(h + 1) * dh].astype(F32))
                o_ref[:, h * dh:(h + 1) * dh] = (acc[:, :dh] / acc[:, dh:] * gate).astype(BF16)


def _fox(proj, small_t, b_f, bsz, seq, dh, q_blk, k_blk, v_blk, og_blk, f_blk):
    n = proj.shape[0]
    tq = min(FOX_TQ, seq)
    nq = seq // tq
    hpb = FOX_HEADS_PER_STEP
    w = hpb * dh
    assert FOX_HEADS % hpb == 0 and q_blk % hpb == 0 and k_blk % hpb == 0 and v_blk % hpb == 0 and og_blk % hpb == 0
    return pl.pallas_call(
        functools.partial(_fox_body, tq=tq, nq=nq, dh=dh, hpb=hpb),
        grid=(bsz, FOX_HEADS // hpb, nq),
        in_specs=[
            pl.BlockSpec((tq, w), lambda b, h, i: (b * nq + i, q_blk // hpb + h)),
            pl.BlockSpec((seq, w), lambda b, h, i: (b, k_blk // hpb + h)),
            pl.BlockSpec((seq, w), lambda b, h, i: (b, v_blk // hpb + h)),
            pl.BlockSpec((tq, w), lambda b, h, i: (b * nq + i, og_blk // hpb + h)),
            pl.BlockSpec((FOX_HEADS, seq), lambda b, h, i: (f_blk, b)),
            pl.BlockSpec((FOX_HEADS, 1), lambda b, h, i: (0, 0)),
        ],
        out_specs=pl.BlockSpec((tq, w), lambda b, h, i: (b * nq + i, h)),
        out_shape=jax.ShapeDtypeStruct((n, FOX_HEADS * dh), BF16),
        scratch_shapes=[pltpu.VMEM((hpb, seq, dh + LANES), BF16), pltpu.VMEM((hpb, tq, dh + LANES), F32),
                        pltpu.VMEM((2, hpb, tq, tq), F32), pltpu.VMEM((nq, FOX_HEADS, 1, tq), F32)],
        compiler_params=_params("parallel", "arbitrary", "arbitrary"),
        name="fox",
    )(proj, proj, proj, proj, small_t, b_f)


def _merge_body(x_ref, y0_ref, y1_ref, y2_ref, gt_ref, bg_ref, wb_ref, wo_ref, o_ref, *, d):
    merged = None
    for n, y_ref in enumerate((y0_ref, y1_ref, y2_ref)):
        cs = slice(n * d, (n + 1) * d)
        gate = _sigmoid(gt_ref[:, cs].astype(F32) + bg_ref[:, cs])
        term = _dot(y_ref[...], wb_ref[n]) * gate
        merged = term if merged is None else merged + term
    o_ref[...] = x_ref[...] + _dot(merged.astype(BF16), wo_ref[...])


def _merge(x2, y_gla, y_mlstm, y_fox, proj, b_gate, w_branch, w_out, gate_blk):
    n, d = x2.shape
    tm = min(MERGE_TM, n)
    rowblk = lambda shape: pl.BlockSpec(shape, lambda i: (i, 0))
    return pl.pallas_call(
        functools.partial(_merge_body, d=d),
        grid=(n // tm,),
        in_specs=[
            rowblk((tm, d)), rowblk((tm, d)), rowblk((tm, d)), rowblk((tm, d)),
            pl.BlockSpec((tm, N_BRANCH * d), lambda i: (i, gate_blk)),
            pl.BlockSpec((1, N_BRANCH * d), lambda i: (0, 0)),
            pl.BlockSpec((N_BRANCH, d, d), lambda i: (0, 0, 0)),
            pl.BlockSpec((d, d), lambda i: (0, 0)),
        ],
        out_specs=rowblk((tm, d)),
        out_shape=jax.ShapeDtypeStruct((n, d), F32),
        compiler_params=_params("parallel"),
        name="merge",
    )(x2, y_gla, y_mlstm, y_fox, proj, b_gate, w_branch, w_out)


def _ffn_body(x_ref, xp_ref, g_ref, wu_ref, cw_ref, cb_ref, wd_ref, gf_ref, o_ref, h_ref,
              *, tm, halo, blocks_per_seq, dff, chunk, final):
    i = pl.program_id(0)

    def norm(x, gain):
        return x * lax.rsqrt(jnp.mean(x * x, axis=-1, keepdims=True) + EPS) * gain

    x = x_ref[...]
    hp = norm(xp_ref[...], g_ref[...]).astype(BF16)
    h_ref[0:halo, :] = jnp.where(i % blocks_per_seq != 0, hp, jnp.zeros_like(hp))
    h_ref[halo:halo + tm, :] = norm(x, g_ref[...]).astype(BF16)
    h = h_ref[...]

    def conv(lo, hi):
        u = _dot(h, wu_ref[:, lo:hi])
        y = cb_ref[:, lo:hi]
        for j in range(FFN_CONV - 1):
            y = y + cw_ref[j:j + 1, lo:hi] * pltpu.roll(u, FFN_CONV - 1 - j, 0)[halo:, :]
        return y + cw_ref[FFN_CONV - 1:FFN_CONV, lo:hi] * u[halo:, :]

    acc = x
    for lo in range(0, dff, chunk):
        hi = min(lo + chunk, dff)
        act = _silu(conv(dff + lo, dff + hi)) * conv(lo, hi)
        acc = acc + _dot(act.astype(BF16), wd_ref[lo:hi, :])
    o_ref[...] = norm(acc, gf_ref[...]) if final else acc


def _ffn(x2, g, w_up, conv_w, conv_b, w_down, seq, final_gain=None):
    n, d = x2.shape
    dff = w_down.shape[0]
    tm = min(FFN_TM, seq)
    halo = BF16_SUBLANES
    hb = tm // halo
    final = final_gain is not None
    resident = lambda shape: pl.BlockSpec(shape, lambda i: (0,) * len(shape), pipeline_mode=pl.Buffered(1))
    return pl.pallas_call(
        functools.partial(_ffn_body, tm=tm, halo=halo, blocks_per_seq=seq // tm, dff=dff, chunk=FFN_CHUNK,
                          final=final),
        grid=(n // tm,),
        in_specs=[
            pl.BlockSpec((tm, d), lambda i: (i, 0)),
            pl.BlockSpec((halo, d), lambda i: (jnp.maximum(i * hb - 1, 0), 0)),
            resident((1, d)),
            resident((d, 2 * dff)),
            resident((FFN_CONV, 2 * dff)),
            resident((1, 2 * dff)),
            resident((dff, d)),
            resident((1, d)),
        ],
        out_specs=pl.BlockSpec((tm, d), lambda i: (i, 0)),
        out_shape=jax.ShapeDtypeStruct((n, d), F32),
        scratch_shapes=[pltpu.VMEM((halo + tm, d), BF16)],
        compiler_params=_params("parallel"),
        name="ffn_final" if final else "ffn",
    )(x2, x2, g, w_up, conv_w, conv_b, w_down, final_gain if final else g)


def _block_diag_heads(w, n_heads):
    nblk, bc, bd = w.shape
    per = nblk // n_heads
    tiled = jnp.tile(w.reshape(n_heads, per * bc, bd), (1, 1, per))
    same_block = (jnp.arange(per * bc)[:, None] // bc) == (jnp.arange(per * bd)[None, :] // bd)
    return jnp.where(same_block, tiled, 0.0)


def _pad_lanes(a, width):
    return jnp.pad(a, ((0, 0),) * (a.ndim - 1) + ((0, width - a.shape[-1]),))


def kernel(x, norm_mix, w_in, b_gate, gla_w_lr, gla_b_lr, gla_norm, mlstm_conv_w, mlstm_conv_b, mlstm_wq, mlstm_wk,
           mlstm_wv, mlstm_w_i, mlstm_b_i, mlstm_w_f, mlstm_b_f, mlstm_skip, mlstm_norm, fox_b_f, w_branch, w_out,
           norm_ffn, ffn_w_up, ffn_conv_w, ffn_conv_b, ffn_w_down, norm_final):
    bsz, seq, d = x.shape
    depth = w_in.shape[0]
    hdk = gla_w_lr.shape[2]
    hdv = gla_norm.shape[1]
    inner = mlstm_conv_w.shape[2]
    fox_w = w_branch.shape[2]
    fox_dh = fox_w // FOX_HEADS
    dff = ffn_w_down.shape[1]
    assert hdv == d and inner == d and fox_w == d and 2 * hdk == d

    o_glr = 2 * hdk + hdv
    o_gr = o_glr + GLA_RANK
    o_ff = o_gr + hdv + 2 * inner + 3 * fox_w
    o_fog = o_ff + FOX_HEADS
    blk = {"gq": 0, "gk": 1, "gv": 1, "gr": 2, "mx": 3, "fq": 5 * d // fox_dh, "fk": 6 * d // fox_dh,
           "fv": 7 * d // fox_dh, "fog": 8 * d // fox_dh, "gates": 3}

    x2 = x.reshape(bsz * seq, d)
    for l in range(depth):
        wl = w_in[l]
        w_big = jnp.concatenate([wl[:, :o_glr], wl[:, o_gr:o_ff], wl[:, o_fog:]], axis=1).astype(BF16)
        w_small = _pad_lanes(jnp.concatenate([wl[:, o_glr:o_gr], wl[:, o_ff:o_fog]], axis=1), SMALL_W).astype(BF16)
        proj, small, small_t = _inproj(x2, norm_mix[l][None, :], w_big, w_small)

        w_lr_pad = jnp.pad(gla_w_lr[l], ((0, SMALL_W - GLA_RANK), (0, 0))).astype(BF16)
        y_gla = _gla(proj, small, w_lr_pad, gla_b_lr[l][None, :], gla_norm[l][None, :], bsz, seq)

        w_if = _pad_lanes(jnp.concatenate([mlstm_w_i[l], mlstm_w_f[l]], axis=1), LANES)
        b_if = _pad_lanes(jnp.concatenate([mlstm_b_i[l], mlstm_b_f[l]])[None, :], LANES)
        w_if3 = w_if.reshape(3, inner, LANES).astype(BF16)
        y_mlstm = _mlstm(
            proj, mlstm_conv_w[l], mlstm_conv_b[l][None, :],
            _block_diag_heads(mlstm_wq[l], MLSTM_HEADS).astype(BF16),
            _block_diag_heads(mlstm_wk[l], MLSTM_HEADS).astype(BF16),
            _block_diag_heads(mlstm_wv[l], MLSTM_HEADS).astype(BF16),
            w_if3, b_if, mlstm_skip[l][None, :], mlstm_norm[l][None, :], bsz, seq, blk["mx"])

        y_fox = _fox(proj, small_t, fox_b_f[l][:, None], bsz, seq, fox_dh,
                     blk["fq"], blk["fk"], blk["fv"], blk["fog"], GLA_RANK // FOX_HEADS)

        x2 = _merge(x2, y_gla, y_mlstm, y_fox, proj, b_gate[l].reshape(1, N_BRANCH * d),
                    w_branch[l].astype(BF16), w_out[l].astype(BF16), blk["gates"])

        x2 = _ffn(x2, norm_ffn[l][None, :], ffn_w_up[l].astype(BF16), ffn_conv_w[l], ffn_conv_b[l][None, :],
                  ffn_w_down[l].astype(BF16), seq, final_gain=norm_final[None, :] if l == depth - 1 else None)
    return x2.reshape(bsz, seq, d)
```

```python
import functools

import jax
import jax.numpy as jnp
from jax import lax
from jax.experimental import pallas as pl
from jax.experimental.pallas import tpu as pltpu

F32 = jnp.float32
BF16 = jnp.bfloat16
EPS = 1e-6
LOG2E = 1.4426950408889634

LANES = 128
F32_SUBLANES = 8
BF16_SUBLANES = 16
VMEM_LIMIT_BYTES = 56 * 1024 * 1024

GLA_HEADS = 4
GLA_RANK = 16
GLA_TAU = 16.0
GLA_CHUNK = 64
MLSTM_HEADS = 4
MLSTM_CONV = 4
MLSTM_BLOCK = 4
FOX_HEADS = 8
N_BRANCH = 3
FFN_CONV = 3
SMALL_W = LANES

INPROJ_TM = 1024
INPROJ_TN = 2048
GLA_ROWS = 512
MLSTM_CHUNK = 256
MLSTM_ROWS = 512
FOX_TQ = 512
FOX_HEADS_PER_STEP = 2
FOX_UNROLL = 4
MERGE_TM = 512
FFN_TM = 512
FFN_CHUNK = 1024


def _params(*sem):
    return pltpu.CompilerParams(dimension_semantics=sem, vmem_limit_bytes=VMEM_LIMIT_BYTES)


def _log_sigmoid(z, wide=False):
    e = jnp.exp(-jnp.abs(z))
    return jnp.minimum(z, 0.0) - (jnp.log(1.0 + e) if wide else jnp.log1p(e))


def _sigmoid(z):
    return 1.0 / (1.0 + jnp.exp(-z))


def _silu(z):
    return z * _sigmoid(z)


def _tri(n, upper=False):
    r = lax.broadcasted_iota(jnp.int32, (n, n), 0)
    c = lax.broadcasted_iota(jnp.int32, (n, n), 1)
    return (r <= c) if upper else (r >= c)


def _dot(a, b, **kw):
    return jnp.dot(a, b, preferred_element_type=F32, **kw)


def _split3(x):
    hi = x.astype(BF16)
    r1 = x - hi.astype(F32)
    mid = r1.astype(BF16)
    lo = (r1 - mid.astype(F32)).astype(BF16)
    return hi, mid, lo


def _cumsum_rows(tri, x):
    return sum(_dot(tri, t) for t in _split3(x))


def _dot_nt(a, b):
    return lax.dot_general(a, b, (((1,), (1,)), ((), ())), preferred_element_type=F32)


def _dot_tn(a, b):
    return lax.dot_general(a, b, (((0,), (0,)), ((), ())), preferred_element_type=F32)


def _inproj_body(x_ref, g_ref, w_ref, ws_ref, o_ref, os_ref, ot_ref, h_ref):
    @pl.when(pl.program_id(1) == 0)
    def _():
        x = x_ref[...]
        ms = jnp.mean(x * x, axis=-1, keepdims=True)
        h = (x * lax.rsqrt(ms + EPS) * g_ref[...]).astype(BF16)
        h_ref[...] = h
        small = _dot(h, ws_ref[...])
        os_ref[...] = small
        ot_ref[...] = small.T

    o_ref[...] = _dot(h_ref[...], w_ref[...]).astype(BF16)


def _inproj(x2, g, w_big, w_small):
    n, d = x2.shape
    c = w_big.shape[1]
    tm, tn = min(INPROJ_TM, n), min(INPROJ_TN, c)
    return pl.pallas_call(
        _inproj_body,
        grid=(n // tm, c // tn),
        in_specs=[
            pl.BlockSpec((tm, d), lambda i, j: (i, 0)),
            pl.BlockSpec((1, d), lambda i, j: (0, 0)),
            pl.BlockSpec((d, tn), lambda i, j: (0, j)),
            pl.BlockSpec((d, SMALL_W), lambda i, j: (0, 0)),
        ],
        out_specs=[
            pl.BlockSpec((tm, tn), lambda i, j: (i, j)),
            pl.BlockSpec((tm, SMALL_W), lambda i, j: (i, 0)),
            pl.BlockSpec((SMALL_W, tm), lambda i, j: (0, i)),
        ],
        out_shape=[jax.ShapeDtypeStruct((n, c), BF16), jax.ShapeDtypeStruct((n, SMALL_W), F32),
                   jax.ShapeDtypeStruct((SMALL_W, n), F32)],
        scratch_shapes=[pltpu.VMEM((tm, d), BF16)],
        compiler_params=_params("parallel", "arbitrary"),
        name="inproj",
    )(x2, g, w_big, w_small)


def _gla_body(q_ref, k_ref, v_ref, r_ref, s_ref, wlr_ref, blr_ref, gn_ref, o_ref, st_ref, *, chunk, nchunk, dk, dv):
    @pl.when(pl.program_id(1) == 0)
    def _():
        st_ref[...] = jnp.zeros_like(st_ref)

    z = _dot(s_ref[...].astype(BF16), wlr_ref[...]) + blr_ref[...]
    log_a = _log_sigmoid(z, wide=True) * (1.0 / GLA_TAU)
    tri = _tri(chunk).astype(BF16)
    causal = _tri(chunk)
    scale = dk ** -0.5
    gn = gn_ref[...]
    for c in range(nchunk):
        rows = slice(c * chunk, (c + 1) * chunk)
        bc = _cumsum_rows(tri, log_a[rows])
        b_last = bc[chunk - 1:chunk, :]
        q = q_ref[rows, :].astype(F32) * scale
        k = k_ref[rows, :].astype(F32)
        q_in = (q * jnp.exp(bc)).astype(BF16)
        k_in = (k * jnp.exp(-bc)).astype(BF16)
        k_st = (k * jnp.exp(b_last - bc)).astype(BF16)
        decay = jnp.exp(b_last)
        for h in range(GLA_HEADS):
            ks = slice(h * dk, (h + 1) * dk)
            vs = slice(h * dv, (h + 1) * dv)
            qh = q_in[:, ks]
            vh = v_ref[rows, vs]
            att = jnp.where(causal, _dot_nt(qh, k_in[:, ks]), 0.0)
            st = st_ref[h]
            o = _dot(att.astype(BF16), vh) + _dot_nt(qh, st.astype(BF16))
            st_ref[h] = st * decay[:, ks] + _dot_tn(vh, k_st[:, ks])
            on = o * lax.rsqrt(jnp.mean(o * o, axis=-1, keepdims=True) + EPS) * gn[:, vs]
            o_ref[rows, vs] = (on * _silu(r_ref[rows, vs].astype(F32))).astype(BF16)


def _gla(proj, small, w_lr_pad, b_lr, g_norm, bsz, seq):
    n = proj.shape[0]
    hdk = w_lr_pad.shape[1]
    dk = hdk // GLA_HEADS
    hdv = g_norm.shape[1]
    dv = hdv // GLA_HEADS
    rows = min(GLA_ROWS, seq)
    nt = seq // rows
    assert hdv == 2 * hdk
    row = lambda b, t: b * nt + t
    return pl.pallas_call(
        functools.partial(_gla_body, chunk=GLA_CHUNK, nchunk=rows // GLA_CHUNK, dk=dk, dv=dv),
        grid=(bsz, nt),
        in_specs=[
            pl.BlockSpec((rows, hdk), lambda b, t: (row(b, t), 0)),
            pl.BlockSpec((rows, hdk), lambda b, t: (row(b, t), 1)),
            pl.BlockSpec((rows, hdv), lambda b, t: (row(b, t), 1)),
            pl.BlockSpec((rows, hdv), lambda b, t: (row(b, t), 2)),
            pl.BlockSpec((rows, SMALL_W), lambda b, t: (row(b, t), 0)),
            pl.BlockSpec((SMALL_W, hdk), lambda b, t: (0, 0)),
            pl.BlockSpec((1, hdk), lambda b, t: (0, 0)),
            pl.BlockSpec((1, hdv), lambda b, t: (0, 0)),
        ],
        out_specs=pl.BlockSpec((rows, hdv), lambda b, t: (row(b, t), 0)),
        out_shape=jax.ShapeDtypeStruct((n, hdv), BF16),
        scratch_shapes=[pltpu.VMEM((GLA_HEADS, dv, dk), F32)],
        compiler_params=_params("parallel", "arbitrary"),
        name="gla",
    )(proj, proj, proj, proj, small, w_lr_pad, b_lr, g_norm)


def _mlstm_body(xm_ref, z_ref, cw_ref, cb_ref, wq_ref, wk_ref, wv_ref, wif_ref, bif_ref,
                skip_ref, gn_ref, o_ref, xf_ref, q_sc, k_sc, v_sc, h_sc, xc_sc, c_sc, m_sc, *, chunk, nchunk, dh):
    rows_blk = chunk * nchunk
    halo = F32_SUBLANES

    @pl.when(pl.program_id(1) == 0)
    def _():
        xf_ref[0:halo, :] = jnp.zeros((halo, xf_ref.shape[1]), F32)
        c_sc[...] = jnp.zeros_like(c_sc)
        m_sc[...] = jnp.zeros_like(m_sc)

    xf_ref[halo:halo + rows_blk, :] = xm_ref[...].astype(F32)
    for h in range(MLSTM_HEADS):
        hs = slice(h * dh, (h + 1) * dh)
        xf = xf_ref[:, hs]
        conv = cb_ref[:, hs]
        for j in range(MLSTM_CONV - 1):
            conv = conv + cw_ref[j:j + 1, hs] * pltpu.roll(xf, MLSTM_CONV - 1 - j, 0)[halo:, :]
        conv = conv + cw_ref[MLSTM_CONV - 1:MLSTM_CONV, hs] * xf[halo:, :]
        xc = _silu(conv)
        xc_sc[:, hs] = xc
        xcb = xc.astype(BF16)
        q_sc[:, hs] = (_dot(xcb, wq_ref[h]) * dh ** -0.5).astype(BF16)
        k_sc[:, hs] = _dot(xcb, wk_ref[h]).astype(BF16)
        v_sc[:, hs] = _dot(xm_ref[:, hs], wv_ref[h]).astype(BF16)
    xf_ref[0:halo, :] = xf_ref[rows_blk:rows_blk + halo, :]
    qa, ka, va = q_sc[...], k_sc[...], v_sc[...]

    gcol = _dot(qa, wif_ref[0]) + _dot(ka, wif_ref[1]) + _dot(va, wif_ref[2]) + bif_ref[...]
    log_i = gcol[:, :LANES] * LOG2E
    log_f = _log_sigmoid(gcol[:, LANES:]) * LOG2E

    tri = _tri(chunk).astype(BF16)
    causal = _tri(chunk)
    ones_aug = jnp.ones((chunk, LANES), BF16)
    for c in range(nchunk):
        rows = slice(c * chunk, (c + 1) * chunk)
        cum_c = _cumsum_rows(tri, log_f[rows])
        a_c = log_i[rows] - cum_c
        cum_r = cum_c.T
        a_r = a_c.T
        for h in range(MLSTM_HEADS):
            hs = slice(h * dh, (h + 1) * dh)
            b_col, a_col = cum_c[:, h:h + 1], a_c[:, h:h + 1]
            b_row, a_row = cum_r[h:h + 1, :], a_r[h:h + 1, :]
            m_st = m_sc[h][:, 0:1]
            d_log = jnp.where(causal, b_col + a_row, -jnp.inf)
            m_inter = b_col + m_st
            m_t = jnp.maximum(m_inter, jnp.max(d_log, axis=-1, keepdims=True))
            w_intra = jnp.exp2(d_log - m_t)
            w_inter = jnp.exp2(m_inter - m_t)
            qc = q_sc[rows, hs]
            kc = k_sc[rows, hs]
            vc = v_sc[rows, hs]
            s = _dot_nt(qc, kc) * w_intra
            c_aug = c_sc[h]
            q_state = _dot(qc, c_aug.astype(BF16))
            num = _dot(s.astype(BF16), vc) + w_inter * q_state[:, :dh]
            qn = jnp.sum(s, axis=-1, keepdims=True) + w_inter * q_state[:, dh:dh + 1]
            h_sc[rows, hs] = num / jnp.maximum(jnp.abs(qn), jnp.exp2(-m_t))
            g = b_row[:, chunk - 1:chunk]
            m_new = jnp.maximum(g + m_st, g + jnp.max(a_row, axis=-1, keepdims=True))
            wa = jnp.exp2(g + a_col - m_new)
            dec = jnp.exp2(g + m_st - m_new)
            kw = (kc.astype(F32) * wa).astype(BF16)
            v_aug = jnp.concatenate([vc, ones_aug], axis=1)
            c_sc[h] = dec * c_aug + _dot_tn(kw, v_aug)
            m_sc[h] = jnp.broadcast_to(m_new, (1, LANES))

    for h in range(MLSTM_HEADS):
        hs = slice(h * dh, (h + 1) * dh)
        hh = h_sc[:, hs]
        hn = hh * lax.rsqrt(jnp.mean(hh * hh, axis=-1, keepdims=True) + EPS) * gn_ref[:, hs]
        zg = _silu(z_ref[:, hs].astype(F32))
        o_ref[:, hs] = ((hn + skip_ref[:, hs] * xc_sc[:, hs]) * zg).astype(BF16)


def _mlstm(proj, conv_w, conv_b, wq_bd, wk_bd, wv_bd, w_if, b_if, skip, g_norm, bsz, seq, col_blk):
    n = proj.shape[0]
    inner = conv_w.shape[1]
    dh = inner // MLSTM_HEADS
    rows = min(MLSTM_ROWS, seq)
    chunk = min(MLSTM_CHUNK, rows)
    nt = seq // rows
    row = lambda b, t: b * nt + t
    full = lambda shape: pl.BlockSpec(shape, lambda b, t: (0,) * len(shape))
    return pl.pallas_call(
        functools.partial(_mlstm_body, chunk=chunk, nchunk=rows // chunk, dh=dh),
        grid=(bsz, nt),
        in_specs=[
            pl.BlockSpec((rows, inner), lambda b, t: (row(b, t), col_blk)),
            pl.BlockSpec((rows, inner), lambda b, t: (row(b, t), col_blk + 1)),
            full((MLSTM_CONV, inner)),
            full((1, inner)),
            full((MLSTM_HEADS, dh, dh)),
            full((MLSTM_HEADS, dh, dh)),
            full((MLSTM_HEADS, dh, dh)),
            full((3, inner, 2 * LANES)),
            full((1, 2 * LANES)),
            full((1, inner)),
            full((1, inner)),
        ],
        out_specs=pl.BlockSpec((rows, inner), lambda b, t: (row(b, t), 0)),
        out_shape=jax.ShapeDtypeStruct((n, inner), BF16),
        scratch_shapes=[
            pltpu.VMEM((rows + F32_SUBLANES, inner), F32),
            pltpu.VMEM((rows, inner), BF16),
            pltpu.VMEM((rows, inner), BF16),
            pltpu.VMEM((rows, inner), BF16),
            pltpu.VMEM((rows, inner), F32),
            pltpu.VMEM((rows, inner), F32),
            pltpu.VMEM((MLSTM_HEADS, dh, dh + LANES), F32),
            pltpu.VMEM((MLSTM_HEADS, 1, LANES), F32),
        ],
        compiler_params=_params("parallel", "arbitrary"),
        name="mlstm",
    )(proj, proj, conv_w, conv_b, wq_bd, wk_bd, wv_bd, w_if, b_if, skip, g_norm)


def _fox_body(q_ref, k_ref, v_ref, og_ref, ft_ref, bf_ref, o_ref, va_ref, acc_ref, s_ref, fr_ref, *, tq, nq, dh, hpb):
    hp = pl.program_id(1)
    qi = pl.program_id(2)

    @pl.when(qi == 0)
    def _():
        for h in range(hpb):
            va_ref[h, :, :dh] = v_ref[:, h * dh:(h + 1) * dh]
            va_ref[h, :, dh:] = jnp.ones((va_ref.shape[1], LANES), BF16)

    @pl.when(jnp.logical_and(qi == 0, hp == 0))
    def _():
        log_f = _log_sigmoid(ft_ref[...] + bf_ref[...])
        tri_u = _tri(tq, upper=True).astype(BF16)
        carry = jnp.zeros((FOX_HEADS, 1), F32)
        for j in range(nq):
            cum = sum(_dot(t, tri_u) for t in _split3(log_f[:, j * tq:(j + 1) * tq])) + carry
            carry = cum[:, tq - 1:tq]
            for r in range(FOX_HEADS):
                fr_ref[j, r] = cum[r:r + 1, :] * LOG2E

    qs = [(q_ref[:, h * dh:(h + 1) * dh].astype(F32) * (dh ** -0.5 * LOG2E)).astype(BF16) for h in range(hpb)]
    acc_ref[...] = jnp.zeros_like(acc_ref)

    def scores(h, j):
        start = pl.multiple_of(j * tq, tq)
        return _dot_nt(qs[h], k_ref[pl.ds(start, tq), h * dh:(h + 1) * dh]) - fr_ref[j, hp * hpb + h]

    def step(j, ms, slot, last):
        start = pl.multiple_of(j * tq, tq)
        out = []
        for h in range(hpb):
            s = s_ref[slot, h]
            if last:
                s = jnp.where(_tri(tq), s, -jnp.inf)
            else:
                s_ref[1 - slot, h] = scores(h, j + 1)
            m_new = jnp.maximum(ms[h], jnp.max(s, axis=-1, keepdims=True))
            p = jnp.exp2(s - m_new).astype(BF16)
            acc_ref[h] = jnp.exp2(ms[h] - m_new) * acc_ref[h] + _dot(p, va_ref[h, pl.ds(start, tq), :])
            out.append(m_new)
        return tuple(out)

    for h in range(hpb):
        s_ref[0, h] = scores(h, 0)
    m0 = tuple(jnp.full((tq, 1), -jnp.inf, F32) for _ in range(hpb))

    def steps(first, count, ms):
        for u in range(count):
            ms = step(first + u, ms, u % 2, False)
        return ms

    unroll = FOX_UNROLL
    ms = lax.fori_loop(0, qi // unroll, lambda t, ms: steps(unroll * t, unroll, ms), m0)

    for rem in range(unroll):
        @pl.when(qi % unroll == rem)
        def _():
            step(qi, steps(qi - rem, rem, ms), rem % 2, True)
            for h in range(hpb):
                acc = acc_ref[h]
                gate = _sigmoid(og_ref[:, h * dh:(h + 1) * dh].astype(F32))
                o_ref[:, h * dh:(h + 1) * dh] = (acc[:, :dh] / acc[:, dh:] * gate).astype(BF16)


def _fox(proj, small_t, b_f, bsz, seq, dh, q_blk, k_blk, v_blk, og_blk, f_blk):
    n = proj.shape[0]
    tq = min(FOX_TQ, seq)
    nq = seq // tq
    hpb = FOX_HEADS_PER_STEP
    w = hpb * dh
    assert FOX_HEADS % hpb == 0 and q_blk % hpb == 0 and k_blk % hpb == 0 and v_blk % hpb == 0 and og_blk % hpb == 0
    return pl.pallas_call(
        functools.partial(_fox_body, tq=tq, nq=nq, dh=dh, hpb=hpb),
        grid=(bsz, FOX_HEADS // hpb, nq),
        in_specs=[
            pl.BlockSpec((tq, w), lambda b, h, i: (b * nq + i, q_blk // hpb + h)),
            pl.BlockSpec((seq, w), lambda b, h, i: (b, k_blk // hpb + h)),
            pl.BlockSpec((seq, w), lambda b, h, i: (b, v_blk // hpb + h)),
            pl.BlockSpec((tq, w), lambda b, h, i: (b * nq + i, og_blk // hpb + h)),
            pl.BlockSpec((FOX_HEADS, seq), lambda b, h, i: (f_blk, b)),
            pl.BlockSpec((FOX_HEADS, 1), lambda b, h, i: (0, 0)),
        ],
        out_specs=pl.BlockSpec((tq, w), lambda b, h, i: (b * nq + i, h)),
        out_shape=jax.ShapeDtypeStruct((n, FOX_HEADS * dh), BF16),
        scratch_shapes=[pltpu.VMEM((hpb, seq, dh + LANES), BF16), pltpu.VMEM((hpb, tq, dh + LANES), F32),
                        pltpu.VMEM((2, hpb, tq, tq), F32), pltpu.VMEM((nq, FOX_HEADS, 1, tq), F32)],
        compiler_params=_params("parallel", "arbitrary", "arbitrary"),
        name="fox",
    )(proj, proj, proj, proj, small_t, b_f)


def _merge_body(x_ref, y0_ref, y1_ref, y2_ref, gt_ref, bg_ref, wb_ref, wo_ref, o_ref, *, d):
    merged = None
    for n, y_ref in enumerate((y0_ref, y1_ref, y2_ref)):
        cs = slice(n * d, (n + 1) * d)
        gate = _sigmoid(gt_ref[:, cs].astype(F32) + bg_ref[:, cs])
        term = _dot(y_ref[...], wb_ref[n]) * gate
        merged = term if merged is None else merged + term
    o_ref[...] = x_ref[...] + _dot(merged.astype(BF16), wo_ref[...])


def _merge(x2, y_gla, y_mlstm, y_fox, proj, b_gate, w_branch, w_out, gate_blk):
    n, d = x2.shape
    tm = min(MERGE_TM, n)
    rowblk = lambda shape: pl.BlockSpec(shape, lambda i: (i, 0))
    return pl.pallas_call(
        functools.partial(_merge_body, d=d),
        grid=(n // tm,),
        in_specs=[
            rowblk((tm, d)), rowblk((tm, d)), rowblk((tm, d)), rowblk((tm, d)),
            pl.BlockSpec((tm, N_BRANCH * d), lambda i: (i, gate_blk)),
            pl.BlockSpec((1, N_BRANCH * d), lambda i: (0, 0)),
            pl.BlockSpec((N_BRANCH, d, d), lambda i: (0, 0, 0)),
            pl.BlockSpec((d, d), lambda i: (0, 0)),
        ],
        out_specs=rowblk((tm, d)),
        out_shape=jax.ShapeDtypeStruct((n, d), F32),
        compiler_params=_params("parallel"),
        name="merge",
    )(x2, y_gla, y_mlstm, y_fox, proj, b_gate, w_branch, w_out)


def _ffn_body(x_ref, xp_ref, g_ref, wu_ref, cw_ref, cb_ref, wd_ref, gf_ref, o_ref, h_ref,
              *, tm, halo, blocks_per_seq, dff, chunk, final):
    i = pl.program_id(0)

    def norm(x, gain):
        return x * lax.rsqrt(jnp.mean(x * x, axis=-1, keepdims=True) + EPS) * gain

    x = x_ref[...]
    hp = norm(xp_ref[...], g_ref[...]).astype(BF16)
    h_ref[0:halo, :] = jnp.where(i % blocks_per_seq != 0, hp, jnp.zeros_like(hp))
    h_ref[halo:halo + tm, :] = norm(x, g_ref[...]).astype(BF16)
    h = h_ref[...]

    def conv(lo, hi):
        u = _dot(h, wu_ref[:, lo:hi])
        y = cb_ref[:, lo:hi]
        for j in range(FFN_CONV - 1):
            y = y + cw_ref[j:j + 1, lo:hi] * pltpu.roll(u, FFN_CONV - 1 - j, 0)[halo:, :]
        return y + cw_ref[FFN_CONV - 1:FFN_CONV, lo:hi] * u[halo:, :]

    acc = x
    for lo in range(0, dff, chunk):
        hi = min(lo + chunk, dff)
        act = _silu(conv(dff + lo, dff + hi)) * conv(lo, hi)
        acc = acc + _dot(act.astype(BF16), wd_ref[lo:hi, :])
    o_ref[...] = norm(acc, gf_ref[...]) if final else acc


def _ffn(x2, g, w_up, conv_w, conv_b, w_down, seq, final_gain=None):
    n, d = x2.shape
    dff = w_down.shape[0]
    tm = min(FFN_TM, seq)
    halo = BF16_SUBLANES
    hb = tm // halo
    final = final_gain is not None
    resident = lambda shape: pl.BlockSpec(shape, lambda i: (0,) * len(shape), pipeline_mode=pl.Buffered(1))
    return pl.pallas_call(
        functools.partial(_ffn_body, tm=tm, halo=halo, blocks_per_seq=seq // tm, dff=dff, chunk=FFN_CHUNK,
                          final=final),
        grid=(n // tm,),
        in_specs=[
            pl.BlockSpec((tm, d), lambda i: (i, 0)),
            pl.BlockSpec((halo, d), lambda i: (jnp.maximum(i * hb - 1, 0), 0)),
            resident((1, d)),
            resident((d, 2 * dff)),
            resident((FFN_CONV, 2 * dff)),
            resident((1, 2 * dff)),
            resident((dff, d)),
            resident((1, d)),
        ],
        out_specs=pl.BlockSpec((tm, d), lambda i: (i, 0)),
        out_shape=jax.ShapeDtypeStruct((n, d), F32),
        scratch_shapes=[pltpu.VMEM((halo + tm, d), BF16)],
        compiler_params=_params("parallel"),
        name="ffn_final" if final else "ffn",
    )(x2, x2, g, w_up, conv_w, conv_b, w_down, final_gain if final else g)


def _block_diag_heads(w, n_heads):
    nblk, bc, bd = w.shape
    per = nblk // n_heads
    tiled = jnp.tile(w.reshape(n_heads, per * bc, bd), (1, 1, per))
    same_block = (jnp.arange(per * bc)[:, None] // bc) == (jnp.arange(per * bd)[None, :] // bd)
    return jnp.where(same_block, tiled, 0.0)


def _pad_lanes(a, width):
    return jnp.pad(a, ((0, 0),) * (a.ndim - 1) + ((0, width - a.shape[-1]),))


def kernel(x, norm_mix, w_in, b_gate, gla_w_lr, gla_b_lr, gla_norm, mlstm_conv_w, mlstm_conv_b, mlstm_wq, mlstm_wk,
           mlstm_wv, mlstm_w_i, mlstm_b_i, mlstm_w_f, mlstm_b_f, mlstm_skip, mlstm_norm, fox_b_f, w_branch, w_out,
           norm_ffn, ffn_w_up, ffn_conv_w, ffn_conv_b, ffn_w_down, norm_final):
    bsz, seq, d = x.shape
    depth = w_in.shape[0]
    hdk = gla_w_lr.shape[2]
    hdv = gla_norm.shape[1]
    inner = mlstm_conv_w.shape[2]
    fox_w = w_branch.shape[2]
    fox_dh = fox_w // FOX_HEADS
    dff = ffn_w_down.shape[1]
    assert hdv == d and inner == d and fox_w == d and 2 * hdk == d

    o_glr = 2 * hdk + hdv
    o_gr = o_glr + GLA_RANK
    o_ff = o_gr + hdv + 2 * inner + 3 * fox_w
    o_fog = o_ff + FOX_HEADS
    blk = {"gq": 0, "gk": 1, "gv": 1, "gr": 2, "mx": 3, "fq": 5 * d // fox_dh, "fk": 6 * d // fox_dh,
           "fv": 7 * d // fox_dh, "fog": 8 * d // fox_dh, "gates": 3}

    x2 = x.reshape(bsz * seq, d)
    for l in range(depth):
        wl = w_in[l]
        w_big = jnp.concatenate([wl[:, :o_glr], wl[:, o_gr:o_ff], wl[:, o_fog:]], axis=1).astype(BF16)
        w_small = _pad_lanes(jnp.concatenate([wl[:, o_glr:o_gr], wl[:, o_ff:o_fog]], axis=1), SMALL_W).astype(BF16)
        proj, small, small_t = _inproj(x2, norm_mix[l][None, :], w_big, w_small)

        w_lr_pad = jnp.pad(gla_w_lr[l], ((0, SMALL_W - GLA_RANK), (0, 0))).astype(BF16)
        y_gla = _gla(proj, small, w_lr_pad, gla_b_lr[l][None, :], gla_norm[l][None, :], bsz, seq)

        w_if = jnp.concatenate([_pad_lanes(mlstm_w_i[l], LANES), _pad_lanes(mlstm_w_f[l], LANES)], axis=1)
        b_if = jnp.concatenate([_pad_lanes(mlstm_b_i[l][None, :], LANES), _pad_lanes(mlstm_b_f[l][None, :], LANES)], axis=1)
        q_unscale = jnp.array([(inner // MLSTM_HEADS) ** 0.5, 1.0, 1.0], F32)[:, None, None]
        w_if3 = (w_if.reshape(3, inner, 2 * LANES) * q_unscale).astype(BF16)
        y_mlstm = _mlstm(
            proj, mlstm_conv_w[l], mlstm_conv_b[l][None, :],
            _block_diag_heads(mlstm_wq[l], MLSTM_HEADS).astype(BF16),
            _block_diag_heads(mlstm_wk[l], MLSTM_HEADS).astype(BF16),
            _block_diag_heads(mlstm_wv[l], MLSTM_HEADS).astype(BF16),
            w_if3, b_if, mlstm_skip[l][None, :], mlstm_norm[l][None, :], bsz, seq, blk["mx"])

        y_fox = _fox(proj, small_t, fox_b_f[l][:, None], bsz, seq, fox_dh,
                     blk["fq"], blk["fk"], blk["fv"], blk["fog"], GLA_RANK // FOX_HEADS)

        x2 = _merge(x2, y_gla, y_mlstm, y_fox, proj, b_gate[l].reshape(1, N_BRANCH * d),
                    w_branch[l].astype(BF16), w_out[l].astype(BF16), blk["gates"])

        x2 = _ffn(x2, norm_ffn[l][None, :], ffn_w_up[l].astype(BF16), ffn_conv_w[l], ffn_conv_b[l][None, :],
                  ffn_w_down[l].astype(BF16), seq, final_gain=norm_final[None, :] if l == depth - 1 else None)
    return x2.reshape(bsz, seq, d)
```

```python
import functools

import jax
import jax.numpy as jnp
from jax import lax
from jax.experimental import pallas as pl
from jax.experimental.pallas import tpu as pltpu

F32 = jnp.float32
BF16 = jnp.bfloat16
EPS = 1e-6
LOG2E = 1.4426950408889634

LANES = 128
F32_SUBLANES = 8
BF16_SUBLANES = 16
VMEM_LIMIT_BYTES = 56 * 1024 * 1024

GLA_HEADS = 4
GLA_RANK = 16
GLA_TAU = 16.0
GLA_CHUNK = 64
MLSTM_HEADS = 4
MLSTM_CONV = 4
MLSTM_BLOCK = 4
FOX_HEADS = 8
N_BRANCH = 3
FFN_CONV = 3
SMALL_W = LANES

INPROJ_TM = 1024
INPROJ_TN = 2048
GLA_ROWS = 512
MLSTM_CHUNK = 256
MLSTM_ROWS = 512
FOX_TQ = 512
FOX_HEADS_PER_STEP = 2
FOX_UNROLL = 4
MERGE_TM = 512
FFN_TM = 512
FFN_CHUNK = 1024


def _params(*sem):
    return pltpu.CompilerParams(dimension_semantics=sem, vmem_limit_bytes=VMEM_LIMIT_BYTES)


def _log_sigmoid(z, wide=False):
    e = jnp.exp(-jnp.abs(z))
    return jnp.minimum(z, 0.0) - (jnp.log(1.0 + e) if wide else jnp.log1p(e))


def _sigmoid(z):
    return 1.0 / (1.0 + jnp.exp(-z))


def _silu(z):
    return z * _sigmoid(z)


def _tri(n, upper=False):
    r = lax.broadcasted_iota(jnp.int32, (n, n), 0)
    c = lax.broadcasted_iota(jnp.int32, (n, n), 1)
    return (r <= c) if upper else (r >= c)


def _dot(a, b, **kw):
    return jnp.dot(a, b, preferred_element_type=F32, **kw)


def _split3(x):
    hi = x.astype(BF16)
    r1 = x - hi.astype(F32)
    mid = r1.astype(BF16)
    lo = (r1 - mid.astype(F32)).astype(BF16)
    return hi, mid, lo


def _cumsum_rows(tri, x):
    return sum(_dot(tri, t) for t in _split3(x))


def _dot_nt(a, b):
    return lax.dot_general(a, b, (((1,), (1,)), ((), ())), preferred_element_type=F32)


def _dot_tn(a, b):
    return lax.dot_general(a, b, (((0,), (0,)), ((), ())), preferred_element_type=F32)


def _inproj_body(x_ref, g_ref, w_ref, ws_ref, o_ref, os_ref, ot_ref, h_ref):
    @pl.when(pl.program_id(1) == 0)
    def _():
        x = x_ref[...]
        ms = jnp.mean(x * x, axis=-1, keepdims=True)
        h = (x * lax.rsqrt(ms + EPS) * g_ref[...]).astype(BF16)
        h_ref[...] = h
        small = _dot(h, ws_ref[...])
        os_ref[...] = small
        ot_ref[...] = small.T

    o_ref[...] = _dot(h_ref[...], w_ref[...]).astype(BF16)


def _inproj(x2, g, w_big, w_small):
    n, d = x2.shape
    c = w_big.shape[1]
    tm, tn = min(INPROJ_TM, n), min(INPROJ_TN, c)
    return pl.pallas_call(
        _inproj_body,
        grid=(n // tm, c // tn),
        in_specs=[
            pl.BlockSpec((tm, d), lambda i, j: (i, 0)),
            pl.BlockSpec((1, d), lambda i, j: (0, 0)),
            pl.BlockSpec((d, tn), lambda i, j: (0, j)),
            pl.BlockSpec((d, SMALL_W), lambda i, j: (0, 0)),
        ],
        out_specs=[
            pl.BlockSpec((tm, tn), lambda i, j: (i, j)),
            pl.BlockSpec((tm, SMALL_W), lambda i, j: (i, 0)),
            pl.BlockSpec((SMALL_W, tm), lambda i, j: (0, i)),
        ],
        out_shape=[jax.ShapeDtypeStruct((n, c), BF16), jax.ShapeDtypeStruct((n, SMALL_W), F32),
                   jax.ShapeDtypeStruct((SMALL_W, n), F32)],
        scratch_shapes=[pltpu.VMEM((tm, d), BF16)],
        compiler_params=_params("parallel", "arbitrary"),
        name="inproj",
    )(x2, g, w_big, w_small)


def _gla_body(q_ref, k_ref, v_ref, r_ref, s_ref, wlr_ref, blr_ref, gn_ref, o_ref, st_ref, *, chunk, nchunk, dk, dv):
    @pl.when(pl.program_id(1) == 0)
    def _():
        st_ref[...] = jnp.zeros_like(st_ref)

    z = _dot(s_ref[...].astype(BF16), wlr_ref[...]) + blr_ref[...]
    log_a = _log_sigmoid(z, wide=True) * (1.0 / GLA_TAU)
    tri = _tri(chunk).astype(BF16)
    causal = _tri(chunk)
    scale = dk ** -0.5
    gn = gn_ref[...]
    for c in range(nchunk):
        rows = slice(c * chunk, (c + 1) * chunk)
        bc = _cumsum_rows(tri, log_a[rows])
        b_last = bc[chunk - 1:chunk, :]
        q = q_ref[rows, :].astype(F32) * scale
        k = k_ref[rows, :].astype(F32)
        q_in = (q * jnp.exp(bc)).astype(BF16)
        k_in = (k * jnp.exp(-bc)).astype(BF16)
        k_st = (k * jnp.exp(b_last - bc)).astype(BF16)
        decay = jnp.exp(b_last)
        for h in range(GLA_HEADS):
            ks = slice(h * dk, (h + 1) * dk)
            vs = slice(h * dv, (h + 1) * dv)
            qh = q_in[:, ks]
            vh = v_ref[rows, vs]
            att = jnp.where(causal, _dot_nt(qh, k_in[:, ks]), 0.0)
            st = st_ref[h]
            o = _dot(att.astype(BF16), vh) + _dot_nt(qh, st.astype(BF16))
            st_ref[h] = st * decay[:, ks] + _dot_tn(vh, k_st[:, ks])
            on = o * lax.rsqrt(jnp.mean(o * o, axis=-1, keepdims=True) + EPS) * gn[:, vs]
            o_ref[rows, vs] = (on * _silu(r_ref[rows, vs].astype(F32))).astype(BF16)


def _gla(proj, small, w_lr_pad, b_lr, g_norm, bsz, seq):
    n = proj.shape[0]
    hdk = w_lr_pad.shape[1]
    dk = hdk // GLA_HEADS
    hdv = g_norm.shape[1]
    dv = hdv // GLA_HEADS
    rows = min(GLA_ROWS, seq)
    nt = seq // rows
    assert hdv == 2 * hdk
    row = lambda b, t: b * nt + t
    return pl.pallas_call(
        functools.partial(_gla_body, chunk=GLA_CHUNK, nchunk=rows // GLA_CHUNK, dk=dk, dv=dv),
        grid=(bsz, nt),
        in_specs=[
            pl.BlockSpec((rows, hdk), lambda b, t: (row(b, t), 0)),
            pl.BlockSpec((rows, hdk), lambda b, t: (row(b, t), 1)),
            pl.BlockSpec((rows, hdv), lambda b, t: (row(b, t), 1)),
            pl.BlockSpec((rows, hdv), lambda b, t: (row(b, t), 2)),
            pl.BlockSpec((rows, SMALL_W), lambda b, t: (row(b, t), 0)),
            pl.BlockSpec((SMALL_W, hdk), lambda b, t: (0, 0)),
            pl.BlockSpec((1, hdk), lambda b, t: (0, 0)),
            pl.BlockSpec((1, hdv), lambda b, t: (0, 0)),
        ],
        out_specs=pl.BlockSpec((rows, hdv), lambda b, t: (row(b, t), 0)),
        out_shape=jax.ShapeDtypeStruct((n, hdv), BF16),
        scratch_shapes=[pltpu.VMEM((GLA_HEADS, dv, dk), F32)],
        compiler_params=_params("parallel", "arbitrary"),
        name="gla",
    )(proj, proj, proj, proj, small, w_lr_pad, b_lr, g_norm)


def _mlstm_body(xm_ref, z_ref, cw_ref, cb_ref, wq_ref, wk_ref, wv_ref, wif_ref, bif_ref,
                skip_ref, gn_ref, o_ref, xf_ref, q_sc, k_sc, v_sc, h_sc, xc_sc, c_sc, m_sc, *, chunk, nchunk, dh):
    rows_blk = chunk * nchunk
    halo = F32_SUBLANES

    @pl.when(pl.program_id(1) == 0)
    def _():
        xf_ref[0:halo, :] = jnp.zeros((halo, xf_ref.shape[1]), F32)
        c_sc[...] = jnp.zeros_like(c_sc)
        m_sc[...] = jnp.zeros_like(m_sc)

    xf_ref[halo:halo + rows_blk, :] = xm_ref[...].astype(F32)
    for h in range(MLSTM_HEADS):
        hs = slice(h * dh, (h + 1) * dh)
        xf = xf_ref[:, hs]
        conv = cb_ref[:, hs]
        for j in range(MLSTM_CONV - 1):
            conv = conv + cw_ref[j:j + 1, hs] * pltpu.roll(xf, MLSTM_CONV - 1 - j, 0)[halo:, :]
        conv = conv + cw_ref[MLSTM_CONV - 1:MLSTM_CONV, hs] * xf[halo:, :]
        xc = _silu(conv)
        xc_sc[:, hs] = xc
        xcb = xc.astype(BF16)
        q_sc[:, hs] = (_dot(xcb, wq_ref[h]) * dh ** -0.5).astype(BF16)
        k_sc[:, hs] = _dot(xcb, wk_ref[h]).astype(BF16)
        v_sc[:, hs] = _dot(xm_ref[:, hs], wv_ref[h]).astype(BF16)
    xf_ref[0:halo, :] = xf_ref[rows_blk:rows_blk + halo, :]
    qa, ka, va = q_sc[...], k_sc[...], v_sc[...]

    gcol = _dot(qa, wif_ref[0]) + _dot(ka, wif_ref[1]) + _dot(va, wif_ref[2]) + bif_ref[...]
    log_i = gcol[:, :LANES] * LOG2E
    log_f = _log_sigmoid(gcol[:, LANES:]) * LOG2E

    tri = _tri(chunk).astype(BF16)
    causal = _tri(chunk)
    ones_aug = jnp.ones((chunk, LANES), BF16)
    for c in range(nchunk):
        rows = slice(c * chunk, (c + 1) * chunk)
        cum_c = _cumsum_rows(tri, log_f[rows])
        a_c = log_i[rows] - cum_c
        cum_r = cum_c.T
        a_r = a_c.T
        for h in range(MLSTM_HEADS):
            hs = slice(h * dh, (h + 1) * dh)
            b_col, a_col = cum_c[:, h:h + 1], a_c[:, h:h + 1]
            b_row, a_row = cum_r[h:h + 1, :], a_r[h:h + 1, :]
            m_st = m_sc[h][:, 0:1]
            d_log = jnp.where(causal, b_col + a_row, -jnp.inf)
            m_inter = b_col + m_st
            m_t = jnp.maximum(m_inter, jnp.max(d_log, axis=-1, keepdims=True))
            w_intra = jnp.exp2(d_log - m_t)
            w_inter = jnp.exp2(m_inter - m_t)
            qc = q_sc[rows, hs]
            kc = k_sc[rows, hs]
            vc = v_sc[rows, hs]
            s = _dot_nt(qc, kc) * w_intra
            c_aug = c_sc[h]
            q_state = _dot(qc, c_aug.astype(BF16))
            num = _dot(s.astype(BF16), vc) + w_inter * q_state[:, :dh]
            qn = jnp.sum(s, axis=-1, keepdims=True) + w_inter * q_state[:, dh:dh + 1]
            h_sc[rows, hs] = num / jnp.maximum(jnp.abs(qn), jnp.exp2(-m_t))
            g = b_row[:, chunk - 1:chunk]
            m_new = jnp.maximum(g + m_st, g + jnp.max(a_row, axis=-1, keepdims=True))
            wa = jnp.exp2(g + a_col - m_new)
            dec = jnp.exp2(g + m_st - m_new)
            kw = (kc.astype(F32) * wa).astype(BF16)
            v_aug = jnp.concatenate([vc, ones_aug], axis=1)
            c_sc[h] = dec * c_aug + _dot_tn(kw, v_aug)
            m_sc[h] = jnp.broadcast_to(m_new, (1, LANES))

    for h in range(MLSTM_HEADS):
        hs = slice(h * dh, (h + 1) * dh)
        hh = h_sc[:, hs]
        hn = hh * lax.rsqrt(jnp.mean(hh * hh, axis=-1, keepdims=True) + EPS) * gn_ref[:, hs]
        zg = _silu(z_ref[:, hs].astype(F32))
        o_ref[:, hs] = ((hn + skip_ref[:, hs] * xc_sc[:, hs]) * zg).astype(BF16)


def _mlstm(proj, conv_w, conv_b, wq_bd, wk_bd, wv_bd, w_if, b_if, skip, g_norm, bsz, seq, col_blk):
    n = proj.shape[0]
    inner = conv_w.shape[1]
    dh = inner // MLSTM_HEADS
    rows = min(MLSTM_ROWS, seq)
    chunk = min(MLSTM_CHUNK, rows)
    nt = seq // rows
    row = lambda b, t: b * nt + t
    full = lambda shape: pl.BlockSpec(shape, lambda b, t: (0,) * len(shape))
    return pl.pallas_call(
        functools.partial(_mlstm_body, chunk=chunk, nchunk=rows // chunk, dh=dh),
        grid=(bsz, nt),
        in_specs=[
            pl.BlockSpec((rows, inner), lambda b, t: (row(b, t), col_blk)),
            pl.BlockSpec((rows, inner), lambda b, t: (row(b, t), col_blk + 1)),
            full((MLSTM_CONV, inner)),
            full((1, inner)),
            full((MLSTM_HEADS, dh, dh)),
            full((MLSTM_HEADS, dh, dh)),
            full((MLSTM_HEADS, dh, dh)),
            full((3, inner, 2 * LANES)),
            full((1, 2 * LANES)),
            full((1, inner)),
            full((1, inner)),
        ],
        out_specs=pl.BlockSpec((rows, inner), lambda b, t: (row(b, t), 0)),
        out_shape=jax.ShapeDtypeStruct((n, inner), BF16),
        scratch_shapes=[
            pltpu.VMEM((rows + F32_SUBLANES, inner), F32),
            pltpu.VMEM((rows, inner), BF16),
            pltpu.VMEM((rows, inner), BF16),
            pltpu.VMEM((rows, inner), BF16),
            pltpu.VMEM((rows, inner), F32),
            pltpu.VMEM((rows, inner), F32),
            pltpu.VMEM((MLSTM_HEADS, dh, dh + LANES), F32),
            pltpu.VMEM((MLSTM_HEADS, 1, LANES), F32),
        ],
        compiler_params=_params("parallel", "arbitrary"),
        name="mlstm",
    )(proj, proj, conv_w, conv_b, wq_bd, wk_bd, wv_bd, w_if, b_if, skip, g_norm)


def _fox_body(q_ref, qn_ref, k_ref, v_ref, og_ref, ft_ref, bf_ref, o_ref, va_ref, acc_ref, s_ref, fr_ref,
              *, tq, nq, dh, hpb):
    hp = pl.program_id(1)
    qi = pl.program_id(2)

    @pl.when(qi == 0)
    def _():
        for h in range(hpb):
            va_ref[h, :, :dh] = v_ref[:, h * dh:(h + 1) * dh]
            va_ref[h, :, dh:] = jnp.ones((va_ref.shape[1], LANES), BF16)

    @pl.when(jnp.logical_and(qi == 0, hp == 0))
    def _():
        log_f = _log_sigmoid(ft_ref[...] + bf_ref[...])
        tri_u = _tri(tq, upper=True).astype(BF16)
        carry = jnp.zeros((FOX_HEADS, 1), F32)
        for j in range(nq):
            cum = sum(_dot(t, tri_u) for t in _split3(log_f[:, j * tq:(j + 1) * tq])) + carry
            carry = cum[:, tq - 1:tq]
            for r in range(FOX_HEADS):
                fr_ref[j, r] = cum[r:r + 1, :] * LOG2E

    def scaled(ref):
        return [(ref[:, h * dh:(h + 1) * dh].astype(F32) * (dh ** -0.5 * LOG2E)).astype(BF16) for h in range(hpb)]

    qs = scaled(q_ref)
    acc_ref[...] = jnp.zeros_like(acc_ref)
    first_slot = 2

    def scores(q_heads, h, j):
        start = pl.multiple_of(j * tq, tq)
        return _dot_nt(q_heads[h], k_ref[pl.ds(start, tq), h * dh:(h + 1) * dh]) - fr_ref[j, hp * hpb + h]

    def step(j, ms, slot, last):
        start = pl.multiple_of(j * tq, tq)
        out = []
        q_next = scaled(qn_ref) if last else None
        for h in range(hpb):
            s = s_ref[slot, h]
            if last:
                s = jnp.where(_tri(tq), s, -jnp.inf)
                s_ref[first_slot, h] = scores(q_next, h, 0)
            else:
                s_ref[1 if slot == first_slot else 1 - slot, h] = scores(qs, h, j + 1)
            m_new = jnp.maximum(ms[h], jnp.max(s, axis=-1, keepdims=True))
            p = jnp.exp2(s - m_new).astype(BF16)
            acc_ref[h] = jnp.exp2(ms[h] - m_new) * acc_ref[h] + _dot(p, va_ref[h, pl.ds(start, tq), :])
            out.append(m_new)
        return tuple(out)

    def finish():
        for h in range(hpb):
            acc = acc_ref[h]
            gate = _sigmoid(og_ref[:, h * dh:(h + 1) * dh].astype(F32))
            o_ref[:, h * dh:(h + 1) * dh] = (acc[:, :dh] / acc[:, dh:] * gate).astype(BF16)

    m0 = tuple(jnp.full((tq, 1), -jnp.inf, F32) for _ in range(hpb))

    @pl.when(qi == 0)
    def _():
        for h in range(hpb):
            s_ref[0, h] = scores(qs, h, 0)
        step(0, m0, 0, True)
        finish()

    @pl.when(qi > 0)
    def _():
        def steps(first, count, ms):
            for u in range(count):
                ms = step(first + u, ms, (1 + u) % 2, False)
            return ms

        ms = step(0, m0, first_slot, False)
        unroll = FOX_UNROLL
        n_mid = qi - 1
        ms = lax.fori_loop(0, n_mid // unroll, lambda t, ms: steps(1 + unroll * t, unroll, ms), ms)

        for rem in range(unroll):
            @pl.when(n_mid % unroll == rem)
            def _():
                step(qi, steps(qi - rem, rem, ms), (1 + rem) % 2, True)
                finish()


def _fox(proj, small_t, b_f, bsz, seq, dh, q_blk, k_blk, v_blk, og_blk, f_blk):
    n = proj.shape[0]
    tq = min(FOX_TQ, seq)
    nq = seq // tq
    hpb = FOX_HEADS_PER_STEP
    w = hpb * dh
    assert FOX_HEADS % hpb == 0 and q_blk % hpb == 0 and k_blk % hpb == 0 and v_blk % hpb == 0 and og_blk % hpb == 0
    return pl.pallas_call(
        functools.partial(_fox_body, tq=tq, nq=nq, dh=dh, hpb=hpb),
        grid=(bsz, FOX_HEADS // hpb, nq),
        in_specs=[
            pl.BlockSpec((tq, w), lambda b, h, i: (b * nq + i, q_blk // hpb + h)),
            pl.BlockSpec((tq, w), lambda b, h, i: (b * nq + jnp.minimum(i + 1, nq - 1), q_blk // hpb + h)),
            pl.BlockSpec((seq, w), lambda b, h, i: (b, k_blk // hpb + h)),
            pl.BlockSpec((seq, w), lambda b, h, i: (b, v_blk // hpb + h)),
            pl.BlockSpec((tq, w), lambda b, h, i: (b * nq + i, og_blk // hpb + h)),
            pl.BlockSpec((FOX_HEADS, seq), lambda b, h, i: (f_blk, b)),
            pl.BlockSpec((FOX_HEADS, 1), lambda b, h, i: (0, 0)),
        ],
        out_specs=pl.BlockSpec((tq, w), lambda b, h, i: (b * nq + i, h)),
        out_shape=jax.ShapeDtypeStruct((n, FOX_HEADS * dh), BF16),
        scratch_shapes=[pltpu.VMEM((hpb, seq, dh + LANES), BF16), pltpu.VMEM((hpb, tq, dh + LANES), F32),
                        pltpu.VMEM((3, hpb, tq, tq), F32), pltpu.VMEM((nq, FOX_HEADS, 1, tq), F32)],
        compiler_params=_params("parallel", "arbitrary", "arbitrary"),
        name="fox",
    )(proj, proj, proj, proj, proj, small_t, b_f)


def _merge_body(x_ref, y0_ref, y1_ref, y2_ref, gt_ref, bg_ref, wb_ref, wo_ref, o_ref, *, d):
    merged = None
    for n, y_ref in enumerate((y0_ref, y1_ref, y2_ref)):
        cs = slice(n * d, (n + 1) * d)
        gate = _sigmoid(gt_ref[:, cs].astype(F32) + bg_ref[:, cs])
        term = _dot(y_ref[...], wb_ref[n]) * gate
        merged = term if merged is None else merged + term
    o_ref[...] = x_ref[...] + _dot(merged.astype(BF16), wo_ref[...])


def _merge(x2, y_gla, y_mlstm, y_fox, proj, b_gate, w_branch, w_out, gate_blk):
    n, d = x2.shape
    tm = min(MERGE_TM, n)
    rowblk = lambda shape: pl.BlockSpec(shape, lambda i: (i, 0))
    return pl.pallas_call(
        functools.partial(_merge_body, d=d),
        grid=(n // tm,),
        in_specs=[
            rowblk((tm, d)), rowblk((tm, d)), rowblk((tm, d)), rowblk((tm, d)),
            pl.BlockSpec((tm, N_BRANCH * d), lambda i: (i, gate_blk)),
            pl.BlockSpec((1, N_BRANCH * d), lambda i: (0, 0)),
            pl.BlockSpec((N_BRANCH, d, d), lambda i: (0, 0, 0)),
            pl.BlockSpec((d, d), lambda i: (0, 0)),
        ],
        out_specs=rowblk((tm, d)),
        out_shape=jax.ShapeDtypeStruct((n, d), F32),
        compiler_params=_params("parallel"),
        name="merge",
    )(x2, y_gla, y_mlstm, y_fox, proj, b_gate, w_branch, w_out)


def _ffn_body(x_ref, xp_ref, g_ref, wu_ref, cw_ref, cb_ref, wd_ref, gf_ref, o_ref, h_ref,
              *, tm, halo, blocks_per_seq, dff, chunk, final):
    i = pl.program_id(0)

    def norm(x, gain):
        return x * lax.rsqrt(jnp.mean(x * x, axis=-1, keepdims=True) + EPS) * gain

    x = x_ref[...]
    hp = norm(xp_ref[...], g_ref[...]).astype(BF16)
    h_ref[0:halo, :] = jnp.where(i % blocks_per_seq != 0, hp, jnp.zeros_like(hp))
    h_ref[halo:halo + tm, :] = norm(x, g_ref[...]).astype(BF16)
    h = h_ref[...]

    def conv(lo, hi):
        u = _dot(h, wu_ref[:, lo:hi])
        y = cb_ref[:, lo:hi]
        for j in range(FFN_CONV - 1):
            y = y + cw_ref[j:j + 1, lo:hi] * pltpu.roll(u, FFN_CONV - 1 - j, 0)[halo:, :]
        return y + cw_ref[FFN_CONV - 1:FFN_CONV, lo:hi] * u[halo:, :]

    acc = x
    for lo in range(0, dff, chunk):
        hi = min(lo + chunk, dff)
        act = _silu(conv(dff + lo, dff + hi)) * conv(lo, hi)
        acc = acc + _dot(act.astype(BF16), wd_ref[lo:hi, :])
    o_ref[...] = norm(acc, gf_ref[...]) if final else acc


def _ffn(x2, g, w_up, conv_w, conv_b, w_down, seq, final_gain=None):
    n, d = x2.shape
    dff = w_down.shape[0]
    tm = min(FFN_TM, seq)
    halo = BF16_SUBLANES
    hb = tm // halo
    final = final_gain is not None
    resident = lambda shape: pl.BlockSpec(shape, lambda i: (0,) * len(shape), pipeline_mode=pl.Buffered(1))
    return pl.pallas_call(
        functools.partial(_ffn_body, tm=tm, halo=halo, blocks_per_seq=seq // tm, dff=dff, chunk=FFN_CHUNK,
                          final=final),
        grid=(n // tm,),
        in_specs=[
            pl.BlockSpec((tm, d), lambda i: (i, 0)),
            pl.BlockSpec((halo, d), lambda i: (jnp.maximum(i * hb - 1, 0), 0)),
            resident((1, d)),
            resident((d, 2 * dff)),
            resident((FFN_CONV, 2 * dff)),
            resident((1, 2 * dff)),
            resident((dff, d)),
            resident((1, d)),
        ],
        out_specs=pl.BlockSpec((tm, d), lambda i: (i, 0)),
        out_shape=jax.ShapeDtypeStruct((n, d), F32),
        scratch_shapes=[pltpu.VMEM((halo + tm, d), BF16)],
        compiler_params=_params("parallel"),
        name="ffn_final" if final else "ffn",
    )(x2, x2, g, w_up, conv_w, conv_b, w_down, final_gain if final else g)


def _block_diag_heads(w, n_heads):
    nblk, bc, bd = w.shape
    per = nblk // n_heads
    tiled = jnp.tile(w.reshape(n_heads, per * bc, bd), (1, 1, per))
    same_block = (jnp.arange(per * bc)[:, None] // bc) == (jnp.arange(per * bd)[None, :] // bd)
    return jnp.where(same_block, tiled, 0.0)


def _pad_lanes(a, width):
    return jnp.pad(a, ((0, 0),) * (a.ndim - 1) + ((0, width - a.shape[-1]),))


def kernel(x, norm_mix, w_in, b_gate, gla_w_lr, gla_b_lr, gla_norm, mlstm_conv_w, mlstm_conv_b, mlstm_wq, mlstm_wk,
           mlstm_wv, mlstm_w_i, mlstm_b_i, mlstm_w_f, mlstm_b_f, mlstm_skip, mlstm_norm, fox_b_f, w_branch, w_out,
           norm_ffn, ffn_w_up, ffn_conv_w, ffn_conv_b, ffn_w_down, norm_final):
    bsz, seq, d = x.shape
    depth = w_in.shape[0]
    hdk = gla_w_lr.shape[2]
    hdv = gla_norm.shape[1]
    inner = mlstm_conv_w.shape[2]
    fox_w = w_branch.shape[2]
    fox_dh = fox_w // FOX_HEADS
    dff = ffn_w_down.shape[1]
    assert hdv == d and inner == d and fox_w == d and 2 * hdk == d

    o_glr = 2 * hdk + hdv
    o_gr = o_glr + GLA_RANK
    o_ff = o_gr + hdv + 2 * inner + 3 * fox_w
    o_fog = o_ff + FOX_HEADS
    blk = {"gq": 0, "gk": 1, "gv": 1, "gr": 2, "mx": 3, "fq": 5 * d // fox_dh, "fk": 6 * d // fox_dh,
           "fv": 7 * d // fox_dh, "fog": 8 * d // fox_dh, "gates": 3}

    x2 = x.reshape(bsz * seq, d)
    for l in range(depth):
        wl = w_in[l]
        w_big = jnp.concatenate([wl[:, :o_glr], wl[:, o_gr:o_ff], wl[:, o_fog:]], axis=1).astype(BF16)
        w_small = _pad_lanes(jnp.concatenate([wl[:, o_glr:o_gr], wl[:, o_ff:o_fog]], axis=1), SMALL_W).astype(BF16)
        proj, small, small_t = _inproj(x2, norm_mix[l][None, :], w_big, w_small)

        w_lr_pad = jnp.pad(gla_w_lr[l], ((0, SMALL_W - GLA_RANK), (0, 0))).astype(BF16)
        y_gla = _gla(proj, small, w_lr_pad, gla_b_lr[l][None, :], gla_norm[l][None, :], bsz, seq)

        w_if = jnp.concatenate([_pad_lanes(mlstm_w_i[l], LANES), _pad_lanes(mlstm_w_f[l], LANES)], axis=1)
        b_if = jnp.concatenate([_pad_lanes(mlstm_b_i[l][None, :], LANES), _pad_lanes(mlstm_b_f[l][None, :], LANES)], axis=1)
        q_unscale = jnp.array([(inner // MLSTM_HEADS) ** 0.5, 1.0, 1.0], F32)[:, None, None]
        w_if3 = (w_if.reshape(3, inner, 2 * LANES) * q_unscale).astype(BF16)
        y_mlstm = _mlstm(
            proj, mlstm_conv_w[l], mlstm_conv_b[l][None, :],
            _block_diag_heads(mlstm_wq[l], MLSTM_HEADS).astype(BF16),
            _block_diag_heads(mlstm_wk[l], MLSTM_HEADS).astype(BF16),
            _block_diag_heads(mlstm_wv[l], MLSTM_HEADS).astype(BF16),
            w_if3, b_if, mlstm_skip[l][None, :], mlstm_norm[l][None, :], bsz, seq, blk["mx"])

        y_fox = _fox(proj, small_t, fox_b_f[l][:, None], bsz, seq, fox_dh,
                     blk["fq"], blk["fk"], blk["fv"], blk["fog"], GLA_RANK // FOX_HEADS)

        x2 = _merge(x2, y_gla, y_mlstm, y_fox, proj, b_gate[l].reshape(1, N_BRANCH * d),
                    w_branch[l].astype(BF16), w_out[l].astype(BF16), blk["gates"])

        x2 = _ffn(x2, norm_ffn[l][None, :], ffn_w_up[l].astype(BF16), ffn_conv_w[l], ffn_conv_b[l][None, :],
                  ffn_w_down[l].astype(BF16), seq, final_gain=norm_final[None, :] if l == depth - 1 else None)
    return x2.reshape(bsz, seq, d)
```

```python
import functools

import jax
import jax.numpy as jnp
from jax import lax
from jax.experimental import pallas as pl
from jax.experimental.pallas import tpu as pltpu

F32 = jnp.float32
BF16 = jnp.bfloat16
EPS = 1e-6
LOG2E = 1.4426950408889634

LANES = 128
F32_SUBLANES = 8
BF16_SUBLANES = 16
VMEM_LIMIT_BYTES = 56 * 1024 * 1024

GLA_HEADS = 4
GLA_RANK = 16
GLA_TAU = 16.0
GLA_CHUNK = 64
MLSTM_HEADS = 4
MLSTM_CONV = 4
MLSTM_BLOCK = 4
FOX_HEADS = 8
N_BRANCH = 3
FFN_CONV = 3
SMALL_W = LANES

INPROJ_TM = 1024
INPROJ_TN = 2048
GLA_ROWS = 512
MLSTM_CHUNK = 256
MLSTM_ROWS = 512
FOX_TQ = 512
FOX_HEADS_PER_STEP = 2
FOX_UNROLL = 8
MERGE_TM = 512
FFN_TM = 512
FFN_CHUNK = 1024


def _params(*sem):
    return pltpu.CompilerParams(dimension_semantics=sem, vmem_limit_bytes=VMEM_LIMIT_BYTES)


def _log_sigmoid(z, wide=False):
    e = jnp.exp(-jnp.abs(z))
    return jnp.minimum(z, 0.0) - (jnp.log(1.0 + e) if wide else jnp.log1p(e))


def _sigmoid(z):
    return 1.0 / (1.0 + jnp.exp(-z))


def _silu(z):
    return z * _sigmoid(z)


def _tri(n, upper=False):
    r = lax.broadcasted_iota(jnp.int32, (n, n), 0)
    c = lax.broadcasted_iota(jnp.int32, (n, n), 1)
    return (r <= c) if upper else (r >= c)


def _dot(a, b, **kw):
    return jnp.dot(a, b, preferred_element_type=F32, **kw)


def _split3(x):
    hi = x.astype(BF16)
    r1 = x - hi.astype(F32)
    mid = r1.astype(BF16)
    lo = (r1 - mid.astype(F32)).astype(BF16)
    return hi, mid, lo


def _cumsum_rows(tri, x):
    return sum(_dot(tri, t) for t in _split3(x))


def _dot_nt(a, b):
    return lax.dot_general(a, b, (((1,), (1,)), ((), ())), preferred_element_type=F32)


def _dot_tn(a, b):
    return lax.dot_general(a, b, (((0,), (0,)), ((), ())), preferred_element_type=F32)


def _inproj_body(x_ref, g_ref, w_ref, ws_ref, o_ref, os_ref, ot_ref, h_ref):
    @pl.when(pl.program_id(1) == 0)
    def _():
        x = x_ref[...]
        ms = jnp.mean(x * x, axis=-1, keepdims=True)
        h = (x * lax.rsqrt(ms + EPS) * g_ref[...]).astype(BF16)
        h_ref[...] = h
        small = _dot(h, ws_ref[...])
        os_ref[...] = small
        ot_ref[...] = small.T

    o_ref[...] = _dot(h_ref[...], w_ref[...]).astype(BF16)


def _inproj(x2, g, w_big, w_small):
    n, d = x2.shape
    c = w_big.shape[1]
    tm, tn = min(INPROJ_TM, n), min(INPROJ_TN, c)
    return pl.pallas_call(
        _inproj_body,
        grid=(n // tm, c // tn),
        in_specs=[
            pl.BlockSpec((tm, d), lambda i, j: (i, 0)),
            pl.BlockSpec((1, d), lambda i, j: (0, 0)),
            pl.BlockSpec((d, tn), lambda i, j: (0, j)),
            pl.BlockSpec((d, SMALL_W), lambda i, j: (0, 0)),
        ],
        out_specs=[
            pl.BlockSpec((tm, tn), lambda i, j: (i, j)),
            pl.BlockSpec((tm, SMALL_W), lambda i, j: (i, 0)),
            pl.BlockSpec((SMALL_W, tm), lambda i, j: (0, i)),
        ],
        out_shape=[jax.ShapeDtypeStruct((n, c), BF16), jax.ShapeDtypeStruct((n, SMALL_W), F32),
                   jax.ShapeDtypeStruct((SMALL_W, n), F32)],
        scratch_shapes=[pltpu.VMEM((tm, d), BF16)],
        compiler_params=_params("parallel", "arbitrary"),
        name="inproj",
    )(x2, g, w_big, w_small)


def _gla_body(q_ref, k_ref, v_ref, r_ref, s_ref, wlr_ref, blr_ref, gn_ref, o_ref, st_ref, *, chunk, nchunk, dk, dv):
    @pl.when(pl.program_id(1) == 0)
    def _():
        st_ref[...] = jnp.zeros_like(st_ref)

    z = _dot(s_ref[...].astype(BF16), wlr_ref[...]) + blr_ref[...]
    log_a = _log_sigmoid(z, wide=True) * (1.0 / GLA_TAU)
    tri = _tri(chunk).astype(BF16)
    causal = _tri(chunk)
    scale = dk ** -0.5
    gn = gn_ref[...]
    for c in range(nchunk):
        rows = slice(c * chunk, (c + 1) * chunk)
        bc = _cumsum_rows(tri, log_a[rows])
        b_last = bc[chunk - 1:chunk, :]
        q = q_ref[rows, :].astype(F32) * scale
        k = k_ref[rows, :].astype(F32)
        q_in = (q * jnp.exp(bc)).astype(BF16)
        k_in = (k * jnp.exp(-bc)).astype(BF16)
        k_st = (k * jnp.exp(b_last - bc)).astype(BF16)
        decay = jnp.exp(b_last)
        for h in range(GLA_HEADS):
            ks = slice(h * dk, (h + 1) * dk)
            vs = slice(h * dv, (h + 1) * dv)
            qh = q_in[:, ks]
            vh = v_ref[rows, vs]
            att = jnp.where(causal, _dot_nt(qh, k_in[:, ks]), 0.0)
            st = st_ref[h]
            o = _dot(att.astype(BF16), vh) + _dot_nt(qh, st.astype(BF16))
            st_ref[h] = st * decay[:, ks] + _dot_tn(vh, k_st[:, ks])
            on = o * lax.rsqrt(jnp.mean(o * o, axis=-1, keepdims=True) + EPS) * gn[:, vs]
            o_ref[rows, vs] = (on * _silu(r_ref[rows, vs].astype(F32))).astype(BF16)


def _gla(proj, small, w_lr_pad, b_lr, g_norm, bsz, seq):
    n = proj.shape[0]
    hdk = w_lr_pad.shape[1]
    dk = hdk // GLA_HEADS
    hdv = g_norm.shape[1]
    dv = hdv // GLA_HEADS
    rows = min(GLA_ROWS, seq)
    nt = seq // rows
    assert hdv == 2 * hdk
    row = lambda b, t: b * nt + t
    return pl.pallas_call(
        functools.partial(_gla_body, chunk=GLA_CHUNK, nchunk=rows // GLA_CHUNK, dk=dk, dv=dv),
        grid=(bsz, nt),
        in_specs=[
            pl.BlockSpec((rows, hdk), lambda b, t: (row(b, t), 0)),
            pl.BlockSpec((rows, hdk), lambda b, t: (row(b, t), 1)),
            pl.BlockSpec((rows, hdv), lambda b, t: (row(b, t), 1)),
            pl.BlockSpec((rows, hdv), lambda b, t: (row(b, t), 2)),
            pl.BlockSpec((rows, SMALL_W), lambda b, t: (row(b, t), 0)),
            pl.BlockSpec((SMALL_W, hdk), lambda b, t: (0, 0)),
            pl.BlockSpec((1, hdk), lambda b, t: (0, 0)),
            pl.BlockSpec((1, hdv), lambda b, t: (0, 0)),
        ],
        out_specs=pl.BlockSpec((rows, hdv), lambda b, t: (row(b, t), 0)),
        out_shape=jax.ShapeDtypeStruct((n, hdv), BF16),
        scratch_shapes=[pltpu.VMEM((GLA_HEADS, dv, dk), F32)],
        compiler_params=_params("parallel", "arbitrary"),
        name="gla",
    )(proj, proj, proj, proj, small, w_lr_pad, b_lr, g_norm)


def _mlstm_body(xm_ref, z_ref, cw_ref, cb_ref, wq_ref, wk_ref, wv_ref, wif_ref, bif_ref,
                skip_ref, gn_ref, o_ref, xf_ref, q_sc, k_sc, v_sc, h_sc, xc_sc, c_sc, m_sc, *, chunk, nchunk, dh):
    rows_blk = chunk * nchunk
    halo = F32_SUBLANES

    @pl.when(pl.program_id(1) == 0)
    def _():
        xf_ref[0:halo, :] = jnp.zeros((halo, xf_ref.shape[1]), F32)
        c_sc[...] = jnp.zeros_like(c_sc)
        m_sc[...] = jnp.zeros_like(m_sc)

    xf_ref[halo:halo + rows_blk, :] = xm_ref[...].astype(F32)
    for h in range(MLSTM_HEADS):
        hs = slice(h * dh, (h + 1) * dh)
        xf = xf_ref[:, hs]
        conv = cb_ref[:, hs]
        for j in range(MLSTM_CONV - 1):
            conv = conv + cw_ref[j:j + 1, hs] * pltpu.roll(xf, MLSTM_CONV - 1 - j, 0)[halo:, :]
        conv = conv + cw_ref[MLSTM_CONV - 1:MLSTM_CONV, hs] * xf[halo:, :]
        xc = _silu(conv)
        xc_sc[:, hs] = xc
        xcb = xc.astype(BF16)
        q_sc[:, hs] = (_dot(xcb, wq_ref[h]) * dh ** -0.5).astype(BF16)
        k_sc[:, hs] = _dot(xcb, wk_ref[h]).astype(BF16)
        v_sc[:, hs] = _dot(xm_ref[:, hs], wv_ref[h]).astype(BF16)
    xf_ref[0:halo, :] = xf_ref[rows_blk:rows_blk + halo, :]
    qa, ka, va = q_sc[...], k_sc[...], v_sc[...]

    gcol = _dot(qa, wif_ref[0]) + _dot(ka, wif_ref[1]) + _dot(va, wif_ref[2]) + bif_ref[...]
    log_i = gcol[:, :LANES] * LOG2E
    log_f = _log_sigmoid(gcol[:, LANES:]) * LOG2E

    tri = _tri(chunk).astype(BF16)
    causal = _tri(chunk)
    ones_aug = jnp.ones((chunk, LANES), BF16)
    for c in range(nchunk):
        rows = slice(c * chunk, (c + 1) * chunk)
        cum_c = _cumsum_rows(tri, log_f[rows])
        a_c = log_i[rows] - cum_c
        cum_r = cum_c.T
        a_r = a_c.T
        for h in range(MLSTM_HEADS):
            hs = slice(h * dh, (h + 1) * dh)
            b_col, a_col = cum_c[:, h:h + 1], a_c[:, h:h + 1]
            b_row, a_row = cum_r[h:h + 1, :], a_r[h:h + 1, :]
            m_st = m_sc[h][:, 0:1]
            d_log = jnp.where(causal, b_col + a_row, -jnp.inf)
            m_inter = b_col + m_st
            m_t = jnp.maximum(m_inter, jnp.max(d_log, axis=-1, keepdims=True))
            w_intra = jnp.exp2(d_log - m_t)
            w_inter = jnp.exp2(m_inter - m_t)
            qc = q_sc[rows, hs]
            kc = k_sc[rows, hs]
            vc = v_sc[rows, hs]
            s = _dot_nt(qc, kc) * w_intra
            c_aug = c_sc[h]
            q_state = _dot(qc, c_aug.astype(BF16))
            num = _dot(s.astype(BF16), vc) + w_inter * q_state[:, :dh]
            qn = jnp.sum(s, axis=-1, keepdims=True) + w_inter * q_state[:, dh:dh + 1]
            h_sc[rows, hs] = num / jnp.maximum(jnp.abs(qn), jnp.exp2(-m_t))
            g = b_row[:, chunk - 1:chunk]
            m_new = jnp.maximum(g + m_st, g + jnp.max(a_row, axis=-1, keepdims=True))
            wa = jnp.exp2(g + a_col - m_new)
            dec = jnp.exp2(g + m_st - m_new)
            kw = (kc.astype(F32) * wa).astype(BF16)
            v_aug = jnp.concatenate([vc, ones_aug], axis=1)
            c_sc[h] = dec * c_aug + _dot_tn(kw, v_aug)
            m_sc[h] = jnp.broadcast_to(m_new, (1, LANES))

    for h in range(MLSTM_HEADS):
        hs = slice(h * dh, (h + 1) * dh)
        hh = h_sc[:, hs]
        hn = hh * lax.rsqrt(jnp.mean(hh * hh, axis=-1, keepdims=True) + EPS) * gn_ref[:, hs]
        zg = _silu(z_ref[:, hs].astype(F32))
        o_ref[:, hs] = ((hn + skip_ref[:, hs] * xc_sc[:, hs]) * zg).astype(BF16)


def _mlstm(proj, conv_w, conv_b, wq_bd, wk_bd, wv_bd, w_if, b_if, skip, g_norm, bsz, seq, col_blk):
    n = proj.shape[0]
    inner = conv_w.shape[1]
    dh = inner // MLSTM_HEADS
    rows = min(MLSTM_ROWS, seq)
    chunk = min(MLSTM_CHUNK, rows)
    nt = seq // rows
    row = lambda b, t: b * nt + t
    full = lambda shape: pl.BlockSpec(shape, lambda b, t: (0,) * len(shape))
    return pl.pallas_call(
        functools.partial(_mlstm_body, chunk=chunk, nchunk=rows // chunk, dh=dh),
        grid=(bsz, nt),
        in_specs=[
            pl.BlockSpec((rows, inner), lambda b, t: (row(b, t), col_blk)),
            pl.BlockSpec((rows, inner), lambda b, t: (row(b, t), col_blk + 1)),
            full((MLSTM_CONV, inner)),
            full((1, inner)),
            full((MLSTM_HEADS, dh, dh)),
            full((MLSTM_HEADS, dh, dh)),
            full((MLSTM_HEADS, dh, dh)),
            full((3, inner, 2 * LANES)),
            full((1, 2 * LANES)),
            full((1, inner)),
            full((1, inner)),
        ],
        out_specs=pl.BlockSpec((rows, inner), lambda b, t: (row(b, t), 0)),
        out_shape=jax.ShapeDtypeStruct((n, inner), BF16),
        scratch_shapes=[
            pltpu.VMEM((rows + F32_SUBLANES, inner), F32),
            pltpu.VMEM((rows, inner), BF16),
            pltpu.VMEM((rows, inner), BF16),
            pltpu.VMEM((rows, inner), BF16),
            pltpu.VMEM((rows, inner), F32),
            pltpu.VMEM((rows, inner), F32),
            pltpu.VMEM((MLSTM_HEADS, dh, dh + LANES), F32),
            pltpu.VMEM((MLSTM_HEADS, 1, LANES), F32),
        ],
        compiler_params=_params("parallel", "arbitrary"),
        name="mlstm",
    )(proj, proj, conv_w, conv_b, wq_bd, wk_bd, wv_bd, w_if, b_if, skip, g_norm)


def _fox_body(q_ref, qn_ref, k_ref, v_ref, og_ref, ft_ref, bf_ref, o_ref, va_ref, acc_ref, s_ref, fr_ref,
              *, tq, nq, dh, hpb):
    hp = pl.program_id(1)
    qi = pl.program_id(2)

    @pl.when(qi == 0)
    def _():
        for h in range(hpb):
            va_ref[h, :, :dh] = v_ref[:, h * dh:(h + 1) * dh]
            va_ref[h, :, dh:] = jnp.ones((va_ref.shape[1], LANES), BF16)

    @pl.when(jnp.logical_and(qi == 0, hp == 0))
    def _():
        log_f = _log_sigmoid(ft_ref[...] + bf_ref[...])
        tri_u = _tri(tq, upper=True).astype(BF16)
        carry = jnp.zeros((FOX_HEADS, 1), F32)
        for j in range(nq):
            cum = sum(_dot(t, tri_u) for t in _split3(log_f[:, j * tq:(j + 1) * tq])) + carry
            carry = cum[:, tq - 1:tq]
            for r in range(FOX_HEADS):
                fr_ref[j, r] = cum[r:r + 1, :] * LOG2E

    def scaled(ref):
        return [(ref[:, h * dh:(h + 1) * dh].astype(F32) * (dh ** -0.5 * LOG2E)).astype(BF16) for h in range(hpb)]

    qs = scaled(q_ref)
    acc_ref[...] = jnp.zeros_like(acc_ref)
    first_slot = 2

    def scores(q_heads, h, j):
        start = pl.multiple_of(j * tq, tq)
        return _dot_nt(q_heads[h], k_ref[pl.ds(start, tq), h * dh:(h + 1) * dh]) - fr_ref[j, hp * hpb + h]

    def step(j, ms, slot, last):
        start = pl.multiple_of(j * tq, tq)
        out = []
        q_next = scaled(qn_ref) if last else None
        for h in range(hpb):
            s = s_ref[slot, h]
            if last:
                s = jnp.where(_tri(tq), s, -jnp.inf)
                s_ref[first_slot, h] = scores(q_next, h, 0)
            else:
                s_ref[1 if slot == first_slot else 1 - slot, h] = scores(qs, h, j + 1)
            m_new = jnp.maximum(ms[h], jnp.max(s, axis=-1, keepdims=True))
            p = jnp.exp2(s - m_new).astype(BF16)
            acc_ref[h] = jnp.exp2(ms[h] - m_new) * acc_ref[h] + _dot(p, va_ref[h, pl.ds(start, tq), :])
            out.append(m_new)
        return tuple(out)

    def finish():
        for h in range(hpb):
            acc = acc_ref[h]
            gate = _sigmoid(og_ref[:, h * dh:(h + 1) * dh].astype(F32))
            o_ref[:, h * dh:(h + 1) * dh] = (acc[:, :dh] / acc[:, dh:] * gate).astype(BF16)

    m0 = tuple(jnp.full((tq, 1), -jnp.inf, F32) for _ in range(hpb))

    @pl.when(qi == 0)
    def _():
        for h in range(hpb):
            s_ref[0, h] = scores(qs, h, 0)
        step(0, m0, 0, True)
        finish()

    @pl.when(qi > 0)
    def _():
        def steps(first, count, ms):
            for u in range(count):
                ms = step(first + u, ms, (1 + u) % 2, False)
            return ms

        ms = step(0, m0, first_slot, False)
        unroll = FOX_UNROLL
        n_mid = qi - 1
        ms = lax.fori_loop(0, n_mid // unroll, lambda t, ms: steps(1 + unroll * t, unroll, ms), ms)

        for rem in range(unroll):
            @pl.when(n_mid % unroll == rem)
            def _():
                step(qi, steps(qi - rem, rem, ms), (1 + rem) % 2, True)
                finish()


def _fox(proj, small_t, b_f, bsz, seq, dh, q_blk, k_blk, v_blk, og_blk, f_blk):
    n = proj.shape[0]
    tq = min(FOX_TQ, seq)
    nq = seq // tq
    hpb = FOX_HEADS_PER_STEP
    w = hpb * dh
    assert FOX_HEADS % hpb == 0 and q_blk % hpb == 0 and k_blk % hpb == 0 and v_blk % hpb == 0 and og_blk % hpb == 0
    return pl.pallas_call(
        functools.partial(_fox_body, tq=tq, nq=nq, dh=dh, hpb=hpb),
        grid=(bsz, FOX_HEADS // hpb, nq),
        in_specs=[
            pl.BlockSpec((tq, w), lambda b, h, i: (b * nq + i, q_blk // hpb + h)),
            pl.BlockSpec((tq, w), lambda b, h, i: (b * nq + jnp.minimum(i + 1, nq - 1), q_blk // hpb + h)),
            pl.BlockSpec((seq, w), lambda b, h, i: (b, k_blk // hpb + h)),
            pl.BlockSpec((seq, w), lambda b, h, i: (b, v_blk // hpb + h)),
            pl.BlockSpec((tq, w), lambda b, h, i: (b * nq + i, og_blk // hpb + h)),
            pl.BlockSpec((FOX_HEADS, seq), lambda b, h, i: (f_blk, b)),
            pl.BlockSpec((FOX_HEADS, 1), lambda b, h, i: (0, 0)),
        ],
        out_specs=pl.BlockSpec((tq, w), lambda b, h, i: (b * nq + i, h)),
        out_shape=jax.ShapeDtypeStruct((n, FOX_HEADS * dh), BF16),
        scratch_shapes=[pltpu.VMEM((hpb, seq, dh + LANES), BF16), pltpu.VMEM((hpb, tq, dh + LANES), F32),
                        pltpu.VMEM((3, hpb, tq, tq), F32), pltpu.VMEM((nq, FOX_HEADS, 1, tq), F32)],
        compiler_params=_params("parallel", "arbitrary", "arbitrary"),
        name="fox",
    )(proj, proj, proj, proj, proj, small_t, b_f)


def _merge_body(x_ref, y0_ref, y1_ref, y2_ref, gt_ref, bg_ref, wb_ref, wo_ref, o_ref, *, d):
    merged = None
    for n, y_ref in enumerate((y0_ref, y1_ref, y2_ref)):
        cs = slice(n * d, (n + 1) * d)
        gate = _sigmoid(gt_ref[:, cs].astype(F32) + bg_ref[:, cs])
        term = _dot(y_ref[...], wb_ref[n]) * gate
        merged = term if merged is None else merged + term
    o_ref[...] = x_ref[...] + _dot(merged.astype(BF16), wo_ref[...])


def _merge(x2, y_gla, y_mlstm, y_fox, proj, b_gate, w_branch, w_out, gate_blk):
    n, d = x2.shape
    tm = min(MERGE_TM, n)
    rowblk = lambda shape: pl.BlockSpec(shape, lambda i: (i, 0))
    return pl.pallas_call(
        functools.partial(_merge_body, d=d),
        grid=(n // tm,),
        in_specs=[
            rowblk((tm, d)), rowblk((tm, d)), rowblk((tm, d)), rowblk((tm, d)),
            pl.BlockSpec((tm, N_BRANCH * d), lambda i: (i, gate_blk)),
            pl.BlockSpec((1, N_BRANCH * d), lambda i: (0, 0)),
            pl.BlockSpec((N_BRANCH, d, d), lambda i: (0, 0, 0)),
            pl.BlockSpec((d, d), lambda i: (0, 0)),
        ],
        out_specs=rowblk((tm, d)),
        out_shape=jax.ShapeDtypeStruct((n, d), F32),
        compiler_params=_params("parallel"),
        name="merge",
    )(x2, y_gla, y_mlstm, y_fox, proj, b_gate, w_branch, w_out)


def _ffn_body(x_ref, xp_ref, g_ref, wu_ref, cw_ref, cb_ref, wd_ref, gf_ref, o_ref, h_ref,
              *, tm, halo, blocks_per_seq, dff, chunk, final):
    i = pl.program_id(0)

    def norm(x, gain):
        return x * lax.rsqrt(jnp.mean(x * x, axis=-1, keepdims=True) + EPS) * gain

    x = x_ref[...]
    hp = norm(xp_ref[...], g_ref[...]).astype(BF16)
    h_ref[0:halo, :] = jnp.where(i % blocks_per_seq != 0, hp, jnp.zeros_like(hp))
    h_ref[halo:halo + tm, :] = norm(x, g_ref[...]).astype(BF16)
    h = h_ref[...]

    def conv(lo, hi):
        u = _dot(h, wu_ref[:, lo:hi])
        y = cb_ref[:, lo:hi]
        for j in range(FFN_CONV - 1):
            y = y + cw_ref[j:j + 1, lo:hi] * pltpu.roll(u, FFN_CONV - 1 - j, 0)[halo:, :]
        return y + cw_ref[FFN_CONV - 1:FFN_CONV, lo:hi] * u[halo:, :]

    acc = x
    for lo in range(0, dff, chunk):
        hi = min(lo + chunk, dff)
        act = _silu(conv(dff + lo, dff + hi)) * conv(lo, hi)
        acc = acc + _dot(act.astype(BF16), wd_ref[lo:hi, :])
    o_ref[...] = norm(acc, gf_ref[...]) if final else acc


def _ffn(x2, g, w_up, conv_w, conv_b, w_down, seq, final_gain=None):
    n, d = x2.shape
    dff = w_down.shape[0]
    tm = min(FFN_TM, seq)
    halo = BF16_SUBLANES
    hb = tm // halo
    final = final_gain is not None
    resident = lambda shape: pl.BlockSpec(shape, lambda i: (0,) * len(shape), pipeline_mode=pl.Buffered(1))
    return pl.pallas_call(
        functools.partial(_ffn_body, tm=tm, halo=halo, blocks_per_seq=seq // tm, dff=dff, chunk=FFN_CHUNK,
                          final=final),
        grid=(n // tm,),
        in_specs=[
            pl.BlockSpec((tm, d), lambda i: (i, 0)),
            pl.BlockSpec((halo, d), lambda i: (jnp.maximum(i * hb - 1, 0), 0)),
            resident((1, d)),
            resident((d, 2 * dff)),
            resident((FFN_CONV, 2 * dff)),
            resident((1, 2 * dff)),
            resident((dff, d)),
            resident((1, d)),
        ],
        out_specs=pl.BlockSpec((tm, d), lambda i: (i, 0)),
        out_shape=jax.ShapeDtypeStruct((n, d), F32),
        scratch_shapes=[pltpu.VMEM((halo + tm, d), BF16)],
        compiler_params=_params("parallel"),
        name="ffn_final" if final else "ffn",
    )(x2, x2, g, w_up, conv_w, conv_b, w_down, final_gain if final else g)


def _block_diag_heads(w, n_heads):
    nblk, bc, bd = w.shape
    per = nblk // n_heads
    tiled = jnp.tile(w.reshape(n_heads, per * bc, bd), (1, 1, per))
    same_block = (jnp.arange(per * bc)[:, None] // bc) == (jnp.arange(per * bd)[None, :] // bd)
    return jnp.where(same_block, tiled, 0.0)


def _pad_lanes(a, width):
    return jnp.pad(a, ((0, 0),) * (a.ndim - 1) + ((0, width - a.shape[-1]),))


def kernel(x, norm_mix, w_in, b_gate, gla_w_lr, gla_b_lr, gla_norm, mlstm_conv_w, mlstm_conv_b, mlstm_wq, mlstm_wk,
           mlstm_wv, mlstm_w_i, mlstm_b_i, mlstm_w_f, mlstm_b_f, mlstm_skip, mlstm_norm, fox_b_f, w_branch, w_out,
           norm_ffn, ffn_w_up, ffn_conv_w, ffn_conv_b, ffn_w_down, norm_final):
    bsz, seq, d = x.shape
    depth = w_in.shape[0]
    hdk = gla_w_lr.shape[2]
    hdv = gla_norm.shape[1]
    inner = mlstm_conv_w.shape[2]
    fox_w = w_branch.shape[2]
    fox_dh = fox_w // FOX_HEADS
    dff = ffn_w_down.shape[1]
    assert hdv == d and inner == d and fox_w == d and 2 * hdk == d

    o_glr = 2 * hdk + hdv
    o_gr = o_glr + GLA_RANK
    o_ff = o_gr + hdv + 2 * inner + 3 * fox_w
    o_fog = o_ff + FOX_HEADS
    blk = {"gq": 0, "gk": 1, "gv": 1, "gr": 2, "mx": 3, "fq": 5 * d // fox_dh, "fk": 6 * d // fox_dh,
           "fv": 7 * d // fox_dh, "fog": 8 * d // fox_dh, "gates": 3}

    x2 = x.reshape(bsz * seq, d)
    for l in range(depth):
        wl = w_in[l]
        w_big = jnp.concatenate([wl[:, :o_glr], wl[:, o_gr:o_ff], wl[:, o_fog:]], axis=1).astype(BF16)
        w_small = _pad_lanes(jnp.concatenate([wl[:, o_glr:o_gr], wl[:, o_ff:o_fog]], axis=1), SMALL_W).astype(BF16)
        proj, small, small_t = _inproj(x2, norm_mix[l][None, :], w_big, w_small)

        w_lr_pad = jnp.pad(gla_w_lr[l], ((0, SMALL_W - GLA_RANK), (0, 0))).astype(BF16)
        y_gla = _gla(proj, small, w_lr_pad, gla_b_lr[l][None, :], gla_norm[l][None, :], bsz, seq)

        w_if = jnp.concatenate([_pad_lanes(mlstm_w_i[l], LANES), _pad_lanes(mlstm_w_f[l], LANES)], axis=1)
        b_if = jnp.concatenate([_pad_lanes(mlstm_b_i[l][None, :], LANES), _pad_lanes(mlstm_b_f[l][None, :], LANES)], axis=1)
        q_unscale = jnp.array([(inner // MLSTM_HEADS) ** 0.5, 1.0, 1.0], F32)[:, None, None]
        w_if3 = (w_if.reshape(3, inner, 2 * LANES) * q_unscale).astype(BF16)
        y_mlstm = _mlstm(
            proj, mlstm_conv_w[l], mlstm_conv_b[l][None, :],
            _block_diag_heads(mlstm_wq[l], MLSTM_HEADS).astype(BF16),
            _block_diag_heads(mlstm_wk[l], MLSTM_HEADS).astype(BF16),
            _block_diag_heads(mlstm_wv[l], MLSTM_HEADS).astype(BF16),
            w_if3, b_if, mlstm_skip[l][None, :], mlstm_norm[l][None, :], bsz, seq, blk["mx"])

        y_fox = _fox(proj, small_t, fox_b_f[l][:, None], bsz, seq, fox_dh,
                     blk["fq"], blk["fk"], blk["fv"], blk["fog"], GLA_RANK // FOX_HEADS)

        x2 = _merge(x2, y_gla, y_mlstm, y_fox, proj, b_gate[l].reshape(1, N_BRANCH * d),
                    w_branch[l].astype(BF16), w_out[l].astype(BF16), blk["gates"])

        x2 = _ffn(x2, norm_ffn[l][None, :], ffn_w_up[l].astype(BF16), ffn_conv_w[l], ffn_conv_b[l][None, :],
                  ffn_w_down[l].astype(BF16), seq, final_gain=norm_final[None, :] if l == depth - 1 else None)
    return x2.reshape(bsz, seq, d)
```

```python
import functools

import jax
import jax.numpy as jnp
from jax import lax
from jax.experimental import pallas as pl
from jax.experimental.pallas import tpu as pltpu

F32 = jnp.float32
BF16 = jnp.bfloat16
EPS = 1e-6
LOG2E = 1.4426950408889634

LANES = 128
F32_SUBLANES = 8
BF16_SUBLANES = 16
VMEM_LIMIT_BYTES = 56 * 1024 * 1024

GLA_HEADS = 4
GLA_RANK = 16
GLA_TAU = 16.0
GLA_CHUNK = 64
MLSTM_HEADS = 4
MLSTM_CONV = 4
MLSTM_BLOCK = 4
FOX_HEADS = 8
N_BRANCH = 3
FFN_CONV = 3
SMALL_W = LANES

INPROJ_TM = 1024
INPROJ_TN = 2048
GLA_ROWS = 1024
MLSTM_CHUNK = 256
MLSTM_ROWS = 1024
FOX_TQ = 512
FOX_HEADS_PER_STEP = 2
FOX_UNROLL = 8
MERGE_TM = 512
FFN_TM = 512
FFN_CHUNK = 1536


def _params(*sem):
    return pltpu.CompilerParams(dimension_semantics=sem, vmem_limit_bytes=VMEM_LIMIT_BYTES)


def _log_sigmoid(z, wide=False):
    e = jnp.exp(-jnp.abs(z))
    return jnp.minimum(z, 0.0) - (jnp.log(1.0 + e) if wide else jnp.log1p(e))


def _sigmoid(z):
    return 1.0 / (1.0 + jnp.exp(-z))


def _silu(z):
    return z * _sigmoid(z)


def _tri(n, upper=False):
    r = lax.broadcasted_iota(jnp.int32, (n, n), 0)
    c = lax.broadcasted_iota(jnp.int32, (n, n), 1)
    return (r <= c) if upper else (r >= c)


def _dot(a, b, **kw):
    return jnp.dot(a, b, preferred_element_type=F32, **kw)


def _split3(x):
    hi = x.astype(BF16)
    r1 = x - hi.astype(F32)
    mid = r1.astype(BF16)
    lo = (r1 - mid.astype(F32)).astype(BF16)
    return hi, mid, lo


def _cumsum_rows(tri, x):
    return sum(_dot(tri, t) for t in _split3(x))


def _dot_nt(a, b):
    return lax.dot_general(a, b, (((1,), (1,)), ((), ())), preferred_element_type=F32)


def _dot_tn(a, b):
    return lax.dot_general(a, b, (((0,), (0,)), ((), ())), preferred_element_type=F32)


def _inproj_body(x_ref, g_ref, w_ref, ws_ref, o_ref, os_ref, ot_ref, h_ref):
    @pl.when(pl.program_id(1) == 0)
    def _():
        x = x_ref[...]
        ms = jnp.mean(x * x, axis=-1, keepdims=True)
        h = (x * lax.rsqrt(ms + EPS) * g_ref[...]).astype(BF16)
        h_ref[...] = h
        small = _dot(h, ws_ref[...])
        os_ref[...] = small
        ot_ref[...] = small.T

    o_ref[...] = _dot(h_ref[...], w_ref[...]).astype(BF16)


def _inproj(x2, g, w_big, w_small):
    n, d = x2.shape
    c = w_big.shape[1]
    tm, tn = min(INPROJ_TM, n), min(INPROJ_TN, c)
    return pl.pallas_call(
        _inproj_body,
        grid=(n // tm, c // tn),
        in_specs=[
            pl.BlockSpec((tm, d), lambda i, j: (i, 0)),
            pl.BlockSpec((1, d), lambda i, j: (0, 0)),
            pl.BlockSpec((d, tn), lambda i, j: (0, j)),
            pl.BlockSpec((d, SMALL_W), lambda i, j: (0, 0)),
        ],
        out_specs=[
            pl.BlockSpec((tm, tn), lambda i, j: (i, j)),
            pl.BlockSpec((tm, SMALL_W), lambda i, j: (i, 0)),
            pl.BlockSpec((SMALL_W, tm), lambda i, j: (0, i)),
        ],
        out_shape=[jax.ShapeDtypeStruct((n, c), BF16), jax.ShapeDtypeStruct((n, SMALL_W), F32),
                   jax.ShapeDtypeStruct((SMALL_W, n), F32)],
        scratch_shapes=[pltpu.VMEM((tm, d), BF16)],
        compiler_params=_params("parallel", "arbitrary"),
        name="inproj",
    )(x2, g, w_big, w_small)


def _gla_body(q_ref, k_ref, v_ref, r_ref, s_ref, wlr_ref, blr_ref, gn_ref, o_ref, st_ref, *, chunk, nchunk, dk, dv):
    @pl.when(pl.program_id(1) == 0)
    def _():
        st_ref[...] = jnp.zeros_like(st_ref)

    z = _dot(s_ref[...].astype(BF16), wlr_ref[...]) + blr_ref[...]
    log_a = _log_sigmoid(z, wide=True) * (1.0 / GLA_TAU)
    tri = _tri(chunk).astype(BF16)
    causal = _tri(chunk)
    scale = dk ** -0.5
    gn = gn_ref[...]
    for c in range(nchunk):
        rows = slice(c * chunk, (c + 1) * chunk)
        bc = _cumsum_rows(tri, log_a[rows])
        b_last = bc[chunk - 1:chunk, :]
        q = q_ref[rows, :].astype(F32) * scale
        k = k_ref[rows, :].astype(F32)
        q_in = (q * jnp.exp(bc)).astype(BF16)
        k_in = (k * jnp.exp(-bc)).astype(BF16)
        k_st = (k * jnp.exp(b_last - bc)).astype(BF16)
        decay = jnp.exp(b_last)
        for h in range(GLA_HEADS):
            ks = slice(h * dk, (h + 1) * dk)
            vs = slice(h * dv, (h + 1) * dv)
            qh = q_in[:, ks]
            vh = v_ref[rows, vs]
            att = jnp.where(causal, _dot_nt(qh, k_in[:, ks]), 0.0)
            st = st_ref[h]
            o = _dot(att.astype(BF16), vh) + _dot_nt(qh, st.astype(BF16))
            st_ref[h] = st * decay[:, ks] + _dot_tn(vh, k_st[:, ks])
            on = o * lax.rsqrt(jnp.mean(o * o, axis=-1, keepdims=True) + EPS) * gn[:, vs]
            o_ref[rows, vs] = (on * _silu(r_ref[rows, vs].astype(F32))).astype(BF16)


def _gla(proj, small, w_lr_pad, b_lr, g_norm, bsz, seq):
    n = proj.shape[0]
    hdk = w_lr_pad.shape[1]
    dk = hdk // GLA_HEADS
    hdv = g_norm.shape[1]
    dv = hdv // GLA_HEADS
    rows = min(GLA_ROWS, seq)
    nt = seq // rows
    assert hdv == 2 * hdk
    row = lambda b, t: b * nt + t
    return pl.pallas_call(
        functools.partial(_gla_body, chunk=GLA_CHUNK, nchunk=rows // GLA_CHUNK, dk=dk, dv=dv),
        grid=(bsz, nt),
        in_specs=[
            pl.BlockSpec((rows, hdk), lambda b, t: (row(b, t), 0)),
            pl.BlockSpec((rows, hdk), lambda b, t: (row(b, t), 1)),
            pl.BlockSpec((rows, hdv), lambda b, t: (row(b, t), 1)),
            pl.BlockSpec((rows, hdv), lambda b, t: (row(b, t), 2)),
            pl.BlockSpec((rows, SMALL_W), lambda b, t: (row(b, t), 0)),
            pl.BlockSpec((SMALL_W, hdk), lambda b, t: (0, 0)),
            pl.BlockSpec((1, hdk), lambda b, t: (0, 0)),
            pl.BlockSpec((1, hdv), lambda b, t: (0, 0)),
        ],
        out_specs=pl.BlockSpec((rows, hdv), lambda b, t: (row(b, t), 0)),
        out_shape=jax.ShapeDtypeStruct((n, hdv), BF16),
        scratch_shapes=[pltpu.VMEM((GLA_HEADS, dv, dk), F32)],
        compiler_params=_params("parallel", "arbitrary"),
        name="gla",
    )(proj, proj, proj, proj, small, w_lr_pad, b_lr, g_norm)


def _mlstm_body(xm_ref, z_ref, cw_ref, cb_ref, wq_ref, wk_ref, wv_ref, wif_ref, bif_ref,
                skip_ref, gn_ref, o_ref, xf_ref, q_sc, k_sc, v_sc, h_sc, xc_sc, c_sc, m_sc, *, chunk, nchunk, dh):
    rows_blk = chunk * nchunk
    halo = F32_SUBLANES

    @pl.when(pl.program_id(1) == 0)
    def _():
        xf_ref[0:halo, :] = jnp.zeros((halo, xf_ref.shape[1]), F32)
        c_sc[...] = jnp.zeros_like(c_sc)
        m_sc[...] = jnp.zeros_like(m_sc)

    xf_ref[halo:halo + rows_blk, :] = xm_ref[...].astype(F32)
    for h in range(MLSTM_HEADS):
        hs = slice(h * dh, (h + 1) * dh)
        xf = xf_ref[:, hs]
        conv = cb_ref[:, hs]
        for j in range(MLSTM_CONV - 1):
            conv = conv + cw_ref[j:j + 1, hs] * pltpu.roll(xf, MLSTM_CONV - 1 - j, 0)[halo:, :]
        conv = conv + cw_ref[MLSTM_CONV - 1:MLSTM_CONV, hs] * xf[halo:, :]
        xc = _silu(conv)
        xc_sc[:, hs] = xc
        xcb = xc.astype(BF16)
        q_sc[:, hs] = (_dot(xcb, wq_ref[h]) * dh ** -0.5).astype(BF16)
        k_sc[:, hs] = _dot(xcb, wk_ref[h]).astype(BF16)
        v_sc[:, hs] = _dot(xm_ref[:, hs], wv_ref[h]).astype(BF16)
    xf_ref[0:halo, :] = xf_ref[rows_blk:rows_blk + halo, :]
    qa, ka, va = q_sc[...], k_sc[...], v_sc[...]

    gcol = _dot(qa, wif_ref[0]) + _dot(ka, wif_ref[1]) + _dot(va, wif_ref[2]) + bif_ref[...]
    log_i = gcol[:, :LANES] * LOG2E
    log_f = _log_sigmoid(gcol[:, LANES:]) * LOG2E

    tri = _tri(chunk).astype(BF16)
    causal = _tri(chunk)
    ones_aug = jnp.ones((chunk, LANES), BF16)
    for c in range(nchunk):
        rows = slice(c * chunk, (c + 1) * chunk)
        cum_c = _cumsum_rows(tri, log_f[rows])
        a_c = log_i[rows] - cum_c
        cum_r = cum_c.T
        a_r = a_c.T
        for h in range(MLSTM_HEADS):
            hs = slice(h * dh, (h + 1) * dh)
            b_col, a_col = cum_c[:, h:h + 1], a_c[:, h:h + 1]
            b_row, a_row = cum_r[h:h + 1, :], a_r[h:h + 1, :]
            m_st = m_sc[h][:, 0:1]
            d_log = jnp.where(causal, b_col + a_row, -jnp.inf)
            m_inter = b_col + m_st
            m_t = jnp.maximum(m_inter, jnp.max(d_log, axis=-1, keepdims=True))
            w_intra = jnp.exp2(d_log - m_t)
            w_inter = jnp.exp2(m_inter - m_t)
            qc = q_sc[rows, hs]
            kc = k_sc[rows, hs]
            vc = v_sc[rows, hs]
            s = _dot_nt(qc, kc) * w_intra
            c_aug = c_sc[h]
            q_state = _dot(qc, c_aug.astype(BF16))
            num = _dot(s.astype(BF16), vc) + w_inter * q_state[:, :dh]
            qn = jnp.sum(s, axis=-1, keepdims=True) + w_inter * q_state[:, dh:dh + 1]
            h_sc[rows, hs] = num / jnp.maximum(jnp.abs(qn), jnp.exp2(-m_t))
            g = b_row[:, chunk - 1:chunk]
            m_new = jnp.maximum(g + m_st, g + jnp.max(a_row, axis=-1, keepdims=True))
            wa = jnp.exp2(g + a_col - m_new)
            dec = jnp.exp2(g + m_st - m_new)
            kw = (kc.astype(F32) * wa).astype(BF16)
            v_aug = jnp.concatenate([vc, ones_aug], axis=1)
            c_sc[h] = dec * c_aug + _dot_tn(kw, v_aug)
            m_sc[h] = jnp.broadcast_to(m_new, (1, LANES))

    for h in range(MLSTM_HEADS):
        hs = slice(h * dh, (h + 1) * dh)
        hh = h_sc[:, hs]
        hn = hh * lax.rsqrt(jnp.mean(hh * hh, axis=-1, keepdims=True) + EPS) * gn_ref[:, hs]
        zg = _silu(z_ref[:, hs].astype(F32))
        o_ref[:, hs] = ((hn + skip_ref[:, hs] * xc_sc[:, hs]) * zg).astype(BF16)


def _mlstm(proj, conv_w, conv_b, wq_bd, wk_bd, wv_bd, w_if, b_if, skip, g_norm, bsz, seq, col_blk):
    n = proj.shape[0]
    inner = conv_w.shape[1]
    dh = inner // MLSTM_HEADS
    rows = min(MLSTM_ROWS, seq)
    chunk = min(MLSTM_CHUNK, rows)
    nt = seq // rows
    row = lambda b, t: b * nt + t
    full = lambda shape: pl.BlockSpec(shape, lambda b, t: (0,) * len(shape))
    return pl.pallas_call(
        functools.partial(_mlstm_body, chunk=chunk, nchunk=rows // chunk, dh=dh),
        grid=(bsz, nt),
        in_specs=[
            pl.BlockSpec((rows, inner), lambda b, t: (row(b, t), col_blk)),
            pl.BlockSpec((rows, inner), lambda b, t: (row(b, t), col_blk + 1)),
            full((MLSTM_CONV, inner)),
            full((1, inner)),
            full((MLSTM_HEADS, dh, dh)),
            full((MLSTM_HEADS, dh, dh)),
            full((MLSTM_HEADS, dh, dh)),
            full((3, inner, 2 * LANES)),
            full((1, 2 * LANES)),
            full((1, inner)),
            full((1, inner)),
        ],
        out_specs=pl.BlockSpec((rows, inner), lambda b, t: (row(b, t), 0)),
        out_shape=jax.ShapeDtypeStruct((n, inner), BF16),
        scratch_shapes=[
            pltpu.VMEM((rows + F32_SUBLANES, inner), F32),
            pltpu.VMEM((rows, inner), BF16),
            pltpu.VMEM((rows, inner), BF16),
            pltpu.VMEM((rows, inner), BF16),
            pltpu.VMEM((rows, inner), F32),
            pltpu.VMEM((rows, inner), F32),
            pltpu.VMEM((MLSTM_HEADS, dh, dh + LANES), F32),
            pltpu.VMEM((MLSTM_HEADS, 1, LANES), F32),
        ],
        compiler_params=_params("parallel", "arbitrary"),
        name="mlstm",
    )(proj, proj, conv_w, conv_b, wq_bd, wk_bd, wv_bd, w_if, b_if, skip, g_norm)


def _fox_body(q_ref, qn_ref, k_ref, v_ref, og_ref, ft_ref, bf_ref, o_ref, va_ref, acc_ref, s_ref, fr_ref,
              *, tq, nq, dh, hpb):
    hp = pl.program_id(1)
    qi = pl.program_id(2)

    @pl.when(qi == 0)
    def _():
        for h in range(hpb):
            va_ref[h, :, :dh] = v_ref[:, h * dh:(h + 1) * dh]
            va_ref[h, :, dh:] = jnp.ones((va_ref.shape[1], LANES), BF16)

    @pl.when(jnp.logical_and(qi == 0, hp == 0))
    def _():
        log_f = _log_sigmoid(ft_ref[...] + bf_ref[...])
        tri_u = _tri(tq, upper=True).astype(BF16)
        carry = jnp.zeros((FOX_HEADS, 1), F32)
        for j in range(nq):
            cum = sum(_dot(t, tri_u) for t in _split3(log_f[:, j * tq:(j + 1) * tq])) + carry
            carry = cum[:, tq - 1:tq]
            for r in range(FOX_HEADS):
                fr_ref[j, r] = cum[r:r + 1, :] * LOG2E

    def scaled(ref):
        return [(ref[:, h * dh:(h + 1) * dh].astype(F32) * (dh ** -0.5 * LOG2E)).astype(BF16) for h in range(hpb)]

    qs = scaled(q_ref)
    acc_ref[...] = jnp.zeros_like(acc_ref)
    first_slot = 2

    def scores(q_heads, h, j):
        start = pl.multiple_of(j * tq, tq)
        return _dot_nt(q_heads[h], k_ref[pl.ds(start, tq), h * dh:(h + 1) * dh]) - fr_ref[j, hp * hpb + h]

    def step(j, ms, slot, last):
        start = pl.multiple_of(j * tq, tq)
        out = []
        q_next = scaled(qn_ref) if last else None
        for h in range(hpb):
            s = s_ref[slot, h]
            if last:
                s = jnp.where(_tri(tq), s, -jnp.inf)
                s_ref[first_slot, h] = scores(q_next, h, 0)
            else:
                s_ref[1 if slot == first_slot else 1 - slot, h] = scores(qs, h, j + 1)
            m_new = jnp.maximum(ms[h], jnp.max(s, axis=-1, keepdims=True))
            p = jnp.exp2(s - m_new).astype(BF16)
            acc_ref[h] = jnp.exp2(ms[h] - m_new) * acc_ref[h] + _dot(p, va_ref[h, pl.ds(start, tq), :])
            out.append(m_new)
        return tuple(out)

    def finish():
        for h in range(hpb):
            acc = acc_ref[h]
            gate = _sigmoid(og_ref[:, h * dh:(h + 1) * dh].astype(F32))
            o_ref[:, h * dh:(h + 1) * dh] = (acc[:, :dh] / acc[:, dh:] * gate).astype(BF16)

    m0 = tuple(jnp.full((tq, 1), -jnp.inf, F32) for _ in range(hpb))

    @pl.when(qi == 0)
    def _():
        for h in range(hpb):
            s_ref[0, h] = scores(qs, h, 0)
        step(0, m0, 0, True)
        finish()

    @pl.when(qi > 0)
    def _():
        def steps(first, count, ms):
            for u in range(count):
                ms = step(first + u, ms, (1 + u) % 2, False)
            return ms

        ms = step(0, m0, first_slot, False)
        unroll = FOX_UNROLL
        n_mid = qi - 1
        ms = lax.fori_loop(0, n_mid // unroll, lambda t, ms: steps(1 + unroll * t, unroll, ms), ms)

        for rem in range(unroll):
            @pl.when(n_mid % unroll == rem)
            def _():
                step(qi, steps(qi - rem, rem, ms), (1 + rem) % 2, True)
                finish()


def _fox(proj, small_t, b_f, bsz, seq, dh, q_blk, k_blk, v_blk, og_blk, f_blk):
    n = proj.shape[0]
    tq = min(FOX_TQ, seq)
    nq = seq // tq
    hpb = FOX_HEADS_PER_STEP
    w = hpb * dh
    assert FOX_HEADS % hpb == 0 and q_blk % hpb == 0 and k_blk % hpb == 0 and v_blk % hpb == 0 and og_blk % hpb == 0
    return pl.pallas_call(
        functools.partial(_fox_body, tq=tq, nq=nq, dh=dh, hpb=hpb),
        grid=(bsz, FOX_HEADS // hpb, nq),
        in_specs=[
            pl.BlockSpec((tq, w), lambda b, h, i: (b * nq + i, q_blk // hpb + h)),
            pl.BlockSpec((tq, w), lambda b, h, i: (b * nq + jnp.minimum(i + 1, nq - 1), q_blk // hpb + h)),
            pl.BlockSpec((seq, w), lambda b, h, i: (b, k_blk // hpb + h)),
            pl.BlockSpec((seq, w), lambda b, h, i: (b, v_blk // hpb + h)),
            pl.BlockSpec((tq, w), lambda b, h, i: (b * nq + i, og_blk // hpb + h)),
            pl.BlockSpec((FOX_HEADS, seq), lambda b, h, i: (f_blk, b)),
            pl.BlockSpec((FOX_HEADS, 1), lambda b, h, i: (0, 0)),
        ],
        out_specs=pl.BlockSpec((tq, w), lambda b, h, i: (b * nq + i, h)),
        out_shape=jax.ShapeDtypeStruct((n, FOX_HEADS * dh), BF16),
        scratch_shapes=[pltpu.VMEM((hpb, seq, dh + LANES), BF16), pltpu.VMEM((hpb, tq, dh + LANES), F32),
                        pltpu.VMEM((3, hpb, tq, tq), F32), pltpu.VMEM((nq, FOX_HEADS, 1, tq), F32)],
        compiler_params=_params("parallel", "arbitrary", "arbitrary"),
        name="fox",
    )(proj, proj, proj, proj, proj, small_t, b_f)


def _merge_body(x_ref, y0_ref, y1_ref, y2_ref, gt_ref, bg_ref, wb_ref, wo_ref, o_ref, *, d):
    merged = None
    for n, y_ref in enumerate((y0_ref, y1_ref, y2_ref)):
        cs = slice(n * d, (n + 1) * d)
        gate = _sigmoid(gt_ref[:, cs].astype(F32) + bg_ref[:, cs])
        term = _dot(y_ref[...], wb_ref[n]) * gate
        merged = term if merged is None else merged + term
    o_ref[...] = x_ref[...] + _dot(merged.astype(BF16), wo_ref[...])


def _merge(x2, y_gla, y_mlstm, y_fox, proj, b_gate, w_branch, w_out, gate_blk):
    n, d = x2.shape
    tm = min(MERGE_TM, n)
    rowblk = lambda shape: pl.BlockSpec(shape, lambda i: (i, 0))
    return pl.pallas_call(
        functools.partial(_merge_body, d=d),
        grid=(n // tm,),
        in_specs=[
            rowblk((tm, d)), rowblk((tm, d)), rowblk((tm, d)), rowblk((tm, d)),
            pl.BlockSpec((tm, N_BRANCH * d), lambda i: (i, gate_blk)),
            pl.BlockSpec((1, N_BRANCH * d), lambda i: (0, 0)),
            pl.BlockSpec((N_BRANCH, d, d), lambda i: (0, 0, 0)),
            pl.BlockSpec((d, d), lambda i: (0, 0)),
        ],
        out_specs=rowblk((tm, d)),
        out_shape=jax.ShapeDtypeStruct((n, d), F32),
        compiler_params=_params("parallel"),
        name="merge",
    )(x2, y_gla, y_mlstm, y_fox, proj, b_gate, w_branch, w_out)


def _ffn_body(x_ref, xp_ref, g_ref, wu_ref, cw_ref, cb_ref, wd_ref, gf_ref, o_ref, h_ref,
              *, tm, halo, blocks_per_seq, dff, chunk, final):
    i = pl.program_id(0)

    def norm(x, gain):
        return x * lax.rsqrt(jnp.mean(x * x, axis=-1, keepdims=True) + EPS) * gain

    x = x_ref[...]
    hp = norm(xp_ref[...], g_ref[...]).astype(BF16)
    h_ref[0:halo, :] = jnp.where(i % blocks_per_seq != 0, hp, jnp.zeros_like(hp))
    h_ref[halo:halo + tm, :] = norm(x, g_ref[...]).astype(BF16)
    h = h_ref[...]

    def conv(lo, hi):
        u = _dot(h, wu_ref[:, lo:hi])
        y = cb_ref[:, lo:hi]
        for j in range(FFN_CONV - 1):
            y = y + cw_ref[j:j + 1, lo:hi] * pltpu.roll(u, FFN_CONV - 1 - j, 0)[halo:, :]
        return y + cw_ref[FFN_CONV - 1:FFN_CONV, lo:hi] * u[halo:, :]

    acc = x
    for lo in range(0, dff, chunk):
        hi = min(lo + chunk, dff)
        act = _silu(conv(dff + lo, dff + hi)) * conv(lo, hi)
        acc = acc + _dot(act.astype(BF16), wd_ref[lo:hi, :])
    o_ref[...] = norm(acc, gf_ref[...]) if final else acc


def _ffn(x2, g, w_up, conv_w, conv_b, w_down, seq, final_gain=None):
    n, d = x2.shape
    dff = w_down.shape[0]
    tm = min(FFN_TM, seq)
    halo = BF16_SUBLANES
    hb = tm // halo
    final = final_gain is not None
    resident = lambda shape: pl.BlockSpec(shape, lambda i: (0,) * len(shape), pipeline_mode=pl.Buffered(1))
    return pl.pallas_call(
        functools.partial(_ffn_body, tm=tm, halo=halo, blocks_per_seq=seq // tm, dff=dff, chunk=FFN_CHUNK,
                          final=final),
        grid=(n // tm,),
        in_specs=[
            pl.BlockSpec((tm, d), lambda i: (i, 0)),
            pl.BlockSpec((halo, d), lambda i: (jnp.maximum(i * hb - 1, 0), 0)),
            resident((1, d)),
            resident((d, 2 * dff)),
            resident((FFN_CONV, 2 * dff)),
            resident((1, 2 * dff)),
            resident((dff, d)),
            resident((1, d)),
        ],
        out_specs=pl.BlockSpec((tm, d), lambda i: (i, 0)),
        out_shape=jax.ShapeDtypeStruct((n, d), F32),
        scratch_shapes=[pltpu.VMEM((halo + tm, d), BF16)],
        compiler_params=_params("parallel"),
        name="ffn_final" if final else "ffn",
    )(x2, x2, g, w_up, conv_w, conv_b, w_down, final_gain if final else g)


def _block_diag_heads(w, n_heads):
    nblk, bc, bd = w.shape
    per = nblk // n_heads
    tiled = jnp.tile(w.reshape(n_heads, per * bc, bd), (1, 1, per))
    same_block = (jnp.arange(per * bc)[:, None] // bc) == (jnp.arange(per * bd)[None, :] // bd)
    return jnp.where(same_block, tiled, 0.0)


def _pad_lanes(a, width):
    return jnp.pad(a, ((0, 0),) * (a.ndim - 1) + ((0, width - a.shape[-1]),))


def kernel(x, norm_mix, w_in, b_gate, gla_w_lr, gla_b_lr, gla_norm, mlstm_conv_w, mlstm_conv_b, mlstm_wq, mlstm_wk,
           mlstm_wv, mlstm_w_i, mlstm_b_i, mlstm_w_f, mlstm_b_f, mlstm_skip, mlstm_norm, fox_b_f, w_branch, w_out,
           norm_ffn, ffn_w_up, ffn_conv_w, ffn_conv_b, ffn_w_down, norm_final):
    bsz, seq, d = x.shape
    depth = w_in.shape[0]
    hdk = gla_w_lr.shape[2]
    hdv = gla_norm.shape[1]
    inner = mlstm_conv_w.shape[2]
    fox_w = w_branch.shape[2]
    fox_dh = fox_w // FOX_HEADS
    dff = ffn_w_down.shape[1]
    assert hdv == d and inner == d and fox_w == d and 2 * hdk == d

    o_glr = 2 * hdk + hdv
    o_gr = o_glr + GLA_RANK
    o_ff = o_gr + hdv + 2 * inner + 3 * fox_w
    o_fog = o_ff + FOX_HEADS
    blk = {"gq": 0, "gk": 1, "gv": 1, "gr": 2, "mx": 3, "fq": 5 * d // fox_dh, "fk": 6 * d // fox_dh,
           "fv": 7 * d // fox_dh, "fog": 8 * d // fox_dh, "gates": 3}

    x2 = x.reshape(bsz * seq, d)
    for l in range(depth):
        wl = w_in[l]
        w_big = jnp.concatenate([wl[:, :o_glr], wl[:, o_gr:o_ff], wl[:, o_fog:]], axis=1).astype(BF16)
        w_small = _pad_lanes(jnp.concatenate([wl[:, o_glr:o_gr], wl[:, o_ff:o_fog]], axis=1), SMALL_W).astype(BF16)
        proj, small, small_t = _inproj(x2, norm_mix[l][None, :], w_big, w_small)

        w_lr_pad = jnp.pad(gla_w_lr[l], ((0, SMALL_W - GLA_RANK), (0, 0))).astype(BF16)
        y_gla = _gla(proj, small, w_lr_pad, gla_b_lr[l][None, :], gla_norm[l][None, :], bsz, seq)

        w_if = jnp.concatenate([_pad_lanes(mlstm_w_i[l], LANES), _pad_lanes(mlstm_w_f[l], LANES)], axis=1)
        b_if = jnp.concatenate([_pad_lanes(mlstm_b_i[l][None, :], LANES), _pad_lanes(mlstm_b_f[l][None, :], LANES)], axis=1)
        q_unscale = jnp.array([(inner // MLSTM_HEADS) ** 0.5, 1.0, 1.0], F32)[:, None, None]
        w_if3 = (w_if.reshape(3, inner, 2 * LANES) * q_unscale).astype(BF16)
        y_mlstm = _mlstm(
            proj, mlstm_conv_w[l], mlstm_conv_b[l][None, :],
            _block_diag_heads(mlstm_wq[l], MLSTM_HEADS).astype(BF16),
            _block_diag_heads(mlstm_wk[l], MLSTM_HEADS).astype(BF16),
            _block_diag_heads(mlstm_wv[l], MLSTM_HEADS).astype(BF16),
            w_if3, b_if, mlstm_skip[l][None, :], mlstm_norm[l][None, :], bsz, seq, blk["mx"])

        y_fox = _fox(proj, small_t, fox_b_f[l][:, None], bsz, seq, fox_dh,
                     blk["fq"], blk["fk"], blk["fv"], blk["fog"], GLA_RANK // FOX_HEADS)

        x2 = _merge(x2, y_gla, y_mlstm, y_fox, proj, b_gate[l].reshape(1, N_BRANCH * d),
                    w_branch[l].astype(BF16), w_out[l].astype(BF16), blk["gates"])

        x2 = _ffn(x2, norm_ffn[l][None, :], ffn_w_up[l].astype(BF16), ffn_conv_w[l], ffn_conv_b[l][None, :],
                  ffn_w_down[l].astype(BF16), seq, final_gain=norm_final[None, :] if l == depth - 1 else None)
    return x2.reshape(bsz, seq, d)
```

```python
import functools

import jax
import jax.numpy as jnp
from jax import lax
from jax.experimental import pallas as pl
from jax.experimental.pallas import tpu as pltpu

F32 = jnp.float32
BF16 = jnp.bfloat16
EPS = 1e-6
LOG2E = 1.4426950408889634

LANES = 128
F32_SUBLANES = 8
BF16_SUBLANES = 16
VMEM_LIMIT_BYTES = 56 * 1024 * 1024

GLA_HEADS = 4
GLA_RANK = 16
GLA_TAU = 16.0
GLA_CHUNK = 64
MLSTM_HEADS = 4
MLSTM_CONV = 4
MLSTM_BLOCK = 4
FOX_HEADS = 8
N_BRANCH = 3
FFN_CONV = 3
SMALL_W = LANES

INPROJ_TM = 1024
INPROJ_TN = 4096
GLA_ROWS = 1024
MLSTM_CHUNK = 256
MLSTM_ROWS = 1024
FOX_TQ = 512
FOX_HEADS_PER_STEP = 2
FOX_UNROLL = 8
MERGE_TM = 512
FFN_TM = 512
FFN_CHUNK = 1536


def _params(*sem):
    return pltpu.CompilerParams(dimension_semantics=sem, vmem_limit_bytes=VMEM_LIMIT_BYTES)


def _log_sigmoid(z, wide=False):
    e = jnp.exp(-jnp.abs(z))
    return jnp.minimum(z, 0.0) - (jnp.log(1.0 + e) if wide else jnp.log1p(e))


def _sigmoid(z):
    return 1.0 / (1.0 + jnp.exp(-z))


def _silu(z):
    return z * _sigmoid(z)


def _tri(n, upper=False):
    r = lax.broadcasted_iota(jnp.int32, (n, n), 0)
    c = lax.broadcasted_iota(jnp.int32, (n, n), 1)
    return (r <= c) if upper else (r >= c)


def _dot(a, b, **kw):
    return jnp.dot(a, b, preferred_element_type=F32, **kw)


def _split3(x):
    hi = x.astype(BF16)
    r1 = x - hi.astype(F32)
    mid = r1.astype(BF16)
    lo = (r1 - mid.astype(F32)).astype(BF16)
    return hi, mid, lo


def _cumsum_rows(tri, x):
    return sum(_dot(tri, t) for t in _split3(x))


def _dot_nt(a, b):
    return lax.dot_general(a, b, (((1,), (1,)), ((), ())), preferred_element_type=F32)


def _dot_tn(a, b):
    return lax.dot_general(a, b, (((0,), (0,)), ((), ())), preferred_element_type=F32)


def _inproj_body(x_ref, g_ref, w_ref, ws_ref, o_ref, os_ref, ot_ref, h_ref):
    @pl.when(pl.program_id(1) == 0)
    def _():
        x = x_ref[...]
        ms = jnp.mean(x * x, axis=-1, keepdims=True)
        h = (x * lax.rsqrt(ms + EPS) * g_ref[...]).astype(BF16)
        h_ref[...] = h
        small = _dot(h, ws_ref[...])
        os_ref[...] = small
        ot_ref[...] = small.T

    o_ref[...] = _dot(h_ref[...], w_ref[...]).astype(BF16)


def _inproj(x2, g, w_big, w_small):
    n, d = x2.shape
    c = w_big.shape[1]
    tm, tn = min(INPROJ_TM, n), min(INPROJ_TN, c)
    return pl.pallas_call(
        _inproj_body,
        grid=(n // tm, c // tn),
        in_specs=[
            pl.BlockSpec((tm, d), lambda i, j: (i, 0)),
            pl.BlockSpec((1, d), lambda i, j: (0, 0)),
            pl.BlockSpec((d, tn), lambda i, j: (0, j)),
            pl.BlockSpec((d, SMALL_W), lambda i, j: (0, 0)),
        ],
        out_specs=[
            pl.BlockSpec((tm, tn), lambda i, j: (i, j)),
            pl.BlockSpec((tm, SMALL_W), lambda i, j: (i, 0)),
            pl.BlockSpec((SMALL_W, tm), lambda i, j: (0, i)),
        ],
        out_shape=[jax.ShapeDtypeStruct((n, c), BF16), jax.ShapeDtypeStruct((n, SMALL_W), F32),
                   jax.ShapeDtypeStruct((SMALL_W, n), F32)],
        scratch_shapes=[pltpu.VMEM((tm, d), BF16)],
        compiler_params=_params("parallel", "arbitrary"),
        name="inproj",
    )(x2, g, w_big, w_small)


def _gla_body(q_ref, k_ref, v_ref, r_ref, s_ref, wlr_ref, blr_ref, gn_ref, o_ref, st_ref, *, chunk, nchunk, dk, dv):
    @pl.when(pl.program_id(1) == 0)
    def _():
        st_ref[...] = jnp.zeros_like(st_ref)

    z = _dot(s_ref[...].astype(BF16), wlr_ref[...]) + blr_ref[...]
    log_a = _log_sigmoid(z, wide=True) * (1.0 / GLA_TAU)
    tri = _tri(chunk).astype(BF16)
    causal = _tri(chunk)
    scale = dk ** -0.5
    gn = gn_ref[...]
    for c in range(nchunk):
        rows = slice(c * chunk, (c + 1) * chunk)
        bc = _cumsum_rows(tri, log_a[rows])
        b_last = bc[chunk - 1:chunk, :]
        q = q_ref[rows, :].astype(F32) * scale
        k = k_ref[rows, :].astype(F32)
        q_in = (q * jnp.exp(bc)).astype(BF16)
        k_in = (k * jnp.exp(-bc)).astype(BF16)
        k_st = (k * jnp.exp(b_last - bc)).astype(BF16)
        decay = jnp.exp(b_last)
        for h in range(GLA_HEADS):
            ks = slice(h * dk, (h + 1) * dk)
            vs = slice(h * dv, (h + 1) * dv)
            qh = q_in[:, ks]
            vh = v_ref[rows, vs]
            att = jnp.where(causal, _dot_nt(qh, k_in[:, ks]), 0.0)
            st = st_ref[h]
            o = _dot(att.astype(BF16), vh) + _dot_nt(qh, st.astype(BF16))
            st_ref[h] = st * decay[:, ks] + _dot_tn(vh, k_st[:, ks])
            on = o * lax.rsqrt(jnp.mean(o * o, axis=-1, keepdims=True) + EPS) * gn[:, vs]
            o_ref[rows, vs] = (on * _silu(r_ref[rows, vs].astype(F32))).astype(BF16)


def _gla(proj, small, w_lr_pad, b_lr, g_norm, bsz, seq):
    n = proj.shape[0]
    hdk = w_lr_pad.shape[1]
    dk = hdk // GLA_HEADS
    hdv = g_norm.shape[1]
    dv = hdv // GLA_HEADS
    rows = min(GLA_ROWS, seq)
    nt = seq // rows
    assert hdv == 2 * hdk
    row = lambda b, t: b * nt + t
    return pl.pallas_call(
        functools.partial(_gla_body, chunk=GLA_CHUNK, nchunk=rows // GLA_CHUNK, dk=dk, dv=dv),
        grid=(bsz, nt),
        in_specs=[
            pl.BlockSpec((rows, hdk), lambda b, t: (row(b, t), 0)),
            pl.BlockSpec((rows, hdk), lambda b, t: (row(b, t), 1)),
            pl.BlockSpec((rows, hdv), lambda b, t: (row(b, t), 1)),
            pl.BlockSpec((rows, hdv), lambda b, t: (row(b, t), 2)),
            pl.BlockSpec((rows, SMALL_W), lambda b, t: (row(b, t), 0)),
            pl.BlockSpec((SMALL_W, hdk), lambda b, t: (0, 0)),
            pl.BlockSpec((1, hdk), lambda b, t: (0, 0)),
            pl.BlockSpec((1, hdv), lambda b, t: (0, 0)),
        ],
        out_specs=pl.BlockSpec((rows, hdv), lambda b, t: (row(b, t), 0)),
        out_shape=jax.ShapeDtypeStruct((n, hdv), BF16),
        scratch_shapes=[pltpu.VMEM((GLA_HEADS, dv, dk), F32)],
        compiler_params=_params("parallel", "arbitrary"),
        name="gla",
    )(proj, proj, proj, proj, small, w_lr_pad, b_lr, g_norm)


def _mlstm_body(xm_ref, z_ref, cw_ref, cb_ref, wq_ref, wk_ref, wv_ref, wif_ref, bif_ref,
                skip_ref, gn_ref, o_ref, xf_ref, q_sc, k_sc, v_sc, h_sc, xc_sc, c_sc, m_sc, *, chunk, nchunk, dh):
    rows_blk = chunk * nchunk
    halo = F32_SUBLANES

    @pl.when(pl.program_id(1) == 0)
    def _():
        xf_ref[0:halo, :] = jnp.zeros((halo, xf_ref.shape[1]), F32)
        c_sc[...] = jnp.zeros_like(c_sc)
        m_sc[...] = jnp.zeros_like(m_sc)

    xf_ref[halo:halo + rows_blk, :] = xm_ref[...].astype(F32)
    for h in range(MLSTM_HEADS):
        hs = slice(h * dh, (h + 1) * dh)
        xf = xf_ref[:, hs]
        conv = cb_ref[:, hs]
        for j in range(MLSTM_CONV - 1):
            conv = conv + cw_ref[j:j + 1, hs] * pltpu.roll(xf, MLSTM_CONV - 1 - j, 0)[halo:, :]
        conv = conv + cw_ref[MLSTM_CONV - 1:MLSTM_CONV, hs] * xf[halo:, :]
        xc = _silu(conv)
        xc_sc[:, hs] = xc
        xcb = xc.astype(BF16)
        q_sc[:, hs] = (_dot(xcb, wq_ref[h]) * dh ** -0.5).astype(BF16)
        k_sc[:, hs] = _dot(xcb, wk_ref[h]).astype(BF16)
        v_sc[:, hs] = _dot(xm_ref[:, hs], wv_ref[h]).astype(BF16)
    xf_ref[0:halo, :] = xf_ref[rows_blk:rows_blk + halo, :]
    qa, ka, va = q_sc[...], k_sc[...], v_sc[...]

    gcol = _dot(qa, wif_ref[0]) + _dot(ka, wif_ref[1]) + _dot(va, wif_ref[2]) + bif_ref[...]
    log_i = gcol[:, :LANES] * LOG2E
    log_f = _log_sigmoid(gcol[:, LANES:]) * LOG2E

    tri = _tri(chunk).astype(BF16)
    causal = _tri(chunk)
    ones_aug = jnp.ones((chunk, LANES), BF16)
    for c in range(nchunk):
        rows = slice(c * chunk, (c + 1) * chunk)
        cum_c = _cumsum_rows(tri, log_f[rows])
        a_c = log_i[rows] - cum_c
        cum_r = cum_c.T
        a_r = a_c.T
        for h in range(MLSTM_HEADS):
            hs = slice(h * dh, (h + 1) * dh)
            b_col, a_col = cum_c[:, h:h + 1], a_c[:, h:h + 1]
            b_row, a_row = cum_r[h:h + 1, :], a_r[h:h + 1, :]
            m_st = m_sc[h][:, 0:1]
            d_log = jnp.where(causal, b_col + a_row, -jnp.inf)
            m_inter = b_col + m_st
            m_t = jnp.maximum(m_inter, jnp.max(d_log, axis=-1, keepdims=True))
            w_intra = jnp.exp2(d_log - m_t)
            w_inter = jnp.exp2(m_inter - m_t)
            qc = q_sc[rows, hs]
            kc = k_sc[rows, hs]
            vc = v_sc[rows, hs]
            s = _dot_nt(qc, kc) * w_intra
            c_aug = c_sc[h]
            q_state = _dot(qc, c_aug.astype(BF16))
            num = _dot(s.astype(BF16), vc) + w_inter * q_state[:, :dh]
            qn = jnp.sum(s, axis=-1, keepdims=True) + w_inter * q_state[:, dh:dh + 1]
            h_sc[rows, hs] = num / jnp.maximum(jnp.abs(qn), jnp.exp2(-m_t))
            g = b_row[:, chunk - 1:chunk]
            m_new = jnp.maximum(g + m_st, g + jnp.max(a_row, axis=-1, keepdims=True))
            wa = jnp.exp2(g + a_col - m_new)
            dec = jnp.exp2(g + m_st - m_new)
            kw = (kc.astype(F32) * wa).astype(BF16)
            v_aug = jnp.concatenate([vc, ones_aug], axis=1)
            c_sc[h] = dec * c_aug + _dot_tn(kw, v_aug)
            m_sc[h] = jnp.broadcast_to(m_new, (1, LANES))

    for h in range(MLSTM_HEADS):
        hs = slice(h * dh, (h + 1) * dh)
        hh = h_sc[:, hs]
        hn = hh * lax.rsqrt(jnp.mean(hh * hh, axis=-1, keepdims=True) + EPS) * gn_ref[:, hs]
        zg = _silu(z_ref[:, hs].astype(F32))
        o_ref[:, hs] = ((hn + skip_ref[:, hs] * xc_sc[:, hs]) * zg).astype(BF16)


def _mlstm(proj, conv_w, conv_b, wq_bd, wk_bd, wv_bd, w_if, b_if, skip, g_norm, bsz, seq, col_blk):
    n = proj.shape[0]
    inner = conv_w.shape[1]
    dh = inner // MLSTM_HEADS
    rows = min(MLSTM_ROWS, seq)
    chunk = min(MLSTM_CHUNK, rows)
    nt = seq // rows
    row = lambda b, t: b * nt + t
    full = lambda shape: pl.BlockSpec(shape, lambda b, t: (0,) * len(shape))
    return pl.pallas_call(
        functools.partial(_mlstm_body, chunk=chunk, nchunk=rows // chunk, dh=dh),
        grid=(bsz, nt),
        in_specs=[
            pl.BlockSpec((rows, inner), lambda b, t: (row(b, t), col_blk)),
            pl.BlockSpec((rows, inner), lambda b, t: (row(b, t), col_blk + 1)),
            full((MLSTM_CONV, inner)),
            full((1, inner)),
            full((MLSTM_HEADS, dh, dh)),
            full((MLSTM_HEADS, dh, dh)),
            full((MLSTM_HEADS, dh, dh)),
            full((3, inner, 2 * LANES)),
            full((1, 2 * LANES)),
            full((1, inner)),
            full((1, inner)),
        ],
        out_specs=pl.BlockSpec((rows, inner), lambda b, t: (row(b, t), 0)),
        out_shape=jax.ShapeDtypeStruct((n, inner), BF16),
        scratch_shapes=[
            pltpu.VMEM((rows + F32_SUBLANES, inner), F32),
            pltpu.VMEM((rows, inner), BF16),
            pltpu.VMEM((rows, inner), BF16),
            pltpu.VMEM((rows, inner), BF16),
            pltpu.VMEM((rows, inner), F32),
            pltpu.VMEM((rows, inner), F32),
            pltpu.VMEM((MLSTM_HEADS, dh, dh + LANES), F32),
            pltpu.VMEM((MLSTM_HEADS, 1, LANES), F32),
        ],
        compiler_params=_params("parallel", "arbitrary"),
        name="mlstm",
    )(proj, proj, conv_w, conv_b, wq_bd, wk_bd, wv_bd, w_if, b_if, skip, g_norm)


def _fox_body(q_ref, qn_ref, k_ref, v_ref, og_ref, ft_ref, bf_ref, o_ref, va_ref, acc_ref, s_ref, fr_ref,
              *, tq, nq, dh, hpb):
    hp = pl.program_id(1)
    qi = pl.program_id(2)

    @pl.when(qi == 0)
    def _():
        for h in range(hpb):
            va_ref[h, :, :dh] = v_ref[:, h * dh:(h + 1) * dh]
            va_ref[h, :, dh:] = jnp.ones((va_ref.shape[1], LANES), BF16)

    @pl.when(jnp.logical_and(qi == 0, hp == 0))
    def _():
        log_f = _log_sigmoid(ft_ref[...] + bf_ref[...])
        tri_u = _tri(tq, upper=True).astype(BF16)
        carry = jnp.zeros((FOX_HEADS, 1), F32)
        for j in range(nq):
            cum = sum(_dot(t, tri_u) for t in _split3(log_f[:, j * tq:(j + 1) * tq])) + carry
            carry = cum[:, tq - 1:tq]
            for r in range(FOX_HEADS):
                fr_ref[j, r] = cum[r:r + 1, :] * LOG2E

    def scaled(ref):
        return [(ref[:, h * dh:(h + 1) * dh].astype(F32) * (dh ** -0.5 * LOG2E)).astype(BF16) for h in range(hpb)]

    qs = scaled(q_ref)
    acc_ref[...] = jnp.zeros_like(acc_ref)
    first_slot = 2

    def scores(q_heads, h, j):
        start = pl.multiple_of(j * tq, tq)
        return _dot_nt(q_heads[h], k_ref[pl.ds(start, tq), h * dh:(h + 1) * dh]) - fr_ref[j, hp * hpb + h]

    def step(j, ms, slot, last):
        start = pl.multiple_of(j * tq, tq)
        out = []
        q_next = scaled(qn_ref) if last else None
        for h in range(hpb):
            s = s_ref[slot, h]
            if last:
                s = jnp.where(_tri(tq), s, -jnp.inf)
                s_ref[first_slot, h] = scores(q_next, h, 0)
            else:
                s_ref[1 if slot == first_slot else 1 - slot, h] = scores(qs, h, j + 1)
            m_new = jnp.maximum(ms[h], jnp.max(s, axis=-1, keepdims=True))
            p = jnp.exp2(s - m_new).astype(BF16)
            acc_ref[h] = jnp.exp2(ms[h] - m_new) * acc_ref[h] + _dot(p, va_ref[h, pl.ds(start, tq), :])
            out.append(m_new)
        return tuple(out)

    def finish():
        for h in range(hpb):
            acc = acc_ref[h]
            gate = _sigmoid(og_ref[:, h * dh:(h + 1) * dh].astype(F32))
            o_ref[:, h * dh:(h + 1) * dh] = (acc[:, :dh] / acc[:, dh:] * gate).astype(BF16)

    m0 = tuple(jnp.full((tq, 1), -jnp.inf, F32) for _ in range(hpb))

    @pl.when(qi == 0)
    def _():
        for h in range(hpb):
            s_ref[0, h] = scores(qs, h, 0)
        step(0, m0, 0, True)
        finish()

    @pl.when(qi > 0)
    def _():
        def steps(first, count, ms):
            for u in range(count):
                ms = step(first + u, ms, (1 + u) % 2, False)
            return ms

        ms = step(0, m0, first_slot, False)
        unroll = FOX_UNROLL
        n_mid = qi - 1
        ms = lax.fori_loop(0, n_mid // unroll, lambda t, ms: steps(1 + unroll * t, unroll, ms), ms)

        for rem in range(unroll):
            @pl.when(n_mid % unroll == rem)
            def _():
                step(qi, steps(qi - rem, rem, ms), (1 + rem) % 2, True)
                finish()


def _fox(proj, small_t, b_f, bsz, seq, dh, q_blk, k_blk, v_blk, og_blk, f_blk):
    n = proj.shape[0]
    tq = min(FOX_TQ, seq)
    nq = seq // tq
    hpb = FOX_HEADS_PER_STEP
    w = hpb * dh
    assert FOX_HEADS % hpb == 0 and q_blk % hpb == 0 and k_blk % hpb == 0 and v_blk % hpb == 0 and og_blk % hpb == 0
    return pl.pallas_call(
        functools.partial(_fox_body, tq=tq, nq=nq, dh=dh, hpb=hpb),
        grid=(bsz, FOX_HEADS // hpb, nq),
        in_specs=[
            pl.BlockSpec((tq, w), lambda b, h, i: (b * nq + i, q_blk // hpb + h)),
            pl.BlockSpec((tq, w), lambda b, h, i: (b * nq + jnp.minimum(i + 1, nq - 1), q_blk // hpb + h)),
            pl.BlockSpec((seq, w), lambda b, h, i: (b, k_blk // hpb + h)),
            pl.BlockSpec((seq, w), lambda b, h, i: (b, v_blk // hpb + h)),
            pl.BlockSpec((tq, w), lambda b, h, i: (b * nq + i, og_blk // hpb + h)),
            pl.BlockSpec((FOX_HEADS, seq), lambda b, h, i: (f_blk, b)),
            pl.BlockSpec((FOX_HEADS, 1), lambda b, h, i: (0, 0)),
        ],
        out_specs=pl.BlockSpec((tq, w), lambda b, h, i: (b * nq + i, h)),
        out_shape=jax.ShapeDtypeStruct((n, FOX_HEADS * dh), BF16),
        scratch_shapes=[pltpu.VMEM((hpb, seq, dh + LANES), BF16), pltpu.VMEM((hpb, tq, dh + LANES), F32),
                        pltpu.VMEM((3, hpb, tq, tq), F32), pltpu.VMEM((nq, FOX_HEADS, 1, tq), F32)],
        compiler_params=_params("parallel", "arbitrary", "arbitrary"),
        name="fox",
    )(proj, proj, proj, proj, proj, small_t, b_f)


def _merge_body(x_ref, y0_ref, y1_ref, y2_ref, gt_ref, bg_ref, wb_ref, wo_ref, o_ref, *, d):
    merged = None
    for n, y_ref in enumerate((y0_ref, y1_ref, y2_ref)):
        cs = slice(n * d, (n + 1) * d)
        gate = _sigmoid(gt_ref[:, cs].astype(F32) + bg_ref[:, cs])
        term = _dot(y_ref[...], wb_ref[n]) * gate
        merged = term if merged is None else merged + term
    o_ref[...] = x_ref[...] + _dot(merged.astype(BF16), wo_ref[...])


def _merge(x2, y_gla, y_mlstm, y_fox, proj, b_gate, w_branch, w_out, gate_blk):
    n, d = x2.shape
    tm = min(MERGE_TM, n)
    rowblk = lambda shape: pl.BlockSpec(shape, lambda i: (i, 0))
    return pl.pallas_call(
        functools.partial(_merge_body, d=d),
        grid=(n // tm,),
        in_specs=[
            rowblk((tm, d)), rowblk((tm, d)), rowblk((tm, d)), rowblk((tm, d)),
            pl.BlockSpec((tm, N_BRANCH * d), lambda i: (i, gate_blk)),
            pl.BlockSpec((1, N_BRANCH * d), lambda i: (0, 0)),
            pl.BlockSpec((N_BRANCH, d, d), lambda i: (0, 0, 0)),
            pl.BlockSpec((d, d), lambda i: (0, 0)),
        ],
        out_specs=rowblk((tm, d)),
        out_shape=jax.ShapeDtypeStruct((n, d), F32),
        compiler_params=_params("parallel"),
        name="merge",
    )(x2, y_gla, y_mlstm, y_fox, proj, b_gate, w_branch, w_out)


def _ffn_body(x_ref, xp_ref, g_ref, wu_ref, cw_ref, cb_ref, wd_ref, gf_ref, o_ref, h_ref,
              *, tm, halo, blocks_per_seq, dff, chunk, final):
    i = pl.program_id(0)

    def norm(x, gain):
        return x * lax.rsqrt(jnp.mean(x * x, axis=-1, keepdims=True) + EPS) * gain

    x = x_ref[...]
    hp = norm(xp_ref[...], g_ref[...]).astype(BF16)
    h_ref[0:halo, :] = jnp.where(i % blocks_per_seq != 0, hp, jnp.zeros_like(hp))
    h_ref[halo:halo + tm, :] = norm(x, g_ref[...]).astype(BF16)
    h = h_ref[...]

    def conv(lo, hi):
        u = _dot(h, wu_ref[:, lo:hi])
        y = cb_ref[:, lo:hi]
        for j in range(FFN_CONV - 1):
            y = y + cw_ref[j:j + 1, lo:hi] * pltpu.roll(u, FFN_CONV - 1 - j, 0)[halo:, :]
        return y + cw_ref[FFN_CONV - 1:FFN_CONV, lo:hi] * u[halo:, :]

    acc = x
    for lo in range(0, dff, chunk):
        hi = min(lo + chunk, dff)
        act = _silu(conv(dff + lo, dff + hi)) * conv(lo, hi)
        acc = acc + _dot(act.astype(BF16), wd_ref[lo:hi, :])
    o_ref[...] = norm(acc, gf_ref[...]) if final else acc


def _ffn(x2, g, w_up, conv_w, conv_b, w_down, seq, final_gain=None):
    n, d = x2.shape
    dff = w_down.shape[0]
    tm = min(FFN_TM, seq)
    halo = BF16_SUBLANES
    hb = tm // halo
    final = final_gain is not None
    resident = lambda shape: pl.BlockSpec(shape, lambda i: (0,) * len(shape), pipeline_mode=pl.Buffered(1))
    return pl.pallas_call(
        functools.partial(_ffn_body, tm=tm, halo=halo, blocks_per_seq=seq // tm, dff=dff, chunk=FFN_CHUNK,
                          final=final),
        grid=(n // tm,),
        in_specs=[
            pl.BlockSpec((tm, d), lambda i: (i, 0)),
            pl.BlockSpec((halo, d), lambda i: (jnp.maximum(i * hb - 1, 0), 0)),
            resident((1, d)),
            resident((d, 2 * dff)),
            resident((FFN_CONV, 2 * dff)),
            resident((1, 2 * dff)),
            resident((dff, d)),
            resident((1, d)),
        ],
        out_specs=pl.BlockSpec((tm, d), lambda i: (i, 0)),
        out_shape=jax.ShapeDtypeStruct((n, d), F32),
        scratch_shapes=[pltpu.VMEM((halo + tm, d), BF16)],
        compiler_params=_params("parallel"),
        name="ffn_final" if final else "ffn",
    )(x2, x2, g, w_up, conv_w, conv_b, w_down, final_gain if final else g)


def _block_diag_heads(w, n_heads):
    nblk, bc, bd = w.shape
    per = nblk // n_heads
    tiled = jnp.tile(w.reshape(n_heads, per * bc, bd), (1, 1, per))
    same_block = (jnp.arange(per * bc)[:, None] // bc) == (jnp.arange(per * bd)[None, :] // bd)
    return jnp.where(same_block, tiled, 0.0)


def _pad_lanes(a, width):
    return jnp.pad(a, ((0, 0),) * (a.ndim - 1) + ((0, width - a.shape[-1]),))


def kernel(x, norm_mix, w_in, b_gate, gla_w_lr, gla_b_lr, gla_norm, mlstm_conv_w, mlstm_conv_b, mlstm_wq, mlstm_wk,
           mlstm_wv, mlstm_w_i, mlstm_b_i, mlstm_w_f, mlstm_b_f, mlstm_skip, mlstm_norm, fox_b_f, w_branch, w_out,
           norm_ffn, ffn_w_up, ffn_conv_w, ffn_conv_b, ffn_w_down, norm_final):
    bsz, seq, d = x.shape
    depth = w_in.shape[0]
    hdk = gla_w_lr.shape[2]
    hdv = gla_norm.shape[1]
    inner = mlstm_conv_w.shape[2]
    fox_w = w_branch.shape[2]
    fox_dh = fox_w // FOX_HEADS
    dff = ffn_w_down.shape[1]
    assert hdv == d and inner == d and fox_w == d and 2 * hdk == d

    o_glr = 2 * hdk + hdv
    o_gr = o_glr + GLA_RANK
    o_ff = o_gr + hdv + 2 * inner + 3 * fox_w
    o_fog = o_ff + FOX_HEADS
    blk = {"gq": 0, "gk": 1, "gv": 1, "gr": 2, "mx": 3, "fq": 5 * d // fox_dh, "fk": 6 * d // fox_dh,
           "fv": 7 * d // fox_dh, "fog": 8 * d // fox_dh, "gates": 3}

    x2 = x.reshape(bsz * seq, d)
    for l in range(depth):
        wl = w_in[l]
        w_big = jnp.concatenate([wl[:, :o_glr], wl[:, o_gr:o_ff], wl[:, o_fog:]], axis=1).astype(BF16)
        w_small = _pad_lanes(jnp.concatenate([wl[:, o_glr:o_gr], wl[:, o_ff:o_fog]], axis=1), SMALL_W).astype(BF16)
        proj, small, small_t = _inproj(x2, norm_mix[l][None, :], w_big, w_small)

        w_lr_pad = jnp.pad(gla_w_lr[l], ((0, SMALL_W - GLA_RANK), (0, 0))).astype(BF16)
        y_gla = _gla(proj, small, w_lr_pad, gla_b_lr[l][None, :], gla_norm[l][None, :], bsz, seq)

        w_if = jnp.concatenate([_pad_lanes(mlstm_w_i[l], LANES), _pad_lanes(mlstm_w_f[l], LANES)], axis=1)
        b_if = jnp.concatenate([_pad_lanes(mlstm_b_i[l][None, :], LANES), _pad_lanes(mlstm_b_f[l][None, :], LANES)], axis=1)
        q_unscale = jnp.array([(inner // MLSTM_HEADS) ** 0.5, 1.0, 1.0], F32)[:, None, None]
        w_if3 = (w_if.reshape(3, inner, 2 * LANES) * q_unscale).astype(BF16)
        y_mlstm = _mlstm(
            proj, mlstm_conv_w[l], mlstm_conv_b[l][None, :],
            _block_diag_heads(mlstm_wq[l], MLSTM_HEADS).astype(BF16),
            _block_diag_heads(mlstm_wk[l], MLSTM_HEADS).astype(BF16),
            _block_diag_heads(mlstm_wv[l], MLSTM_HEADS).astype(BF16),
            w_if3, b_if, mlstm_skip[l][None, :], mlstm_norm[l][None, :], bsz, seq, blk["mx"])

        y_fox = _fox(proj, small_t, fox_b_f[l][:, None], bsz, seq, fox_dh,
                     blk["fq"], blk["fk"], blk["fv"], blk["fog"], GLA_RANK // FOX_HEADS)

        x2 = _merge(x2, y_gla, y_mlstm, y_fox, proj, b_gate[l].reshape(1, N_BRANCH * d),
                    w_branch[l].astype(BF16), w_out[l].astype(BF16), blk["gates"])

        x2 = _ffn(x2, norm_ffn[l][None, :], ffn_w_up[l].astype(BF16), ffn_conv_w[l], ffn_conv_b[l][None, :],
                  ffn_w_down[l].astype(BF16), seq, final_gain=norm_final[None, :] if l == depth - 1 else None)
    return x2.reshape(bsz, seq, d)
```

```python
import functools

import jax
import jax.numpy as jnp
from jax import lax
from jax.experimental import pallas as pl
from jax.experimental.pallas import tpu as pltpu

F32 = jnp.float32
BF16 = jnp.bfloat16
EPS = 1e-6
LOG2E = 1.4426950408889634

LANES = 128
F32_SUBLANES = 8
BF16_SUBLANES = 16
VMEM_LIMIT_BYTES = 56 * 1024 * 1024

GLA_HEADS = 4
GLA_RANK = 16
GLA_TAU = 16.0
GLA_CHUNK = 64
MLSTM_HEADS = 4
MLSTM_CONV = 4
MLSTM_BLOCK = 4
FOX_HEADS = 8
N_BRANCH = 3
FFN_CONV = 3
SMALL_W = LANES

INPROJ_TM = 1024
INPROJ_TN = 4096
GLA_ROWS = 1024
MLSTM_CHUNK = 256
MLSTM_ROWS = 1024
FOX_TQ = 512
FOX_HEADS_PER_STEP = 2
FOX_UNROLL = 8
MERGE_TM = 512
FFN_TM = 512
FFN_CHUNK = 1536


def _params(*sem):
    return pltpu.CompilerParams(dimension_semantics=sem, vmem_limit_bytes=VMEM_LIMIT_BYTES)


def _log_sigmoid(z, wide=False):
    e = jnp.exp(-jnp.abs(z))
    return jnp.minimum(z, 0.0) - (jnp.log(1.0 + e) if wide else jnp.log1p(e))


def _sigmoid(z):
    return 1.0 / (1.0 + jnp.exp(-z))


def _silu(z):
    return z * _sigmoid(z)


def _tri(n, upper=False):
    r = lax.broadcasted_iota(jnp.int32, (n, n), 0)
    c = lax.broadcasted_iota(jnp.int32, (n, n), 1)
    return (r <= c) if upper else (r >= c)


def _dot(a, b, **kw):
    return jnp.dot(a, b, preferred_element_type=F32, **kw)


def _split3(x):
    hi = x.astype(BF16)
    r1 = x - hi.astype(F32)
    mid = r1.astype(BF16)
    lo = (r1 - mid.astype(F32)).astype(BF16)
    return hi, mid, lo


def _cumsum_rows(tri, x):
    return sum(_dot(tri, t) for t in _split3(x))


def _dot_nt(a, b):
    return lax.dot_general(a, b, (((1,), (1,)), ((), ())), preferred_element_type=F32)


def _dot_tn(a, b):
    return lax.dot_general(a, b, (((0,), (0,)), ((), ())), preferred_element_type=F32)


def _inproj_body(x_ref, g_ref, w_ref, ws_ref, o_ref, os_ref, ot_ref, h_ref):
    @pl.when(pl.program_id(1) == 0)
    def _():
        x = x_ref[...]
        ms = jnp.mean(x * x, axis=-1, keepdims=True)
        h = (x * lax.rsqrt(ms + EPS) * g_ref[...]).astype(BF16)
        h_ref[...] = h
        small = _dot(h, ws_ref[...])
        os_ref[...] = small
        ot_ref[...] = small.T

    o_ref[...] = _dot(h_ref[...], w_ref[...]).astype(BF16)


def _inproj(x2, g, w_big, w_small, layer):
    n, d = x2.shape
    c = w_big.shape[2]
    tm, tn = min(INPROJ_TM, n), min(INPROJ_TN, c)
    return pl.pallas_call(
        _inproj_body,
        grid=(n // tm, c // tn),
        in_specs=[
            pl.BlockSpec((tm, d), lambda i, j: (i, 0)),
            pl.BlockSpec((1, d), lambda i, j: (0, 0)),
            pl.BlockSpec((None, d, tn), lambda i, j: (layer, 0, j)),
            pl.BlockSpec((None, d, SMALL_W), lambda i, j: (layer, 0, 0)),
        ],
        out_specs=[
            pl.BlockSpec((tm, tn), lambda i, j: (i, j)),
            pl.BlockSpec((tm, SMALL_W), lambda i, j: (i, 0)),
            pl.BlockSpec((SMALL_W, tm), lambda i, j: (0, i)),
        ],
        out_shape=[jax.ShapeDtypeStruct((n, c), BF16), jax.ShapeDtypeStruct((n, SMALL_W), F32),
                   jax.ShapeDtypeStruct((SMALL_W, n), F32)],
        scratch_shapes=[pltpu.VMEM((tm, d), BF16)],
        compiler_params=_params("parallel", "arbitrary"),
        name="inproj",
    )(x2, g, w_big, w_small)


def _gla_body(q_ref, k_ref, v_ref, r_ref, s_ref, wlr_ref, blr_ref, gn_ref, o_ref, st_ref, *, chunk, nchunk, dk, dv):
    @pl.when(pl.program_id(1) == 0)
    def _():
        st_ref[...] = jnp.zeros_like(st_ref)

    z = _dot(s_ref[...].astype(BF16), wlr_ref[...]) + blr_ref[...]
    log_a = _log_sigmoid(z, wide=True) * (1.0 / GLA_TAU)
    tri = _tri(chunk).astype(BF16)
    causal = _tri(chunk)
    scale = dk ** -0.5
    gn = gn_ref[...]
    for c in range(nchunk):
        rows = slice(c * chunk, (c + 1) * chunk)
        bc = _cumsum_rows(tri, log_a[rows])
        b_last = bc[chunk - 1:chunk, :]
        q = q_ref[rows, :].astype(F32) * scale
        k = k_ref[rows, :].astype(F32)
        q_in = (q * jnp.exp(bc)).astype(BF16)
        k_in = (k * jnp.exp(-bc)).astype(BF16)
        k_st = (k * jnp.exp(b_last - bc)).astype(BF16)
        decay = jnp.exp(b_last)
        for h in range(GLA_HEADS):
            ks = slice(h * dk, (h + 1) * dk)
            vs = slice(h * dv, (h + 1) * dv)
            qh = q_in[:, ks]
            vh = v_ref[rows, vs]
            att = jnp.where(causal, _dot_nt(qh, k_in[:, ks]), 0.0)
            st = st_ref[h]
            o = _dot(att.astype(BF16), vh) + _dot_nt(qh, st.astype(BF16))
            st_ref[h] = st * decay[:, ks] + _dot_tn(vh, k_st[:, ks])
            on = o * lax.rsqrt(jnp.mean(o * o, axis=-1, keepdims=True) + EPS) * gn[:, vs]
            o_ref[rows, vs] = (on * _silu(r_ref[rows, vs].astype(F32))).astype(BF16)


def _gla(proj, small, w_lr_pad, b_lr, g_norm, bsz, seq):
    n = proj.shape[0]
    hdk = w_lr_pad.shape[1]
    dk = hdk // GLA_HEADS
    hdv = g_norm.shape[1]
    dv = hdv // GLA_HEADS
    rows = min(GLA_ROWS, seq)
    nt = seq // rows
    assert hdv == 2 * hdk
    row = lambda b, t: b * nt + t
    return pl.pallas_call(
        functools.partial(_gla_body, chunk=GLA_CHUNK, nchunk=rows // GLA_CHUNK, dk=dk, dv=dv),
        grid=(bsz, nt),
        in_specs=[
            pl.BlockSpec((rows, hdk), lambda b, t: (row(b, t), 0)),
            pl.BlockSpec((rows, hdk), lambda b, t: (row(b, t), 1)),
            pl.BlockSpec((rows, hdv), lambda b, t: (row(b, t), 1)),
            pl.BlockSpec((rows, hdv), lambda b, t: (row(b, t), 2)),
            pl.BlockSpec((rows, SMALL_W), lambda b, t: (row(b, t), 0)),
            pl.BlockSpec((SMALL_W, hdk), lambda b, t: (0, 0)),
            pl.BlockSpec((1, hdk), lambda b, t: (0, 0)),
            pl.BlockSpec((1, hdv), lambda b, t: (0, 0)),
        ],
        out_specs=pl.BlockSpec((rows, hdv), lambda b, t: (row(b, t), 0)),
        out_shape=jax.ShapeDtypeStruct((n, hdv), BF16),
        scratch_shapes=[pltpu.VMEM((GLA_HEADS, dv, dk), F32)],
        compiler_params=_params("parallel", "arbitrary"),
        name="gla",
    )(proj, proj, proj, proj, small, w_lr_pad, b_lr, g_norm)


def _mlstm_body(xm_ref, z_ref, cw_ref, cb_ref, wq_ref, wk_ref, wv_ref, wif_ref, bif_ref,
                skip_ref, gn_ref, o_ref, xf_ref, q_sc, k_sc, v_sc, h_sc, xc_sc, c_sc, m_sc, *, chunk, nchunk, dh):
    rows_blk = chunk * nchunk
    halo = F32_SUBLANES

    @pl.when(pl.program_id(1) == 0)
    def _():
        xf_ref[0:halo, :] = jnp.zeros((halo, xf_ref.shape[1]), F32)
        c_sc[...] = jnp.zeros_like(c_sc)
        m_sc[...] = jnp.zeros_like(m_sc)

    xf_ref[halo:halo + rows_blk, :] = xm_ref[...].astype(F32)
    for h in range(MLSTM_HEADS):
        hs = slice(h * dh, (h + 1) * dh)
        xf = xf_ref[:, hs]
        conv = cb_ref[:, hs]
        for j in range(MLSTM_CONV - 1):
            conv = conv + cw_ref[j:j + 1, hs] * pltpu.roll(xf, MLSTM_CONV - 1 - j, 0)[halo:, :]
        conv = conv + cw_ref[MLSTM_CONV - 1:MLSTM_CONV, hs] * xf[halo:, :]
        xc = _silu(conv)
        xc_sc[:, hs] = xc
        xcb = xc.astype(BF16)
        q_sc[:, hs] = (_dot(xcb, wq_ref[h]) * dh ** -0.5).astype(BF16)
        k_sc[:, hs] = _dot(xcb, wk_ref[h]).astype(BF16)
        v_sc[:, hs] = _dot(xm_ref[:, hs], wv_ref[h]).astype(BF16)
    xf_ref[0:halo, :] = xf_ref[rows_blk:rows_blk + halo, :]
    qa, ka, va = q_sc[...], k_sc[...], v_sc[...]

    gcol = _dot(qa, wif_ref[0]) + _dot(ka, wif_ref[1]) + _dot(va, wif_ref[2]) + bif_ref[...]
    log_i = gcol[:, :LANES] * LOG2E
    log_f = _log_sigmoid(gcol[:, LANES:]) * LOG2E

    tri = _tri(chunk).astype(BF16)
    causal = _tri(chunk)
    ones_aug = jnp.ones((chunk, LANES), BF16)
    for c in range(nchunk):
        rows = slice(c * chunk, (c + 1) * chunk)
        cum_c = _cumsum_rows(tri, log_f[rows])
        a_c = log_i[rows] - cum_c
        cum_r = cum_c.T
        a_r = a_c.T
        for h in range(MLSTM_HEADS):
            hs = slice(h * dh, (h + 1) * dh)
            b_col, a_col = cum_c[:, h:h + 1], a_c[:, h:h + 1]
            b_row, a_row = cum_r[h:h + 1, :], a_r[h:h + 1, :]
            m_st = m_sc[h][:, 0:1]
            d_log = jnp.where(causal, b_col + a_row, -jnp.inf)
            m_inter = b_col + m_st
            m_t = jnp.maximum(m_inter, jnp.max(d_log, axis=-1, keepdims=True))
            w_intra = jnp.exp2(d_log - m_t)
            w_inter = jnp.exp2(m_inter - m_t)
            qc = q_sc[rows, hs]
            kc = k_sc[rows, hs]
            vc = v_sc[rows, hs]
            s = _dot_nt(qc, kc) * w_intra
            c_aug = c_sc[h]
            q_state = _dot(qc, c_aug.astype(BF16))
            num = _dot(s.astype(BF16), vc) + w_inter * q_state[:, :dh]
            qn = jnp.sum(s, axis=-1, keepdims=True) + w_inter * q_state[:, dh:dh + 1]
            h_sc[rows, hs] = num / jnp.maximum(jnp.abs(qn), jnp.exp2(-m_t))
            g = b_row[:, chunk - 1:chunk]
            m_new = jnp.maximum(g + m_st, g + jnp.max(a_row, axis=-1, keepdims=True))
            wa = jnp.exp2(g + a_col - m_new)
            dec = jnp.exp2(g + m_st - m_new)
            kw = (kc.astype(F32) * wa).astype(BF16)
            v_aug = jnp.concatenate([vc, ones_aug], axis=1)
            c_sc[h] = dec * c_aug + _dot_tn(kw, v_aug)
            m_sc[h] = jnp.broadcast_to(m_new, (1, LANES))

    for h in range(MLSTM_HEADS):
        hs = slice(h * dh, (h + 1) * dh)
        hh = h_sc[:, hs]
        hn = hh * lax.rsqrt(jnp.mean(hh * hh, axis=-1, keepdims=True) + EPS) * gn_ref[:, hs]
        zg = _silu(z_ref[:, hs].astype(F32))
        o_ref[:, hs] = ((hn + skip_ref[:, hs] * xc_sc[:, hs]) * zg).astype(BF16)


def _mlstm(proj, conv_w, conv_b, wq_bd, wk_bd, wv_bd, w_if, b_if, skip, g_norm, bsz, seq, col_blk):
    n = proj.shape[0]
    inner = conv_w.shape[1]
    dh = inner // MLSTM_HEADS
    rows = min(MLSTM_ROWS, seq)
    chunk = min(MLSTM_CHUNK, rows)
    nt = seq // rows
    row = lambda b, t: b * nt + t
    full = lambda shape: pl.BlockSpec(shape, lambda b, t: (0,) * len(shape))
    return pl.pallas_call(
        functools.partial(_mlstm_body, chunk=chunk, nchunk=rows // chunk, dh=dh),
        grid=(bsz, nt),
        in_specs=[
            pl.BlockSpec((rows, inner), lambda b, t: (row(b, t), col_blk)),
            pl.BlockSpec((rows, inner), lambda b, t: (row(b, t), col_blk + 1)),
            full((MLSTM_CONV, inner)),
            full((1, inner)),
            full((MLSTM_HEADS, dh, dh)),
            full((MLSTM_HEADS, dh, dh)),
            full((MLSTM_HEADS, dh, dh)),
            full((3, inner, 2 * LANES)),
            full((1, 2 * LANES)),
            full((1, inner)),
            full((1, inner)),
        ],
        out_specs=pl.BlockSpec((rows, inner), lambda b, t: (row(b, t), 0)),
        out_shape=jax.ShapeDtypeStruct((n, inner), BF16),
        scratch_shapes=[
            pltpu.VMEM((rows + F32_SUBLANES, inner), F32),
            pltpu.VMEM((rows, inner), BF16),
            pltpu.VMEM((rows, inner), BF16),
            pltpu.VMEM((rows, inner), BF16),
            pltpu.VMEM((rows, inner), F32),
            pltpu.VMEM((rows, inner), F32),
            pltpu.VMEM((MLSTM_HEADS, dh, dh + LANES), F32),
            pltpu.VMEM((MLSTM_HEADS, 1, LANES), F32),
        ],
        compiler_params=_params("parallel", "arbitrary"),
        name="mlstm",
    )(proj, proj, conv_w, conv_b, wq_bd, wk_bd, wv_bd, w_if, b_if, skip, g_norm)


def _fox_body(q_ref, qn_ref, k_ref, v_ref, og_ref, ft_ref, bf_ref, o_ref, va_ref, acc_ref, s_ref, fr_ref,
              *, tq, nq, dh, hpb):
    hp = pl.program_id(1)
    qi = pl.program_id(2)

    @pl.when(qi == 0)
    def _():
        for h in range(hpb):
            va_ref[h, :, :dh] = v_ref[:, h * dh:(h + 1) * dh]
            va_ref[h, :, dh:] = jnp.ones((va_ref.shape[1], LANES), BF16)

    @pl.when(jnp.logical_and(qi == 0, hp == 0))
    def _():
        log_f = _log_sigmoid(ft_ref[...] + bf_ref[...])
        tri_u = _tri(tq, upper=True).astype(BF16)
        carry = jnp.zeros((FOX_HEADS, 1), F32)
        for j in range(nq):
            cum = sum(_dot(t, tri_u) for t in _split3(log_f[:, j * tq:(j + 1) * tq])) + carry
            carry = cum[:, tq - 1:tq]
            for r in range(FOX_HEADS):
                fr_ref[j, r] = cum[r:r + 1, :] * LOG2E

    def scaled(ref):
        return [(ref[:, h * dh:(h + 1) * dh].astype(F32) * (dh ** -0.5 * LOG2E)).astype(BF16) for h in range(hpb)]

    qs = scaled(q_ref)
    acc_ref[...] = jnp.zeros_like(acc_ref)
    first_slot = 2

    def scores(q_heads, h, j):
        start = pl.multiple_of(j * tq, tq)
        return _dot_nt(q_heads[h], k_ref[pl.ds(start, tq), h * dh:(h + 1) * dh]) - fr_ref[j, hp * hpb + h]

    def step(j, ms, slot, last):
        start = pl.multiple_of(j * tq, tq)
        out = []
        q_next = scaled(qn_ref) if last else None
        for h in range(hpb):
            s = s_ref[slot, h]
            if last:
                s = jnp.where(_tri(tq), s, -jnp.inf)
                s_ref[first_slot, h] = scores(q_next, h, 0)
            else:
                s_ref[1 if slot == first_slot else 1 - slot, h] = scores(qs, h, j + 1)
            m_new = jnp.maximum(ms[h], jnp.max(s, axis=-1, keepdims=True))
            p = jnp.exp2(s - m_new).astype(BF16)
            acc_ref[h] = jnp.exp2(ms[h] - m_new) * acc_ref[h] + _dot(p, va_ref[h, pl.ds(start, tq), :])
            out.append(m_new)
        return tuple(out)

    def finish():
        for h in range(hpb):
            acc = acc_ref[h]
            gate = _sigmoid(og_ref[:, h * dh:(h + 1) * dh].astype(F32))
            o_ref[:, h * dh:(h + 1) * dh] = (acc[:, :dh] / acc[:, dh:] * gate).astype(BF16)

    m0 = tuple(jnp.full((tq, 1), -jnp.inf, F32) for _ in range(hpb))

    @pl.when(qi == 0)
    def _():
        for h in range(hpb):
            s_ref[0, h] = scores(qs, h, 0)
        step(0, m0, 0, True)
        finish()

    @pl.when(qi > 0)
    def _():
        def steps(first, count, ms):
            for u in range(count):
                ms = step(first + u, ms, (1 + u) % 2, False)
            return ms

        ms = step(0, m0, first_slot, False)
        unroll = FOX_UNROLL
        n_mid = qi - 1
        ms = lax.fori_loop(0, n_mid // unroll, lambda t, ms: steps(1 + unroll * t, unroll, ms), ms)

        for rem in range(unroll):
            @pl.when(n_mid % unroll == rem)
            def _():
                step(qi, steps(qi - rem, rem, ms), (1 + rem) % 2, True)
                finish()


def _fox(proj, small_t, b_f, bsz, seq, dh, q_blk, k_blk, v_blk, og_blk, f_blk):
    n = proj.shape[0]
    tq = min(FOX_TQ, seq)
    nq = seq // tq
    hpb = FOX_HEADS_PER_STEP
    w = hpb * dh
    assert FOX_HEADS % hpb == 0 and q_blk % hpb == 0 and k_blk % hpb == 0 and v_blk % hpb == 0 and og_blk % hpb == 0
    return pl.pallas_call(
        functools.partial(_fox_body, tq=tq, nq=nq, dh=dh, hpb=hpb),
        grid=(bsz, FOX_HEADS // hpb, nq),
        in_specs=[
            pl.BlockSpec((tq, w), lambda b, h, i: (b * nq + i, q_blk // hpb + h)),
            pl.BlockSpec((tq, w), lambda b, h, i: (b * nq + jnp.minimum(i + 1, nq - 1), q_blk // hpb + h)),
            pl.BlockSpec((seq, w), lambda b, h, i: (b, k_blk // hpb + h)),
            pl.BlockSpec((seq, w), lambda b, h, i: (b, v_blk // hpb + h)),
            pl.BlockSpec((tq, w), lambda b, h, i: (b * nq + i, og_blk // hpb + h)),
            pl.BlockSpec((FOX_HEADS, seq), lambda b, h, i: (f_blk, b)),
            pl.BlockSpec((FOX_HEADS, 1), lambda b, h, i: (0, 0)),
        ],
        out_specs=pl.BlockSpec((tq, w), lambda b, h, i: (b * nq + i, h)),
        out_shape=jax.ShapeDtypeStruct((n, FOX_HEADS * dh), BF16),
        scratch_shapes=[pltpu.VMEM((hpb, seq, dh + LANES), BF16), pltpu.VMEM((hpb, tq, dh + LANES), F32),
                        pltpu.VMEM((3, hpb, tq, tq), F32), pltpu.VMEM((nq, FOX_HEADS, 1, tq), F32)],
        compiler_params=_params("parallel", "arbitrary", "arbitrary"),
        name="fox",
    )(proj, proj, proj, proj, proj, small_t, b_f)


def _merge_body(x_ref, y0_ref, y1_ref, y2_ref, gt_ref, bg_ref, wb_ref, wo_ref, o_ref, *, d):
    merged = None
    for n, y_ref in enumerate((y0_ref, y1_ref, y2_ref)):
        cs = slice(n * d, (n + 1) * d)
        gate = _sigmoid(gt_ref[:, cs].astype(F32) + bg_ref[:, cs])
        term = _dot(y_ref[...], wb_ref[n]) * gate
        merged = term if merged is None else merged + term
    o_ref[...] = x_ref[...] + _dot(merged.astype(BF16), wo_ref[...])


def _merge(x2, y_gla, y_mlstm, y_fox, proj, b_gate, w_branch, w_out, gate_blk, layer):
    n, d = x2.shape
    tm = min(MERGE_TM, n)
    rowblk = lambda shape: pl.BlockSpec(shape, lambda i: (i, 0))
    return pl.pallas_call(
        functools.partial(_merge_body, d=d),
        grid=(n // tm,),
        in_specs=[
            rowblk((tm, d)), rowblk((tm, d)), rowblk((tm, d)), rowblk((tm, d)),
            pl.BlockSpec((tm, N_BRANCH * d), lambda i: (i, gate_blk)),
            pl.BlockSpec((1, N_BRANCH * d), lambda i: (0, 0)),
            pl.BlockSpec((None, N_BRANCH, d, d), lambda i: (layer, 0, 0, 0)),
            pl.BlockSpec((None, d, d), lambda i: (layer, 0, 0)),
        ],
        out_specs=rowblk((tm, d)),
        out_shape=jax.ShapeDtypeStruct((n, d), F32),
        compiler_params=_params("parallel"),
        name="merge",
    )(x2, y_gla, y_mlstm, y_fox, proj, b_gate, w_branch, w_out)


def _ffn_body(x_ref, xp_ref, g_ref, wu_ref, cw_ref, cb_ref, wd_ref, gf_ref, o_ref, h_ref,
              *, tm, halo, blocks_per_seq, dff, chunk, final):
    i = pl.program_id(0)

    def norm(x, gain):
        return x * lax.rsqrt(jnp.mean(x * x, axis=-1, keepdims=True) + EPS) * gain

    x = x_ref[...]
    hp = norm(xp_ref[...], g_ref[...]).astype(BF16)
    h_ref[0:halo, :] = jnp.where(i % blocks_per_seq != 0, hp, jnp.zeros_like(hp))
    h_ref[halo:halo + tm, :] = norm(x, g_ref[...]).astype(BF16)
    h = h_ref[...]

    def conv(lo, hi):
        u = _dot(h, wu_ref[:, lo:hi])
        y = cb_ref[:, lo:hi]
        for j in range(FFN_CONV - 1):
            y = y + cw_ref[j:j + 1, lo:hi] * pltpu.roll(u, FFN_CONV - 1 - j, 0)[halo:, :]
        return y + cw_ref[FFN_CONV - 1:FFN_CONV, lo:hi] * u[halo:, :]

    acc = x
    for lo in range(0, dff, chunk):
        hi = min(lo + chunk, dff)
        act = _silu(conv(dff + lo, dff + hi)) * conv(lo, hi)
        acc = acc + _dot(act.astype(BF16), wd_ref[lo:hi, :])
    o_ref[...] = norm(acc, gf_ref[...]) if final else acc


def _ffn(x2, g, w_up, conv_w, conv_b, w_down, seq, layer, final_gain=None):
    n, d = x2.shape
    dff = w_down.shape[1]
    tm = min(FFN_TM, seq)
    halo = BF16_SUBLANES
    hb = tm // halo
    final = final_gain is not None
    resident = lambda shape: pl.BlockSpec(shape, lambda i: (0,) * len(shape), pipeline_mode=pl.Buffered(1))
    return pl.pallas_call(
        functools.partial(_ffn_body, tm=tm, halo=halo, blocks_per_seq=seq // tm, dff=dff, chunk=FFN_CHUNK,
                          final=final),
        grid=(n // tm,),
        in_specs=[
            pl.BlockSpec((tm, d), lambda i: (i, 0)),
            pl.BlockSpec((halo, d), lambda i: (jnp.maximum(i * hb - 1, 0), 0)),
            resident((1, d)),
            pl.BlockSpec((None, d, 2 * dff), lambda i: (layer, 0, 0), pipeline_mode=pl.Buffered(1)),
            resident((FFN_CONV, 2 * dff)),
            resident((1, 2 * dff)),
            pl.BlockSpec((None, dff, d), lambda i: (layer, 0, 0), pipeline_mode=pl.Buffered(1)),
            resident((1, d)),
        ],
        out_specs=pl.BlockSpec((tm, d), lambda i: (i, 0)),
        out_shape=jax.ShapeDtypeStruct((n, d), F32),
        scratch_shapes=[pltpu.VMEM((halo + tm, d), BF16)],
        compiler_params=_params("parallel"),
        name="ffn_final" if final else "ffn",
    )(x2, x2, g, w_up, conv_w, conv_b, w_down, final_gain if final else g)


def _block_diag_heads(w, n_heads):
    nblk, bc, bd = w.shape
    per = nblk // n_heads
    tiled = jnp.tile(w.reshape(n_heads, per * bc, bd), (1, 1, per))
    same_block = (jnp.arange(per * bc)[:, None] // bc) == (jnp.arange(per * bd)[None, :] // bd)
    return jnp.where(same_block, tiled, 0.0)


def _pad_lanes(a, width):
    return jnp.pad(a, ((0, 0),) * (a.ndim - 1) + ((0, width - a.shape[-1]),))


def kernel(x, norm_mix, w_in, b_gate, gla_w_lr, gla_b_lr, gla_norm, mlstm_conv_w, mlstm_conv_b, mlstm_wq, mlstm_wk,
           mlstm_wv, mlstm_w_i, mlstm_b_i, mlstm_w_f, mlstm_b_f, mlstm_skip, mlstm_norm, fox_b_f, w_branch, w_out,
           norm_ffn, ffn_w_up, ffn_conv_w, ffn_conv_b, ffn_w_down, norm_final):
    bsz, seq, d = x.shape
    depth = w_in.shape[0]
    hdk = gla_w_lr.shape[2]
    hdv = gla_norm.shape[1]
    inner = mlstm_conv_w.shape[2]
    fox_w = w_branch.shape[2]
    fox_dh = fox_w // FOX_HEADS
    dff = ffn_w_down.shape[1]
    assert hdv == d and inner == d and fox_w == d and 2 * hdk == d

    o_glr = 2 * hdk + hdv
    o_gr = o_glr + GLA_RANK
    o_ff = o_gr + hdv + 2 * inner + 3 * fox_w
    o_fog = o_ff + FOX_HEADS
    blk = {"gq": 0, "gk": 1, "gv": 1, "gr": 2, "mx": 3, "fq": 5 * d // fox_dh, "fk": 6 * d // fox_dh,
           "fv": 7 * d // fox_dh, "fog": 8 * d // fox_dh, "gates": 3}

    x2 = x.reshape(bsz * seq, d)
    w_big = jnp.concatenate([w_in[..., :o_glr], w_in[..., o_gr:o_ff], w_in[..., o_fog:]], axis=-1).astype(BF16)
    w_small = _pad_lanes(jnp.concatenate([w_in[..., o_glr:o_gr], w_in[..., o_ff:o_fog]], axis=-1), SMALL_W).astype(BF16)
    w_branch_bf, w_out_bf = w_branch.astype(BF16), w_out.astype(BF16)
    ffn_w_up_bf, ffn_w_down_bf = ffn_w_up.astype(BF16), ffn_w_down.astype(BF16)
    for l in range(depth):
        proj, small, small_t = _inproj(x2, norm_mix[l][None, :], w_big, w_small, l)

        w_lr_pad = jnp.pad(gla_w_lr[l], ((0, SMALL_W - GLA_RANK), (0, 0))).astype(BF16)
        y_gla = _gla(proj, small, w_lr_pad, gla_b_lr[l][None, :], gla_norm[l][None, :], bsz, seq)

        w_if = jnp.concatenate([_pad_lanes(mlstm_w_i[l], LANES), _pad_lanes(mlstm_w_f[l], LANES)], axis=1)
        b_if = jnp.concatenate([_pad_lanes(mlstm_b_i[l][None, :], LANES), _pad_lanes(mlstm_b_f[l][None, :], LANES)], axis=1)
        q_unscale = jnp.array([(inner // MLSTM_HEADS) ** 0.5, 1.0, 1.0], F32)[:, None, None]
        w_if3 = (w_if.reshape(3, inner, 2 * LANES) * q_unscale).astype(BF16)
        y_mlstm = _mlstm(
            proj, mlstm_conv_w[l], mlstm_conv_b[l][None, :],
            _block_diag_heads(mlstm_wq[l], MLSTM_HEADS).astype(BF16),
            _block_diag_heads(mlstm_wk[l], MLSTM_HEADS).astype(BF16),
            _block_diag_heads(mlstm_wv[l], MLSTM_HEADS).astype(BF16),
            w_if3, b_if, mlstm_skip[l][None, :], mlstm_norm[l][None, :], bsz, seq, blk["mx"])

        y_fox = _fox(proj, small_t, fox_b_f[l][:, None], bsz, seq, fox_dh,
                     blk["fq"], blk["fk"], blk["fv"], blk["fog"], GLA_RANK // FOX_HEADS)

        x2 = _merge(x2, y_gla, y_mlstm, y_fox, proj, b_gate[l].reshape(1, N_BRANCH * d),
                    w_branch_bf, w_out_bf, blk["gates"], l)

        x2 = _ffn(x2, norm_ffn[l][None, :], ffn_w_up_bf, ffn_conv_w[l], ffn_conv_b[l][None, :],
                  ffn_w_down_bf, seq, l, final_gain=norm_final[None, :] if l == depth - 1 else None)
    return x2.reshape(bsz, seq, d)
```

```python
import functools

import jax
import jax.numpy as jnp
from jax import lax
from jax.experimental import pallas as pl
from jax.experimental.pallas import tpu as pltpu

F32 = jnp.float32
BF16 = jnp.bfloat16
EPS = 1e-6
LOG2E = 1.4426950408889634

LANES = 128
F32_SUBLANES = 8
BF16_SUBLANES = 16
VMEM_LIMIT_BYTES = 56 * 1024 * 1024

GLA_HEADS = 4
GLA_RANK = 16
GLA_TAU = 16.0
GLA_CHUNK = 64
MLSTM_HEADS = 4
MLSTM_CONV = 4
MLSTM_BLOCK = 4
FOX_HEADS = 8
N_BRANCH = 3
FFN_CONV = 3
SMALL_W = LANES

INPROJ_TM = 1024
INPROJ_TN = 4096
GLA_ROWS = 1024
MLSTM_CHUNK = 256
MLSTM_ROWS = 1024
FOX_TQ = 512
FOX_HEADS_PER_STEP = 2
FOX_UNROLL = 8
MERGE_TM = 512
FFN_TM = 512
FFN_CHUNK = 1536


def _params(*sem):
    return pltpu.CompilerParams(dimension_semantics=sem, vmem_limit_bytes=VMEM_LIMIT_BYTES)


def _log_sigmoid(z, wide=False):
    e = jnp.exp(-jnp.abs(z))
    return jnp.minimum(z, 0.0) - (jnp.log(1.0 + e) if wide else jnp.log1p(e))


def _sigmoid(z):
    return 1.0 / (1.0 + jnp.exp(-z))


def _silu(z):
    return z * _sigmoid(z)


def _tri(n, upper=False):
    r = lax.broadcasted_iota(jnp.int32, (n, n), 0)
    c = lax.broadcasted_iota(jnp.int32, (n, n), 1)
    return (r <= c) if upper else (r >= c)


def _dot(a, b, **kw):
    return jnp.dot(a, b, preferred_element_type=F32, **kw)


def _split3(x):
    hi = x.astype(BF16)
    r1 = x - hi.astype(F32)
    mid = r1.astype(BF16)
    lo = (r1 - mid.astype(F32)).astype(BF16)
    return hi, mid, lo


def _cumsum_rows(tri, x):
    return sum(_dot(tri, t) for t in _split3(x))


def _dot_nt(a, b):
    return lax.dot_general(a, b, (((1,), (1,)), ((), ())), preferred_element_type=F32)


def _dot_tn(a, b):
    return lax.dot_general(a, b, (((0,), (0,)), ((), ())), preferred_element_type=F32)


def _inproj_body(x_ref, g_ref, w_ref, ws_ref, o_ref, os_ref, ot_ref, h_ref):
    @pl.when(pl.program_id(1) == 0)
    def _():
        x = x_ref[...]
        ms = jnp.mean(x * x, axis=-1, keepdims=True)
        h = (x * lax.rsqrt(ms + EPS) * g_ref[...]).astype(BF16)
        h_ref[...] = h
        small = _dot(h, ws_ref[...])
        os_ref[...] = small
        ot_ref[...] = small.T

    o_ref[...] = _dot(h_ref[...], w_ref[...]).astype(BF16)


def _inproj(x2, g, w_big, w_small, layer):
    n, d = x2.shape
    c = w_big.shape[2]
    tm, tn = min(INPROJ_TM, n), min(INPROJ_TN, c)
    return pl.pallas_call(
        _inproj_body,
        grid=(n // tm, c // tn),
        in_specs=[
            pl.BlockSpec((tm, d), lambda i, j: (i, 0)),
            pl.BlockSpec((1, d), lambda i, j: (0, 0)),
            pl.BlockSpec((None, d, tn), lambda i, j: (layer, 0, j)),
            pl.BlockSpec((None, d, SMALL_W), lambda i, j: (layer, 0, 0)),
        ],
        out_specs=[
            pl.BlockSpec((tm, tn), lambda i, j: (i, j)),
            pl.BlockSpec((tm, SMALL_W), lambda i, j: (i, 0)),
            pl.BlockSpec((SMALL_W, tm), lambda i, j: (0, i)),
        ],
        out_shape=[jax.ShapeDtypeStruct((n, c), BF16), jax.ShapeDtypeStruct((n, SMALL_W), F32),
                   jax.ShapeDtypeStruct((SMALL_W, n), F32)],
        scratch_shapes=[pltpu.VMEM((tm, d), BF16)],
        compiler_params=_params("parallel", "arbitrary"),
        name="inproj",
    )(x2, g, w_big, w_small)


def _gla_body(q_ref, k_ref, v_ref, r_ref, s_ref, wlr_ref, blr_ref, gn_ref, o_ref, st_ref, *, chunk, nchunk, dk, dv):
    @pl.when(pl.program_id(1) == 0)
    def _():
        st_ref[...] = jnp.zeros_like(st_ref)

    z = _dot(s_ref[...].astype(BF16), wlr_ref[...]) + blr_ref[...]
    log_a = _log_sigmoid(z, wide=True) * (1.0 / GLA_TAU)
    tri = _tri(chunk).astype(BF16)
    causal = _tri(chunk)
    scale = dk ** -0.5
    gn = gn_ref[...]
    for c in range(nchunk):
        rows = slice(c * chunk, (c + 1) * chunk)
        bc = _cumsum_rows(tri, log_a[rows])
        b_last = bc[chunk - 1:chunk, :]
        q = q_ref[rows, :].astype(F32) * scale
        k = k_ref[rows, :].astype(F32)
        q_in = (q * jnp.exp(bc)).astype(BF16)
        k_in = (k * jnp.exp(-bc)).astype(BF16)
        k_st = (k * jnp.exp(b_last - bc)).astype(BF16)
        decay = jnp.exp(b_last)
        for h in range(GLA_HEADS):
            ks = slice(h * dk, (h + 1) * dk)
            vs = slice(h * dv, (h + 1) * dv)
            qh = q_in[:, ks]
            vh = v_ref[rows, vs]
            att = jnp.where(causal, _dot_nt(qh, k_in[:, ks]), 0.0)
            st = st_ref[h]
            o = _dot(att.astype(BF16), vh) + _dot_nt(qh, st.astype(BF16))
            st_ref[h] = st * decay[:, ks] + _dot_tn(vh, k_st[:, ks])
            on = o * lax.rsqrt(jnp.mean(o * o, axis=-1, keepdims=True) + EPS) * gn[:, vs]
            o_ref[rows, vs] = (on * _silu(r_ref[rows, vs].astype(F32))).astype(BF16)


def _gla(proj, small, w_lr_pad, b_lr, g_norm, bsz, seq):
    n = proj.shape[0]
    hdk = w_lr_pad.shape[1]
    dk = hdk // GLA_HEADS
    hdv = g_norm.shape[1]
    dv = hdv // GLA_HEADS
    rows = min(GLA_ROWS, seq)
    nt = seq // rows
    assert hdv == 2 * hdk
    row = lambda b, t: b * nt + t
    return pl.pallas_call(
        functools.partial(_gla_body, chunk=GLA_CHUNK, nchunk=rows // GLA_CHUNK, dk=dk, dv=dv),
        grid=(bsz, nt),
        in_specs=[
            pl.BlockSpec((rows, hdk), lambda b, t: (row(b, t), 0)),
            pl.BlockSpec((rows, hdk), lambda b, t: (row(b, t), 1)),
            pl.BlockSpec((rows, hdv), lambda b, t: (row(b, t), 1)),
            pl.BlockSpec((rows, hdv), lambda b, t: (row(b, t), 2)),
            pl.BlockSpec((rows, SMALL_W), lambda b, t: (row(b, t), 0)),
            pl.BlockSpec((SMALL_W, hdk), lambda b, t: (0, 0)),
            pl.BlockSpec((1, hdk), lambda b, t: (0, 0)),
            pl.BlockSpec((1, hdv), lambda b, t: (0, 0)),
        ],
        out_specs=pl.BlockSpec((rows, hdv), lambda b, t: (row(b, t), 0)),
        out_shape=jax.ShapeDtypeStruct((n, hdv), BF16),
        scratch_shapes=[pltpu.VMEM((GLA_HEADS, dv, dk), F32)],
        compiler_params=_params("parallel", "arbitrary"),
        name="gla",
    )(proj, proj, proj, proj, small, w_lr_pad, b_lr, g_norm)


def _mlstm_body(xm_ref, z_ref, cw_ref, cb_ref, wq_ref, wk_ref, wv_ref, wif_ref, bif_ref,
                skip_ref, gn_ref, o_ref, xf_ref, q_sc, k_sc, v_sc, h_sc, xc_sc, c_sc, m_sc, *, chunk, nchunk, dh):
    rows_blk = chunk * nchunk
    halo = F32_SUBLANES

    @pl.when(pl.program_id(1) == 0)
    def _():
        xf_ref[0:halo, :] = jnp.zeros((halo, xf_ref.shape[1]), F32)
        c_sc[...] = jnp.zeros_like(c_sc)
        m_sc[...] = jnp.zeros_like(m_sc)

    xf_ref[halo:halo + rows_blk, :] = xm_ref[...].astype(F32)
    for h in range(MLSTM_HEADS):
        hs = slice(h * dh, (h + 1) * dh)
        xf = xf_ref[:, hs]
        conv = cb_ref[:, hs]
        for j in range(MLSTM_CONV - 1):
            conv = conv + cw_ref[j:j + 1, hs] * pltpu.roll(xf, MLSTM_CONV - 1 - j, 0)[halo:, :]
        conv = conv + cw_ref[MLSTM_CONV - 1:MLSTM_CONV, hs] * xf[halo:, :]
        xc = _silu(conv)
        xc_sc[:, hs] = xc
        xcb = xc.astype(BF16)
        q_sc[:, hs] = (_dot(xcb, wq_ref[h]) * dh ** -0.5).astype(BF16)
        k_sc[:, hs] = _dot(xcb, wk_ref[h]).astype(BF16)
        v_sc[:, hs] = _dot(xm_ref[:, hs], wv_ref[h]).astype(BF16)
    xf_ref[0:halo, :] = xf_ref[rows_blk:rows_blk + halo, :]
    qa, ka, va = q_sc[...], k_sc[...], v_sc[...]

    gcol = _dot(qa, wif_ref[0]) + _dot(ka, wif_ref[1]) + _dot(va, wif_ref[2]) + bif_ref[...]
    log_i = gcol[:, :LANES] * LOG2E
    log_f = _log_sigmoid(gcol[:, LANES:]) * LOG2E

    tri = _tri(chunk).astype(BF16)
    causal = _tri(chunk)
    ones_aug = jnp.ones((chunk, LANES), BF16)
    for c in range(nchunk):
        rows = slice(c * chunk, (c + 1) * chunk)
        cum_c = _cumsum_rows(tri, log_f[rows])
        a_c = log_i[rows] - cum_c
        cum_r = cum_c.T
        a_r = a_c.T
        for h in range(MLSTM_HEADS):
            hs = slice(h * dh, (h + 1) * dh)
            b_col, a_col = cum_c[:, h:h + 1], a_c[:, h:h + 1]
            b_row, a_row = cum_r[h:h + 1, :], a_r[h:h + 1, :]
            m_st = m_sc[h][:, 0:1]
            d_log = jnp.where(causal, b_col + a_row, -jnp.inf)
            m_inter = b_col + m_st
            m_t = jnp.maximum(m_inter, jnp.max(d_log, axis=-1, keepdims=True))
            w_intra = jnp.exp2(d_log - m_t)
            w_inter = jnp.exp2(m_inter - m_t)
            qc = q_sc[rows, hs]
            kc = k_sc[rows, hs]
            vc = v_sc[rows, hs]
            s = _dot_nt(qc, kc) * w_intra
            c_aug = c_sc[h]
            q_state = _dot(qc, c_aug.astype(BF16))
            num = _dot(s.astype(BF16), vc) + w_inter * q_state[:, :dh]
            qn = jnp.sum(s, axis=-1, keepdims=True) + w_inter * q_state[:, dh:dh + 1]
            h_sc[rows, hs] = num / jnp.maximum(jnp.abs(qn), jnp.exp2(-m_t))
            g = b_row[:, chunk - 1:chunk]
            m_new = jnp.maximum(g + m_st, g + jnp.max(a_row, axis=-1, keepdims=True))
            wa = jnp.exp2(g + a_col - m_new)
            dec = jnp.exp2(g + m_st - m_new)
            kw = (kc.astype(F32) * wa).astype(BF16)
            v_aug = jnp.concatenate([vc, ones_aug], axis=1)
            c_sc[h] = dec * c_aug + _dot_tn(kw, v_aug)
            m_sc[h] = jnp.broadcast_to(m_new, (1, LANES))

    for h in range(MLSTM_HEADS):
        hs = slice(h * dh, (h + 1) * dh)
        hh = h_sc[:, hs]
        hn = hh * lax.rsqrt(jnp.mean(hh * hh, axis=-1, keepdims=True) + EPS) * gn_ref[:, hs]
        zg = _silu(z_ref[:, hs].astype(F32))
        o_ref[:, hs] = ((hn + skip_ref[:, hs] * xc_sc[:, hs]) * zg).astype(BF16)


def _mlstm(proj, conv_w, conv_b, wq_bd, wk_bd, wv_bd, w_if, b_if, skip, g_norm, bsz, seq, col_blk):
    n = proj.shape[0]
    inner = conv_w.shape[1]
    dh = inner // MLSTM_HEADS
    rows = min(MLSTM_ROWS, seq)
    chunk = min(MLSTM_CHUNK, rows)
    nt = seq // rows
    row = lambda b, t: b * nt + t
    full = lambda shape: pl.BlockSpec(shape, lambda b, t: (0,) * len(shape))
    return pl.pallas_call(
        functools.partial(_mlstm_body, chunk=chunk, nchunk=rows // chunk, dh=dh),
        grid=(bsz, nt),
        in_specs=[
            pl.BlockSpec((rows, inner), lambda b, t: (row(b, t), col_blk)),
            pl.BlockSpec((rows, inner), lambda b, t: (row(b, t), col_blk + 1)),
            full((MLSTM_CONV, inner)),
            full((1, inner)),
            full((MLSTM_HEADS, dh, dh)),
            full((MLSTM_HEADS, dh, dh)),
            full((MLSTM_HEADS, dh, dh)),
            full((3, inner, 2 * LANES)),
            full((1, 2 * LANES)),
            full((1, inner)),
            full((1, inner)),
        ],
        out_specs=pl.BlockSpec((rows, inner), lambda b, t: (row(b, t), 0)),
        out_shape=jax.ShapeDtypeStruct((n, inner), BF16),
        scratch_shapes=[
            pltpu.VMEM((rows + F32_SUBLANES, inner), F32),
            pltpu.VMEM((rows, inner), BF16),
            pltpu.VMEM((rows, inner), BF16),
            pltpu.VMEM((rows, inner), BF16),
            pltpu.VMEM((rows, inner), F32),
            pltpu.VMEM((rows, inner), F32),
            pltpu.VMEM((MLSTM_HEADS, dh, dh + LANES), F32),
            pltpu.VMEM((MLSTM_HEADS, 1, LANES), F32),
        ],
        compiler_params=_params("parallel", "arbitrary"),
        name="mlstm",
    )(proj, proj, conv_w, conv_b, wq_bd, wk_bd, wv_bd, w_if, b_if, skip, g_norm)


def _fox_body(q_ref, qn_ref, k_ref, v_ref, og_ref, ft_ref, bf_ref, o_ref, va_ref, acc_ref, s_ref, fr_ref,
              *, tq, nq, dh, hpb):
    hp = pl.program_id(1)
    qi = pl.program_id(2)

    @pl.when(qi == 0)
    def _():
        for h in range(hpb):
            va_ref[h, :, :dh] = v_ref[:, h * dh:(h + 1) * dh]
            va_ref[h, :, dh:] = jnp.ones((va_ref.shape[1], LANES), BF16)

    @pl.when(jnp.logical_and(qi == 0, hp == 0))
    def _():
        log_f = _log_sigmoid(ft_ref[...] + bf_ref[...])
        tri_u = _tri(tq, upper=True).astype(BF16)
        carry = jnp.zeros((FOX_HEADS, 1), F32)
        for j in range(nq):
            cum = sum(_dot(t, tri_u) for t in _split3(log_f[:, j * tq:(j + 1) * tq])) + carry
            carry = cum[:, tq - 1:tq]
            for r in range(FOX_HEADS):
                fr_ref[j, r] = cum[r:r + 1, :] * LOG2E

    def scaled(ref):
        return [(ref[:, h * dh:(h + 1) * dh].astype(F32) * (dh ** -0.5 * LOG2E)).astype(BF16) for h in range(hpb)]

    qs = scaled(q_ref)
    acc_ref[...] = jnp.zeros_like(acc_ref)
    first_slot = 2

    def scores(q_heads, h, j):
        start = pl.multiple_of(j * tq, tq)
        return _dot_nt(q_heads[h], k_ref[pl.ds(start, tq), h * dh:(h + 1) * dh]) - fr_ref[j, hp * hpb + h]

    def step(j, ms, slot, last):
        start = pl.multiple_of(j * tq, tq)
        out = []
        q_next = scaled(qn_ref) if last else None
        for h in range(hpb):
            s = s_ref[slot, h]
            if last:
                s = jnp.where(_tri(tq), s, -jnp.inf)
                s_ref[first_slot, h] = scores(q_next, h, 0)
            else:
                s_ref[1 if slot == first_slot else 1 - slot, h] = scores(qs, h, j + 1)
            m_new = jnp.maximum(ms[h], jnp.max(s, axis=-1, keepdims=True))
            p = jnp.exp2(s - m_new).astype(BF16)
            acc_ref[h] = jnp.exp2(ms[h] - m_new) * acc_ref[h] + _dot(p, va_ref[h, pl.ds(start, tq), :])
            out.append(m_new)
        return tuple(out)

    def finish():
        for h in range(hpb):
            acc = acc_ref[h]
            gate = _sigmoid(og_ref[:, h * dh:(h + 1) * dh].astype(F32))
            o_ref[:, h * dh:(h + 1) * dh] = (acc[:, :dh] / acc[:, dh:] * gate).astype(BF16)

    m0 = tuple(jnp.full((tq, 1), -jnp.inf, F32) for _ in range(hpb))

    @pl.when(qi == 0)
    def _():
        for h in range(hpb):
            s_ref[0, h] = scores(qs, h, 0)
        step(0, m0, 0, True)
        finish()

    @pl.when(qi > 0)
    def _():
        def steps(first, count, ms):
            for u in range(count):
                ms = step(first + u, ms, (1 + u) % 2, False)
            return ms

        ms = step(0, m0, first_slot, False)
        unroll = FOX_UNROLL
        n_mid = qi - 1
        ms = lax.fori_loop(0, n_mid // unroll, lambda t, ms: steps(1 + unroll * t, unroll, ms), ms)

        for rem in range(unroll):
            @pl.when(n_mid % unroll == rem)
            def _():
                step(qi, steps(qi - rem, rem, ms), (1 + rem) % 2, True)
                finish()


def _fox(proj, small_t, b_f, bsz, seq, dh, q_blk, k_blk, v_blk, og_blk, f_blk):
    n = proj.shape[0]
    tq = min(FOX_TQ, seq)
    nq = seq // tq
    hpb = FOX_HEADS_PER_STEP
    w = hpb * dh
    assert FOX_HEADS % hpb == 0 and q_blk % hpb == 0 and k_blk % hpb == 0 and v_blk % hpb == 0 and og_blk % hpb == 0
    return pl.pallas_call(
        functools.partial(_fox_body, tq=tq, nq=nq, dh=dh, hpb=hpb),
        grid=(bsz, FOX_HEADS // hpb, nq),
        in_specs=[
            pl.BlockSpec((tq, w), lambda b, h, i: (b * nq + i, q_blk // hpb + h)),
            pl.BlockSpec((tq, w), lambda b, h, i: (b * nq + jnp.minimum(i + 1, nq - 1), q_blk // hpb + h)),
            pl.BlockSpec((seq, w), lambda b, h, i: (b, k_blk // hpb + h)),
            pl.BlockSpec((seq, w), lambda b, h, i: (b, v_blk // hpb + h)),
            pl.BlockSpec((tq, w), lambda b, h, i: (b * nq + i, og_blk // hpb + h)),
            pl.BlockSpec((FOX_HEADS, seq), lambda b, h, i: (f_blk, b)),
            pl.BlockSpec((FOX_HEADS, 1), lambda b, h, i: (0, 0)),
        ],
        out_specs=pl.BlockSpec((tq, w), lambda b, h, i: (b * nq + i, h)),
        out_shape=jax.ShapeDtypeStruct((n, FOX_HEADS * dh), BF16),
        scratch_shapes=[pltpu.VMEM((hpb, seq, dh + LANES), BF16), pltpu.VMEM((hpb, tq, dh + LANES), F32),
                        pltpu.VMEM((3, hpb, tq, tq), F32), pltpu.VMEM((nq, FOX_HEADS, 1, tq), F32)],
        compiler_params=_params("parallel", "arbitrary", "arbitrary"),
        name="fox",
    )(proj, proj, proj, proj, proj, small_t, b_f)


def _merge_body(x_ref, y0_ref, y1_ref, y2_ref, gt_ref, bg_ref, wb_ref, wo_ref, o_ref, *, d):
    merged = None
    for n, y_ref in enumerate((y0_ref, y1_ref, y2_ref)):
        cs = slice(n * d, (n + 1) * d)
        gate = _sigmoid(gt_ref[:, cs].astype(F32) + bg_ref[:, cs])
        term = _dot(y_ref[...], wb_ref[n]) * gate
        merged = term if merged is None else merged + term
    o_ref[...] = x_ref[...] + _dot(merged.astype(BF16), wo_ref[...])


def _merge(x2, y_gla, y_mlstm, y_fox, proj, b_gate, w_branch, w_out, gate_blk, layer):
    n, d = x2.shape
    tm = min(MERGE_TM, n)
    rowblk = lambda shape: pl.BlockSpec(shape, lambda i: (i, 0))
    return pl.pallas_call(
        functools.partial(_merge_body, d=d),
        grid=(n // tm,),
        in_specs=[
            rowblk((tm, d)), rowblk((tm, d)), rowblk((tm, d)), rowblk((tm, d)),
            pl.BlockSpec((tm, N_BRANCH * d), lambda i: (i, gate_blk)),
            pl.BlockSpec((1, N_BRANCH * d), lambda i: (0, 0)),
            pl.BlockSpec((None, N_BRANCH, d, d), lambda i: (layer, 0, 0, 0)),
            pl.BlockSpec((None, d, d), lambda i: (layer, 0, 0)),
        ],
        out_specs=rowblk((tm, d)),
        out_shape=jax.ShapeDtypeStruct((n, d), F32),
        compiler_params=_params("parallel"),
        name="merge",
    )(x2, y_gla, y_mlstm, y_fox, proj, b_gate, w_branch, w_out)


def _ffn_body(x_ref, xp_ref, g_ref, wu_ref, cw_ref, cb_ref, wd_ref, gf_ref, o_ref, h_ref,
              *, tm, halo, blocks_per_seq, dff, chunk, final):
    i = pl.program_id(0)

    def norm(x, gain):
        return x * lax.rsqrt(jnp.mean(x * x, axis=-1, keepdims=True) + EPS) * gain

    x = x_ref[...]
    hp = norm(xp_ref[...], g_ref[...]).astype(BF16)
    h_ref[0:halo, :] = jnp.where(i % blocks_per_seq != 0, hp, jnp.zeros_like(hp))
    h_ref[halo:halo + tm, :] = norm(x, g_ref[...]).astype(BF16)
    h = h_ref[...]

    def conv(lo, hi):
        u = _dot(h, wu_ref[:, lo:hi])
        y = cb_ref[:, lo:hi]
        for j in range(FFN_CONV - 1):
            y = y + cw_ref[j:j + 1, lo:hi] * pltpu.roll(u, FFN_CONV - 1 - j, 0)[halo:, :]
        return y + cw_ref[FFN_CONV - 1:FFN_CONV, lo:hi] * u[halo:, :]

    acc = x
    for lo in range(0, dff, chunk):
        hi = min(lo + chunk, dff)
        act = _silu(conv(dff + lo, dff + hi)) * conv(lo, hi)
        acc = acc + _dot(act.astype(BF16), wd_ref[lo:hi, :])
    o_ref[...] = norm(acc, gf_ref[...]) if final else acc


def _ffn(x2, g, w_up, conv_w, conv_b, w_down, seq, layer, final_gain=None):
    n, d = x2.shape
    dff = w_down.shape[1]
    tm = min(FFN_TM, seq)
    halo = BF16_SUBLANES
    hb = tm // halo
    final = final_gain is not None
    resident = lambda shape: pl.BlockSpec(shape, lambda i: (0,) * len(shape), pipeline_mode=pl.Buffered(1))
    return pl.pallas_call(
        functools.partial(_ffn_body, tm=tm, halo=halo, blocks_per_seq=seq // tm, dff=dff, chunk=FFN_CHUNK,
                          final=final),
        grid=(n // tm,),
        in_specs=[
            pl.BlockSpec((tm, d), lambda i: (i, 0)),
            pl.BlockSpec((halo, d), lambda i: (jnp.maximum(i * hb - 1, 0), 0)),
            resident((1, d)),
            pl.BlockSpec((None, d, 2 * dff), lambda i: (layer, 0, 0), pipeline_mode=pl.Buffered(1)),
            resident((FFN_CONV, 2 * dff)),
            resident((1, 2 * dff)),
            pl.BlockSpec((None, dff, d), lambda i: (layer, 0, 0), pipeline_mode=pl.Buffered(1)),
            resident((1, d)),
        ],
        out_specs=pl.BlockSpec((tm, d), lambda i: (i, 0)),
        out_shape=jax.ShapeDtypeStruct((n, d), F32),
        scratch_shapes=[pltpu.VMEM((halo + tm, d), BF16)],
        compiler_params=_params("parallel"),
        name="ffn_final" if final else "ffn",
    )(x2, x2, g, w_up, conv_w, conv_b, w_down, final_gain if final else g)


def _block_diag_heads(w, n_heads):
    nblk, bc, bd = w.shape
    per = nblk // n_heads
    tiled = jnp.tile(w.reshape(n_heads, per * bc, bd), (1, 1, per))
    same_block = (jnp.arange(per * bc)[:, None] // bc) == (jnp.arange(per * bd)[None, :] // bd)
    return jnp.where(same_block, tiled, 0.0)


def _pad_lanes(a, width):
    return jnp.pad(a, ((0, 0),) * (a.ndim - 1) + ((0, width - a.shape[-1]),))


def kernel(x, norm_mix, w_in, b_gate, gla_w_lr, gla_b_lr, gla_norm, mlstm_conv_w, mlstm_conv_b, mlstm_wq, mlstm_wk,
           mlstm_wv, mlstm_w_i, mlstm_b_i, mlstm_w_f, mlstm_b_f, mlstm_skip, mlstm_norm, fox_b_f, w_branch, w_out,
           norm_ffn, ffn_w_up, ffn_conv_w, ffn_conv_b, ffn_w_down, norm_final):
    bsz, seq, d = x.shape
    depth = w_in.shape[0]
    hdk = gla_w_lr.shape[2]
    hdv = gla_norm.shape[1]
    inner = mlstm_conv_w.shape[2]
    fox_w = w_branch.shape[2]
    fox_dh = fox_w // FOX_HEADS
    dff = ffn_w_down.shape[1]
    assert hdv == d and inner == d and fox_w == d and 2 * hdk == d

    o_glr = 2 * hdk + hdv
    o_gr = o_glr + GLA_RANK
    o_ff = o_gr + hdv + 2 * inner + 3 * fox_w
    o_fog = o_ff + FOX_HEADS
    blk = {"gq": 0, "gk": 1, "gv": 1, "gr": 2, "mx": 3, "fq": 5 * d // fox_dh, "fk": 6 * d // fox_dh,
           "fv": 7 * d // fox_dh, "fog": 8 * d // fox_dh, "gates": 3}

    x2 = x.reshape(bsz * seq, d)
    w_in_bf = w_in.astype(BF16)
    w_big = jnp.concatenate([w_in_bf[..., :o_glr], w_in_bf[..., o_gr:o_ff], w_in_bf[..., o_fog:]], axis=-1)
    w_small = _pad_lanes(jnp.concatenate([w_in_bf[..., o_glr:o_gr], w_in_bf[..., o_ff:o_fog]], axis=-1), SMALL_W)
    w_branch_bf, w_out_bf = w_branch.astype(BF16), w_out.astype(BF16)
    ffn_w_up_bf, ffn_w_down_bf = ffn_w_up.astype(BF16), ffn_w_down.astype(BF16)
    for l in range(depth):
        proj, small, small_t = _inproj(x2, norm_mix[l][None, :], w_big, w_small, l)

        w_lr_pad = jnp.pad(gla_w_lr[l], ((0, SMALL_W - GLA_RANK), (0, 0))).astype(BF16)
        y_gla = _gla(proj, small, w_lr_pad, gla_b_lr[l][None, :], gla_norm[l][None, :], bsz, seq)

        w_if = jnp.concatenate([_pad_lanes(mlstm_w_i[l], LANES), _pad_lanes(mlstm_w_f[l], LANES)], axis=1)
        b_if = jnp.concatenate([_pad_lanes(mlstm_b_i[l][None, :], LANES), _pad_lanes(mlstm_b_f[l][None, :], LANES)], axis=1)
        q_unscale = jnp.array([(inner // MLSTM_HEADS) ** 0.5, 1.0, 1.0], F32)[:, None, None]
        w_if3 = (w_if.reshape(3, inner, 2 * LANES) * q_unscale).astype(BF16)
        y_mlstm = _mlstm(
            proj, mlstm_conv_w[l], mlstm_conv_b[l][None, :],
            _block_diag_heads(mlstm_wq[l], MLSTM_HEADS).astype(BF16),
            _block_diag_heads(mlstm_wk[l], MLSTM_HEADS).astype(BF16),
            _block_diag_heads(mlstm_wv[l], MLSTM_HEADS).astype(BF16),
            w_if3, b_if, mlstm_skip[l][None, :], mlstm_norm[l][None, :], bsz, seq, blk["mx"])

        y_fox = _fox(proj, small_t, fox_b_f[l][:, None], bsz, seq, fox_dh,
                     blk["fq"], blk["fk"], blk["fv"], blk["fog"], GLA_RANK // FOX_HEADS)

        x2 = _merge(x2, y_gla, y_mlstm, y_fox, proj, b_gate[l].reshape(1, N_BRANCH * d),
                    w_branch_bf, w_out_bf, blk["gates"], l)

        x2 = _ffn(x2, norm_ffn[l][None, :], ffn_w_up_bf, ffn_conv_w[l], ffn_conv_b[l][None, :],
                  ffn_w_down_bf, seq, l, final_gain=norm_final[None, :] if l == depth - 1 else None)
    return x2.reshape(bsz, seq, d)
```

```python
import functools

import jax
import jax.numpy as jnp
from jax import lax
from jax.experimental import pallas as pl
from jax.experimental.pallas import tpu as pltpu

F32 = jnp.float32
BF16 = jnp.bfloat16
EPS = 1e-6
LOG2E = 1.4426950408889634

LANES = 128
F32_SUBLANES = 8
BF16_SUBLANES = 16
VMEM_LIMIT_BYTES = 56 * 1024 * 1024

GLA_HEADS = 4
GLA_RANK = 16
GLA_TAU = 16.0
GLA_CHUNK = 64
MLSTM_HEADS = 4
MLSTM_CONV = 4
MLSTM_BLOCK = 4
FOX_HEADS = 8
N_BRANCH = 3
FFN_CONV = 3
SMALL_W = LANES

INPROJ_TM = 1024
INPROJ_TN = 4096
GLA_ROWS = 1024
MLSTM_CHUNK = 256
MLSTM_ROWS = 1024
FOX_TQ = 512
FOX_HEADS_PER_STEP = 2
FOX_UNROLL = 8
MERGE_TM = 512
FFN_TM = 512
FFN_CHUNK = 1536


def _params(*sem):
    return pltpu.CompilerParams(dimension_semantics=sem, vmem_limit_bytes=VMEM_LIMIT_BYTES)


def _log_sigmoid(z, wide=False):
    e = jnp.exp(-jnp.abs(z))
    return jnp.minimum(z, 0.0) - (jnp.log(1.0 + e) if wide else jnp.log1p(e))


def _sigmoid(z):
    return 1.0 / (1.0 + jnp.exp(-z))


def _silu(z):
    return z * _sigmoid(z)


def _tri(n, upper=False):
    r = lax.broadcasted_iota(jnp.int32, (n, n), 0)
    c = lax.broadcasted_iota(jnp.int32, (n, n), 1)
    return (r <= c) if upper else (r >= c)


def _dot(a, b, **kw):
    return jnp.dot(a, b, preferred_element_type=F32, **kw)


def _split3(x):
    hi = x.astype(BF16)
    r1 = x - hi.astype(F32)
    mid = r1.astype(BF16)
    lo = (r1 - mid.astype(F32)).astype(BF16)
    return hi, mid, lo


def _cumsum_rows(tri, x):
    return sum(_dot(tri, t) for t in _split3(x))


def _dot_nt(a, b):
    return lax.dot_general(a, b, (((1,), (1,)), ((), ())), preferred_element_type=F32)


def _dot_tn(a, b):
    return lax.dot_general(a, b, (((0,), (0,)), ((), ())), preferred_element_type=F32)


def _inproj_body(x_ref, g_ref, w_ref, ws_ref, o_ref, os_ref, ot_ref, h_ref):
    @pl.when(pl.program_id(1) == 0)
    def _():
        x = x_ref[...]
        ms = jnp.mean(x * x, axis=-1, keepdims=True)
        h = (x * lax.rsqrt(ms + EPS) * g_ref[...]).astype(BF16)
        h_ref[...] = h
        small = _dot(h, ws_ref[...])
        os_ref[...] = small
        ot_ref[...] = small.T

    o_ref[...] = _dot(h_ref[...], w_ref[...]).astype(BF16)


def _inproj(x2, g, w_big, w_small, layer):
    n, d = x2.shape
    c = w_big.shape[2]
    tm, tn = min(INPROJ_TM, n), min(INPROJ_TN, c)
    return pl.pallas_call(
        _inproj_body,
        grid=(n // tm, c // tn),
        in_specs=[
            pl.BlockSpec((tm, d), lambda i, j: (i, 0)),
            pl.BlockSpec((1, d), lambda i, j: (0, 0)),
            pl.BlockSpec((None, d, tn), lambda i, j: (layer, 0, j)),
            pl.BlockSpec((None, d, SMALL_W), lambda i, j: (layer, 0, 0)),
        ],
        out_specs=[
            pl.BlockSpec((tm, tn), lambda i, j: (i, j)),
            pl.BlockSpec((tm, SMALL_W), lambda i, j: (i, 0)),
            pl.BlockSpec((SMALL_W, tm), lambda i, j: (0, i)),
        ],
        out_shape=[jax.ShapeDtypeStruct((n, c), BF16), jax.ShapeDtypeStruct((n, SMALL_W), F32),
                   jax.ShapeDtypeStruct((SMALL_W, n), F32)],
        scratch_shapes=[pltpu.VMEM((tm, d), BF16)],
        compiler_params=_params("parallel", "arbitrary"),
        name="inproj",
    )(x2, g, w_big, w_small)


def _gla_body(q_ref, k_ref, v_ref, r_ref, s_ref, wlr_ref, blr_ref, gn_ref, o_ref, st_ref, *, chunk, nchunk, dk, dv):
    @pl.when(pl.program_id(1) == 0)
    def _():
        st_ref[...] = jnp.zeros_like(st_ref)

    z = _dot(s_ref[...].astype(BF16), wlr_ref[...]) + blr_ref[...]
    log_a = _log_sigmoid(z, wide=True) * (1.0 / GLA_TAU)
    tri = _tri(chunk).astype(BF16)
    causal = _tri(chunk)
    scale = dk ** -0.5
    gn = gn_ref[...]
    for c in range(nchunk):
        rows = slice(c * chunk, (c + 1) * chunk)
        bc = _cumsum_rows(tri, log_a[rows])
        b_last = bc[chunk - 1:chunk, :]
        q = q_ref[rows, :].astype(F32) * scale
        k = k_ref[rows, :].astype(F32)
        q_in = (q * jnp.exp(bc)).astype(BF16)
        k_in = (k * jnp.exp(-bc)).astype(BF16)
        k_st = (k * jnp.exp(b_last - bc)).astype(BF16)
        decay = jnp.exp(b_last)
        for h in range(GLA_HEADS):
            ks = slice(h * dk, (h + 1) * dk)
            vs = slice(h * dv, (h + 1) * dv)
            qh = q_in[:, ks]
            vh = v_ref[rows, vs]
            att = jnp.where(causal, _dot_nt(qh, k_in[:, ks]), 0.0)
            st = st_ref[h]
            o = _dot(att.astype(BF16), vh) + _dot_nt(qh, st.astype(BF16))
            st_ref[h] = st * decay[:, ks] + _dot_tn(vh, k_st[:, ks])
            on = o * lax.rsqrt(jnp.mean(o * o, axis=-1, keepdims=True) + EPS) * gn[:, vs]
            o_ref[rows, vs] = (on * _silu(r_ref[rows, vs].astype(F32))).astype(BF16)


def _gla(proj, small, w_lr_pad, b_lr, g_norm, bsz, seq):
    n = proj.shape[0]
    hdk = w_lr_pad.shape[1]
    dk = hdk // GLA_HEADS
    hdv = g_norm.shape[1]
    dv = hdv // GLA_HEADS
    rows = min(GLA_ROWS, seq)
    nt = seq // rows
    assert hdv == 2 * hdk
    row = lambda b, t: b * nt + t
    return pl.pallas_call(
        functools.partial(_gla_body, chunk=GLA_CHUNK, nchunk=rows // GLA_CHUNK, dk=dk, dv=dv),
        grid=(bsz, nt),
        in_specs=[
            pl.BlockSpec((rows, hdk), lambda b, t: (row(b, t), 0)),
            pl.BlockSpec((rows, hdk), lambda b, t: (row(b, t), 1)),
            pl.BlockSpec((rows, hdv), lambda b, t: (row(b, t), 1)),
            pl.BlockSpec((rows, hdv), lambda b, t: (row(b, t), 2)),
            pl.BlockSpec((rows, SMALL_W), lambda b, t: (row(b, t), 0)),
            pl.BlockSpec((SMALL_W, hdk), lambda b, t: (0, 0)),
            pl.BlockSpec((1, hdk), lambda b, t: (0, 0)),
            pl.BlockSpec((1, hdv), lambda b, t: (0, 0)),
        ],
        out_specs=pl.BlockSpec((rows, hdv), lambda b, t: (row(b, t), 0)),
        out_shape=jax.ShapeDtypeStruct((n, hdv), BF16),
        scratch_shapes=[pltpu.VMEM((GLA_HEADS, dv, dk), F32)],
        compiler_params=_params("parallel", "arbitrary"),
        name="gla",
    )(proj, proj, proj, proj, small, w_lr_pad, b_lr, g_norm)


def _mlstm_body(xm_ref, z_ref, cw_ref, cb_ref, wq_ref, wk_ref, wv_ref, wif_ref, bif_ref,
                skip_ref, gn_ref, o_ref, xf_ref, q_sc, k_sc, v_sc, h_sc, xc_sc, c_sc, m_sc, *, chunk, nchunk, dh):
    rows_blk = chunk * nchunk
    halo = F32_SUBLANES

    @pl.when(pl.program_id(1) == 0)
    def _():
        xf_ref[0:halo, :] = jnp.zeros((halo, xf_ref.shape[1]), F32)
        c_sc[...] = jnp.zeros_like(c_sc)
        m_sc[...] = jnp.zeros_like(m_sc)

    xf_ref[halo:halo + rows_blk, :] = xm_ref[...].astype(F32)
    for h in range(MLSTM_HEADS):
        hs = slice(h * dh, (h + 1) * dh)
        xf = xf_ref[:, hs]
        conv = cb_ref[:, hs]
        for j in range(MLSTM_CONV - 1):
            conv = conv + cw_ref[j:j + 1, hs] * pltpu.roll(xf, MLSTM_CONV - 1 - j, 0)[halo:, :]
        conv = conv + cw_ref[MLSTM_CONV - 1:MLSTM_CONV, hs] * xf[halo:, :]
        xc = _silu(conv)
        xc_sc[:, hs] = xc
        xcb = xc.astype(BF16)
        q_sc[:, hs] = (_dot(xcb, wq_ref[h]) * dh ** -0.5).astype(BF16)
        k_sc[:, hs] = _dot(xcb, wk_ref[h]).astype(BF16)
        v_sc[:, hs] = _dot(xm_ref[:, hs], wv_ref[h]).astype(BF16)
    xf_ref[0:halo, :] = xf_ref[rows_blk:rows_blk + halo, :]
    qa, ka, va = q_sc[...], k_sc[...], v_sc[...]

    gcol = _dot(qa, wif_ref[0]) + _dot(ka, wif_ref[1]) + _dot(va, wif_ref[2]) + bif_ref[...]
    log_i = gcol[:, :LANES] * LOG2E
    log_f = _log_sigmoid(gcol[:, LANES:]) * LOG2E

    tri = _tri(chunk).astype(BF16)
    causal = _tri(chunk)
    ones_aug = jnp.ones((chunk, LANES), BF16)
    for c in range(nchunk):
        rows = slice(c * chunk, (c + 1) * chunk)
        cum_c = _cumsum_rows(tri, log_f[rows])
        a_c = log_i[rows] - cum_c
        cum_r = cum_c.T
        a_r = a_c.T
        for h in range(MLSTM_HEADS):
            hs = slice(h * dh, (h + 1) * dh)
            b_col, a_col = cum_c[:, h:h + 1], a_c[:, h:h + 1]
            b_row, a_row = cum_r[h:h + 1, :], a_r[h:h + 1, :]
            m_st = m_sc[h][:, 0:1]
            d_log = jnp.where(causal, b_col + a_row, -jnp.inf)
            m_inter = b_col + m_st
            m_t = jnp.maximum(m_inter, jnp.max(d_log, axis=-1, keepdims=True))
            w_intra = jnp.exp2(d_log - m_t)
            w_inter = jnp.exp2(m_inter - m_t)
            qc = q_sc[rows, hs]
            kc = k_sc[rows, hs]
            vc = v_sc[rows, hs]
            s = _dot_nt(qc, kc) * w_intra
            c_aug = c_sc[h]
            q_state = _dot(qc, c_aug.astype(BF16))
            num = _dot(s.astype(BF16), vc) + w_inter * q_state[:, :dh]
            qn = jnp.sum(s, axis=-1, keepdims=True) + w_inter * q_state[:, dh:dh + 1]
            h_sc[rows, hs] = num / jnp.maximum(jnp.abs(qn), jnp.exp2(-m_t))
            g = b_row[:, chunk - 1:chunk]
            m_new = jnp.maximum(g + m_st, g + jnp.max(a_row, axis=-1, keepdims=True))
            wa = jnp.exp2(g + a_col - m_new)
            dec = jnp.exp2(g + m_st - m_new)
            kw = (kc.astype(F32) * wa).astype(BF16)
            v_aug = jnp.concatenate([vc, ones_aug], axis=1)
            c_sc[h] = dec * c_aug + _dot_tn(kw, v_aug)
            m_sc[h] = jnp.broadcast_to(m_new, (1, LANES))

    for h in range(MLSTM_HEADS):
        hs = slice(h * dh, (h + 1) * dh)
        hh = h_sc[:, hs]
        hn = hh * lax.rsqrt(jnp.mean(hh * hh, axis=-1, keepdims=True) + EPS) * gn_ref[:, hs]
        zg = _silu(z_ref[:, hs].astype(F32))
        o_ref[:, hs] = ((hn + skip_ref[:, hs] * xc_sc[:, hs]) * zg).astype(BF16)


def _mlstm(proj, conv_w, conv_b, wq_bd, wk_bd, wv_bd, w_if, b_if, skip, g_norm, bsz, seq, col_blk):
    n = proj.shape[0]
    inner = conv_w.shape[1]
    dh = inner // MLSTM_HEADS
    rows = min(MLSTM_ROWS, seq)
    chunk = min(MLSTM_CHUNK, rows)
    nt = seq // rows
    row = lambda b, t: b * nt + t
    full = lambda shape: pl.BlockSpec(shape, lambda b, t: (0,) * len(shape))
    return pl.pallas_call(
        functools.partial(_mlstm_body, chunk=chunk, nchunk=rows // chunk, dh=dh),
        grid=(bsz, nt),
        in_specs=[
            pl.BlockSpec((rows, inner), lambda b, t: (row(b, t), col_blk)),
            pl.BlockSpec((rows, inner), lambda b, t: (row(b, t), col_blk + 1)),
            full((MLSTM_CONV, inner)),
            full((1, inner)),
            full((MLSTM_HEADS, dh, dh)),
            full((MLSTM_HEADS, dh, dh)),
            full((MLSTM_HEADS, dh, dh)),
            full((3, inner, 2 * LANES)),
            full((1, 2 * LANES)),
            full((1, inner)),
            full((1, inner)),
        ],
        out_specs=pl.BlockSpec((rows, inner), lambda b, t: (row(b, t), 0)),
        out_shape=jax.ShapeDtypeStruct((n, inner), BF16),
        scratch_shapes=[
            pltpu.VMEM((rows + F32_SUBLANES, inner), F32),
            pltpu.VMEM((rows, inner), BF16),
            pltpu.VMEM((rows, inner), BF16),
            pltpu.VMEM((rows, inner), BF16),
            pltpu.VMEM((rows, inner), F32),
            pltpu.VMEM((rows, inner), F32),
            pltpu.VMEM((MLSTM_HEADS, dh, dh + LANES), F32),
            pltpu.VMEM((MLSTM_HEADS, 1, LANES), F32),
        ],
        compiler_params=_params("parallel", "arbitrary"),
        name="mlstm",
    )(proj, proj, conv_w, conv_b, wq_bd, wk_bd, wv_bd, w_if, b_if, skip, g_norm)


def _fox_body(q_ref, qn_ref, k_ref, v_ref, og_ref, ft_ref, bf_ref, o_ref, va_ref, acc_ref, s_ref, fr_ref,
              *, tq, nq, dh, hpb):
    hp = pl.program_id(1)
    qi = pl.program_id(2)

    @pl.when(qi == 0)
    def _():
        for h in range(hpb):
            va_ref[h, :, :dh] = v_ref[:, h * dh:(h + 1) * dh]
            va_ref[h, :, dh:] = jnp.ones((va_ref.shape[1], LANES), BF16)

    @pl.when(jnp.logical_and(qi == 0, hp == 0))
    def _():
        log_f = _log_sigmoid(ft_ref[...] + bf_ref[...])
        tri_u = _tri(tq, upper=True).astype(BF16)
        carry = jnp.zeros((FOX_HEADS, 1), F32)
        for j in range(nq):
            cum = sum(_dot(t, tri_u) for t in _split3(log_f[:, j * tq:(j + 1) * tq])) + carry
            carry = cum[:, tq - 1:tq]
            for r in range(FOX_HEADS):
                fr_ref[j, r] = cum[r:r + 1, :] * LOG2E

    def scaled(ref):
        return [(ref[:, h * dh:(h + 1) * dh].astype(F32) * (dh ** -0.5 * LOG2E)).astype(BF16) for h in range(hpb)]

    qs = scaled(q_ref)
    acc_ref[...] = jnp.zeros_like(acc_ref)
    first_slot = 2

    def scores(q_heads, h, j):
        start = pl.multiple_of(j * tq, tq)
        return _dot_nt(q_heads[h], k_ref[pl.ds(start, tq), h * dh:(h + 1) * dh]) - fr_ref[j, hp * hpb + h]

    def step(j, ms, slot, last):
        start = pl.multiple_of(j * tq, tq)
        out = []
        q_next = scaled(qn_ref) if last else None
        for h in range(hpb):
            s = s_ref[slot, h]
            if last:
                s = jnp.where(_tri(tq), s, -jnp.inf)
                s_ref[first_slot, h] = scores(q_next, h, 0)
            else:
                s_ref[1 if slot == first_slot else 1 - slot, h] = scores(qs, h, j + 1)
            m_new = jnp.maximum(ms[h], jnp.max(s, axis=-1, keepdims=True))
            p = jnp.exp2(s - m_new).astype(BF16)
            acc_ref[h] = jnp.exp2(ms[h] - m_new) * acc_ref[h] + _dot(p, va_ref[h, pl.ds(start, tq), :])
            out.append(m_new)
        return tuple(out)

    def finish():
        for h in range(hpb):
            acc = acc_ref[h]
            gate = _sigmoid(og_ref[:, h * dh:(h + 1) * dh].astype(F32))
            o_ref[:, h * dh:(h + 1) * dh] = (acc[:, :dh] / acc[:, dh:] * gate).astype(BF16)

    m0 = tuple(jnp.full((tq, 1), -jnp.inf, F32) for _ in range(hpb))

    @pl.when(qi == 0)
    def _():
        for h in range(hpb):
            s_ref[0, h] = scores(qs, h, 0)
        step(0, m0, 0, True)
        finish()

    @pl.when(qi > 0)
    def _():
        def steps(first, count, ms):
            for u in range(count):
                ms = step(first + u, ms, (1 + u) % 2, False)
            return ms

        ms = step(0, m0, first_slot, False)
        unroll = FOX_UNROLL
        n_mid = qi - 1
        ms = lax.fori_loop(0, n_mid // unroll, lambda t, ms: steps(1 + unroll * t, unroll, ms), ms)

        for rem in range(unroll):
            @pl.when(n_mid % unroll == rem)
            def _():
                step(qi, steps(qi - rem, rem, ms), (1 + rem) % 2, True)
                finish()


def _fox(proj, small_t, b_f, bsz, seq, dh, q_blk, k_blk, v_blk, og_blk, f_blk):
    n = proj.shape[0]
    tq = min(FOX_TQ, seq)
    nq = seq // tq
    hpb = FOX_HEADS_PER_STEP
    w = hpb * dh
    assert FOX_HEADS % hpb == 0 and q_blk % hpb == 0 and k_blk % hpb == 0 and v_blk % hpb == 0 and og_blk % hpb == 0
    return pl.pallas_call(
        functools.partial(_fox_body, tq=tq, nq=nq, dh=dh, hpb=hpb),
        grid=(bsz, FOX_HEADS // hpb, nq),
        in_specs=[
            pl.BlockSpec((tq, w), lambda b, h, i: (b * nq + i, q_blk // hpb + h)),
            pl.BlockSpec((tq, w), lambda b, h, i: (b * nq + jnp.minimum(i + 1, nq - 1), q_blk // hpb + h)),
            pl.BlockSpec((seq, w), lambda b, h, i: (b, k_blk // hpb + h)),
            pl.BlockSpec((seq, w), lambda b, h, i: (b, v_blk // hpb + h)),
            pl.BlockSpec((tq, w), lambda b, h, i: (b * nq + i, og_blk // hpb + h)),
            pl.BlockSpec((FOX_HEADS, seq), lambda b, h, i: (f_blk, b)),
            pl.BlockSpec((FOX_HEADS, 1), lambda b, h, i: (0, 0)),
        ],
        out_specs=pl.BlockSpec((tq, w), lambda b, h, i: (b * nq + i, h)),
        out_shape=jax.ShapeDtypeStruct((n, FOX_HEADS * dh), BF16),
        scratch_shapes=[pltpu.VMEM((hpb, seq, dh + LANES), BF16), pltpu.VMEM((hpb, tq, dh + LANES), F32),
                        pltpu.VMEM((3, hpb, tq, tq), F32), pltpu.VMEM((nq, FOX_HEADS, 1, tq), F32)],
        compiler_params=_params("parallel", "arbitrary", "arbitrary"),
        name="fox",
    )(proj, proj, proj, proj, proj, small_t, b_f)


def _merge_body(x_ref, y0_ref, y1_ref, y2_ref, gt_ref, bg_ref, wb_ref, wo_ref, o_ref, *, d):
    merged = None
    for n, y_ref in enumerate((y0_ref, y1_ref, y2_ref)):
        cs = slice(n * d, (n + 1) * d)
        gate = _sigmoid(gt_ref[:, cs].astype(F32) + bg_ref[:, cs])
        term = _dot(y_ref[...], wb_ref[n]) * gate
        merged = term if merged is None else merged + term
    o_ref[...] = x_ref[...] + _dot(merged.astype(BF16), wo_ref[...])


def _merge(x2, y_gla, y_mlstm, y_fox, proj, b_gate, w_branch, w_out, gate_blk, layer):
    n, d = x2.shape
    tm = min(MERGE_TM, n)
    rowblk = lambda shape: pl.BlockSpec(shape, lambda i: (i, 0))
    return pl.pallas_call(
        functools.partial(_merge_body, d=d),
        grid=(n // tm,),
        in_specs=[
            rowblk((tm, d)), rowblk((tm, d)), rowblk((tm, d)), rowblk((tm, d)),
            pl.BlockSpec((tm, N_BRANCH * d), lambda i: (i, gate_blk)),
            pl.BlockSpec((1, N_BRANCH * d), lambda i: (0, 0)),
            pl.BlockSpec((None, N_BRANCH, d, d), lambda i: (layer, 0, 0, 0), pipeline_mode=pl.Buffered(1)),
            pl.BlockSpec((None, d, d), lambda i: (layer, 0, 0), pipeline_mode=pl.Buffered(1)),
        ],
        out_specs=rowblk((tm, d)),
        out_shape=jax.ShapeDtypeStruct((n, d), F32),
        compiler_params=_params("parallel"),
        name="merge",
    )(x2, y_gla, y_mlstm, y_fox, proj, b_gate, w_branch, w_out)


def _ffn_body(x_ref, xp_ref, g_ref, wu_ref, cw_ref, cb_ref, wd_ref, gf_ref, o_ref, h_ref,
              *, tm, halo, blocks_per_seq, dff, chunk, final):
    i = pl.program_id(0)

    def norm(x, gain):
        return x * lax.rsqrt(jnp.mean(x * x, axis=-1, keepdims=True) + EPS) * gain

    x = x_ref[...]
    hp = norm(xp_ref[...], g_ref[...]).astype(BF16)
    h_ref[0:halo, :] = jnp.where(i % blocks_per_seq != 0, hp, jnp.zeros_like(hp))
    h_ref[halo:halo + tm, :] = norm(x, g_ref[...]).astype(BF16)
    h = h_ref[...]

    def conv(lo, hi):
        u = _dot(h, wu_ref[:, lo:hi])
        y = cb_ref[:, lo:hi]
        for j in range(FFN_CONV - 1):
            y = y + cw_ref[j:j + 1, lo:hi] * pltpu.roll(u, FFN_CONV - 1 - j, 0)[halo:, :]
        return y + cw_ref[FFN_CONV - 1:FFN_CONV, lo:hi] * u[halo:, :]

    acc = x
    for lo in range(0, dff, chunk):
        hi = min(lo + chunk, dff)
        act = _silu(conv(dff + lo, dff + hi)) * conv(lo, hi)
        acc = acc + _dot(act.astype(BF16), wd_ref[lo:hi, :])
    o_ref[...] = norm(acc, gf_ref[...]) if final else acc


def _ffn(x2, g, w_up, conv_w, conv_b, w_down, seq, layer, final_gain=None):
    n, d = x2.shape
    dff = w_down.shape[1]
    tm = min(FFN_TM, seq)
    halo = BF16_SUBLANES
    hb = tm // halo
    final = final_gain is not None
    resident = lambda shape: pl.BlockSpec(shape, lambda i: (0,) * len(shape), pipeline_mode=pl.Buffered(1))
    return pl.pallas_call(
        functools.partial(_ffn_body, tm=tm, halo=halo, blocks_per_seq=seq // tm, dff=dff, chunk=FFN_CHUNK,
                          final=final),
        grid=(n // tm,),
        in_specs=[
            pl.BlockSpec((tm, d), lambda i: (i, 0)),
            pl.BlockSpec((halo, d), lambda i: (jnp.maximum(i * hb - 1, 0), 0)),
            resident((1, d)),
            pl.BlockSpec((None, d, 2 * dff), lambda i: (layer, 0, 0), pipeline_mode=pl.Buffered(1)),
            resident((FFN_CONV, 2 * dff)),
            resident((1, 2 * dff)),
            pl.BlockSpec((None, dff, d), lambda i: (layer, 0, 0), pipeline_mode=pl.Buffered(1)),
            resident((1, d)),
        ],
        out_specs=pl.BlockSpec((tm, d), lambda i: (i, 0)),
        out_shape=jax.ShapeDtypeStruct((n, d), F32),
        scratch_shapes=[pltpu.VMEM((halo + tm, d), BF16)],
        compiler_params=_params("parallel"),
        name="ffn_final" if final else "ffn",
    )(x2, x2, g, w_up, conv_w, conv_b, w_down, final_gain if final else g)


def _block_diag_heads(w, n_heads):
    nblk, bc, bd = w.shape
    per = nblk // n_heads
    tiled = jnp.tile(w.reshape(n_heads, per * bc, bd), (1, 1, per))
    same_block = (jnp.arange(per * bc)[:, None] // bc) == (jnp.arange(per * bd)[None, :] // bd)
    return jnp.where(same_block, tiled, 0.0)


def _pad_lanes(a, width):
    return jnp.pad(a, ((0, 0),) * (a.ndim - 1) + ((0, width - a.shape[-1]),))


def kernel(x, norm_mix, w_in, b_gate, gla_w_lr, gla_b_lr, gla_norm, mlstm_conv_w, mlstm_conv_b, mlstm_wq, mlstm_wk,
           mlstm_wv, mlstm_w_i, mlstm_b_i, mlstm_w_f, mlstm_b_f, mlstm_skip, mlstm_norm, fox_b_f, w_branch, w_out,
           norm_ffn, ffn_w_up, ffn_conv_w, ffn_conv_b, ffn_w_down, norm_final):
    bsz, seq, d = x.shape
    depth = w_in.shape[0]
    hdk = gla_w_lr.shape[2]
    hdv = gla_norm.shape[1]
    inner = mlstm_conv_w.shape[2]
    fox_w = w_branch.shape[2]
    fox_dh = fox_w // FOX_HEADS
    dff = ffn_w_down.shape[1]
    assert hdv == d and inner == d and fox_w == d and 2 * hdk == d
    for tile in (GLA_ROWS, MLSTM_ROWS, FOX_TQ, FFN_TM):
        assert seq % min(tile, seq) == 0
    assert (bsz * seq) % min(INPROJ_TM, bsz * seq) == 0 and (bsz * seq) % min(MERGE_TM, bsz * seq) == 0
    assert min(GLA_ROWS, seq) % GLA_CHUNK == 0 and min(MLSTM_ROWS, seq) % min(MLSTM_CHUNK, seq) == 0

    o_glr = 2 * hdk + hdv
    o_gr = o_glr + GLA_RANK
    o_ff = o_gr + hdv + 2 * inner + 3 * fox_w
    o_fog = o_ff + FOX_HEADS
    blk = {"gq": 0, "gk": 1, "gv": 1, "gr": 2, "mx": 3, "fq": 5 * d // fox_dh, "fk": 6 * d // fox_dh,
           "fv": 7 * d // fox_dh, "fog": 8 * d // fox_dh, "gates": 3}

    x2 = x.reshape(bsz * seq, d)
    w_in_bf = w_in.astype(BF16)
    w_big = jnp.concatenate([w_in_bf[..., :o_glr], w_in_bf[..., o_gr:o_ff], w_in_bf[..., o_fog:]], axis=-1)
    w_small = _pad_lanes(jnp.concatenate([w_in_bf[..., o_glr:o_gr], w_in_bf[..., o_ff:o_fog]], axis=-1), SMALL_W)
    w_branch_bf, w_out_bf = w_branch.astype(BF16), w_out.astype(BF16)
    ffn_w_up_bf, ffn_w_down_bf = ffn_w_up.astype(BF16), ffn_w_down.astype(BF16)
    for l in range(depth):
        proj, small, small_t = _inproj(x2, norm_mix[l][None, :], w_big, w_small, l)

        w_lr_pad = jnp.pad(gla_w_lr[l], ((0, SMALL_W - GLA_RANK), (0, 0))).astype(BF16)
        y_gla = _gla(proj, small, w_lr_pad, gla_b_lr[l][None, :], gla_norm[l][None, :], bsz, seq)

        w_if = jnp.concatenate([_pad_lanes(mlstm_w_i[l], LANES), _pad_lanes(mlstm_w_f[l], LANES)], axis=1)
        b_if = jnp.concatenate([_pad_lanes(mlstm_b_i[l][None, :], LANES), _pad_lanes(mlstm_b_f[l][None, :], LANES)], axis=1)
        q_unscale = jnp.array([(inner // MLSTM_HEADS) ** 0.5, 1.0, 1.0], F32)[:, None, None]
        w_if3 = (w_if.reshape(3, inner, 2 * LANES) * q_unscale).astype(BF16)
        y_mlstm = _mlstm(
            proj, mlstm_conv_w[l], mlstm_conv_b[l][None, :],
            _block_diag_heads(mlstm_wq[l], MLSTM_HEADS).astype(BF16),
            _block_diag_heads(mlstm_wk[l], MLSTM_HEADS).astype(BF16),
            _block_diag_heads(mlstm_wv[l], MLSTM_HEADS).astype(BF16),
            w_if3, b_if, mlstm_skip[l][None, :], mlstm_norm[l][None, :], bsz, seq, blk["mx"])

        y_fox = _fox(proj, small_t, fox_b_f[l][:, None], bsz, seq, fox_dh,
                     blk["fq"], blk["fk"], blk["fv"], blk["fog"], GLA_RANK // FOX_HEADS)

        x2 = _merge(x2, y_gla, y_mlstm, y_fox, proj, b_gate[l].reshape(1, N_BRANCH * d),
                    w_branch_bf, w_out_bf, blk["gates"], l)

        x2 = _ffn(x2, norm_ffn[l][None, :], ffn_w_up_bf, ffn_conv_w[l], ffn_conv_b[l][None, :],
                  ffn_w_down_bf, seq, l, final_gain=norm_final[None, :] if l == depth - 1 else None)
    return x2.reshape(bsz, seq, d)
```
